```python
import jax, jax.numpy as jnp
from jax import lax
import numpy as np

D_MODEL = 2048
BATCH = 4
SEQ = 8192
DEPTH = 1
DEC_BATCH = 16
DEC_SEQ = 16
PAST_LEN = 1024

CHUNK = 64
Q_BLOCK = 128
EPS = 1e-6
ROPE_THETA = 500000.0
MIX_WIDTH = D_MODEL
A_WIDTH = MIX_WIDTH // 2
A_HEAD_DIM = 128
A_HEADS = A_WIDTH // A_HEAD_DIM
A_KV_HEADS = 2
A_GROUP = A_HEADS // A_KV_HEADS
A_ROT = A_HEAD_DIM // 4
IDX_HEADS = 8
IDX_DIM = 64
IDX_ROT = IDX_DIM // 4
TOPK_MAX = 256
M_WIDTH = MIX_WIDTH - A_WIDTH
M_HEAD_DIM = 128
M_HEADS = M_WIDTH // M_HEAD_DIM
CONV_W = 4
D_FF = 4 * D_MODEL
IN_SPLITS = (A_WIDTH, A_KV_HEADS * A_HEAD_DIM, A_KV_HEADS * A_HEAD_DIM, IDX_HEADS * IDX_DIM, IDX_DIM, IDX_HEADS, M_WIDTH, M_WIDTH, M_WIDTH, M_HEADS, M_HEADS)
IN_WIDTH = A_WIDTH + 2 * A_KV_HEADS * A_HEAD_DIM + IDX_HEADS * IDX_DIM + IDX_DIM + IDX_HEADS + 3 * M_WIDTH + 2 * M_HEADS

kernel_name = 'hybrid_dsa_mlstm_stream'


def _rmsnorm(x, g):
    xf = x.astype(jnp.float32)
    r = lax.rsqrt(jnp.mean(xf * xf, axis=-1, keepdims=True) + EPS)
    return (xf * r * g.astype(jnp.float32)).astype(x.dtype)


def _partial_rope(x, pos, rot):
    half = rot // 2
    inv_freq = ROPE_THETA ** (-jnp.arange(half, dtype=jnp.float32) / half)
    ang = pos.astype(jnp.float32)[:, None] * inv_freq[None, :]
    cos = jnp.cos(ang)[:, None, :]
    sin = jnp.sin(ang)[:, None, :]
    xr = x[..., :rot].astype(jnp.float32)
    x1, x2 = xr[..., :half], xr[..., half:]
    rotated = jnp.concatenate([x1 * cos - x2 * sin, x2 * cos + x1 * sin], axis=-1).astype(x.dtype)
    return jnp.concatenate([rotated, x[..., rot:]], axis=-1)


def _project(x, pos, norm1_g, w_in, q_norm_g, k_norm_g):
    B, T, _ = x.shape
    z = _rmsnorm(x, norm1_g) @ w_in
    offs = [int(o) for o in np.cumsum(IN_SPLITS)[:-1]]
    q, k, v, qi, ki, wi, u, vm, om, ig, fg = jnp.split(z, offs, axis=-1)
    q = _partial_rope(_rmsnorm(q.reshape(B, T, A_HEADS, A_HEAD_DIM), q_norm_g), pos, A_ROT)
    k = _partial_rope(_rmsnorm(k.reshape(B, T, A_KV_HEADS, A_HEAD_DIM), k_norm_g), pos, A_ROT)
    v = v.reshape(B, T, A_KV_HEADS, A_HEAD_DIM)
    qi = _partial_rope(qi.reshape(B, T, IDX_HEADS, IDX_DIM), pos, IDX_ROT)
    ki = _partial_rope(ki.reshape(B, T, 1, IDX_DIM), pos, IDX_ROT)[:, :, 0]
    return q, k, v, qi, ki, wi, u, vm, om, ig, fg


def _sparse_attend(q, q_idx, w_idx, limit, k_all, v_all, kidx_all, n_sel):
    B, Q = q.shape[0], q.shape[1]
    S = k_all.shape[1]
    dots = jnp.einsum('bqhd,bsd->bqhs', q_idx.astype(jnp.float32), kidx_all.astype(jnp.float32)) * IDX_DIM ** -0.5
    w = w_idx.astype(jnp.float32) * IDX_HEADS ** -0.5
    score = jnp.einsum('bqh,bqhs->bqs', w, jax.nn.relu(dots))
    admissible = jnp.arange(S)[None, :] < limit[:, None]
    score = jnp.where(admissible[None], score, -jnp.inf)
    _, sel = lax.top_k(score, n_sel)
    valid = sel < limit[None, :, None]
    bidx = jnp.arange(B)[:, None, None]
    k_sel = k_all[bidx, sel].astype(jnp.float32)
    v_sel = v_all[bidx, sel].astype(jnp.float32)
    qg = q.reshape(B, Q, A_KV_HEADS, A_GROUP, A_HEAD_DIM).astype(jnp.float32)
    logits = jnp.einsum('bqgrd,bqkgd->bqgrk', qg, k_sel) * A_HEAD_DIM ** -0.5
    logits = jnp.where(valid[:, :, None, None, :], logits, -jnp.inf)
    probs = jax.nn.softmax(logits, axis=-1)
    out = jnp.einsum('bqgrk,bqkgd->bqgrd', probs, v_sel)
    return out.reshape(B, Q, A_WIDTH).astype(q.dtype)


def _causal_conv(u, prev, conv_w, conv_b):
    T = u.shape[1]
    up = jnp.concatenate([prev.astype(u.dtype), u], axis=1)
    out = conv_b
    for i in range(CONV_W):
        out = out + up[:, i:i + T] * conv_w[i]
    return out, up[:, up.shape[1] - (CONV_W - 1):]


def _mlstm_inputs(uc, vm, ig_pre, f_pre, wq_m, wk_m, b_igate, b_fgate):
    B, T, _ = uc.shape
    uh = jax.nn.silu(uc).reshape(B, T, M_HEADS, M_HEAD_DIM)
    q = jnp.einsum('bthd,hde->bthe', uh, wq_m).astype(jnp.float32)
    k = (jnp.einsum('bthd,hde->bthe', uh, wk_m) * M_HEAD_DIM ** -0.5).astype(jnp.float32)
    v = vm.reshape(B, T, M_HEADS, M_HEAD_DIM).astype(jnp.float32)
    log_i = (ig_pre + b_igate).astype(jnp.float32)
    log_f = jax.nn.log_sigmoid((f_pre + b_fgate).astype(jnp.float32))
    return q, k, v, log_i, log_f


def _mlstm_chunk(state, inp):
    C, n, m = state
    q, k, v, log_i, log_f = inp
    L = q.shape[1]
    b = jnp.cumsum(log_f, axis=1).transpose(0, 2, 1)
    li = log_i.transpose(0, 2, 1)
    causal = jnp.tril(jnp.ones((L, L), dtype=bool))
    D = jnp.where(causal, b[..., :, None] - b[..., None, :] + li[..., None, :], -jnp.inf)
    inter = b + m[..., None]
    m_t = jnp.maximum(inter, jnp.max(D, axis=-1))
    sw = jnp.exp(D - m_t[..., None]) * jnp.einsum('bthd,bshd->bhts', q, k)
    sp = jnp.exp(inter - m_t)
    num = sp[..., None] * jnp.einsum('bthd,bhde->bhte', q, C) + jnp.einsum('bhts,bshe->bhte', sw, v)
    den = sp * jnp.einsum('bthd,bhd->bht', q, n) + jnp.sum(sw, axis=-1)
    h = num / jnp.maximum(jnp.abs(den), jnp.exp(-m_t))[..., None]
    bL = b[..., -1]
    dec = bL[..., None] - b + li
    m_new = jnp.maximum(bL + m, jnp.max(dec, axis=-1))
    wts = jnp.exp(dec - m_new[..., None])
    sc = jnp.exp(bL + m - m_new)
    C_new = sc[..., None, None] * C + jnp.einsum('bhs,bshd,bshe->bhde', wts, k, v)
    n_new = sc[..., None] * n + jnp.einsum('bhs,bshd->bhd', wts, k)
    return (C_new, n_new, m_new), h.transpose(0, 2, 1, 3)


def _mlstm_output(h, o_pre, hnorm_g):
    B, T = h.shape[0], h.shape[1]
    hn = _rmsnorm(h, hnorm_g.reshape(M_HEADS, M_HEAD_DIM)).reshape(B, T, M_WIDTH)
    return (jax.nn.sigmoid(o_pre.astype(jnp.float32)) * hn).astype(o_pre.dtype)


def _finish(x, a_out, m_out, w_out, norm2_g, w_up, w_down):
    h = x + jnp.concatenate([a_out, m_out], axis=-1) @ w_out
    f = jnp.square(jax.nn.relu(_rmsnorm(h, norm2_g) @ w_up))
    return h + f @ w_down


def _layer_prompt(x, p):
    norm1_g, w_in, q_norm_g, k_norm_g, conv_w, conv_b, wq_m, wk_m, b_igate, b_fgate, hnorm_g, w_out, norm2_g, w_up, w_down = p
    B, T, _ = x.shape
    pos = jnp.arange(T)
    q, k, v, qi, ki, wi, u, vm, om, ig, fg = _project(x, pos, norm1_g, w_in, q_norm_g, k_norm_g)
    n_sel = min(TOPK_MAX, T // 4)
    nb = T // Q_BLOCK

    def blocks(a):
        return a.reshape((B, nb, Q_BLOCK) + a.shape[2:]).swapaxes(0, 1)

    def attend_block(args):
        qb, qib, wb, start = args
        limit = ((start + jnp.arange(Q_BLOCK)) // CHUNK + 1) * CHUNK
        return _sparse_attend(qb, qib, wb, limit, k, v, ki, n_sel)

    a = lax.map(attend_block, (blocks(q), blocks(qi), blocks(wi), jnp.arange(nb) * Q_BLOCK))
    a = a.swapaxes(0, 1).reshape(B, T, A_WIDTH)
    uc, conv_state = _causal_conv(u, jnp.zeros((B, CONV_W - 1, M_WIDTH), u.dtype), conv_w, conv_b)
    mq, mk, mv, mli, mlf = _mlstm_inputs(uc, vm, ig, fg, wq_m, wk_m, b_igate, b_fgate)
    nc = T // CHUNK

    def chunks(a_):
        return a_.reshape((B, nc, CHUNK) + a_.shape[2:]).swapaxes(0, 1)

    init = (jnp.zeros((B, M_HEADS, M_HEAD_DIM, M_HEAD_DIM), jnp.float32),
            jnp.zeros((B, M_HEADS, M_HEAD_DIM), jnp.float32),
            jnp.zeros((B, M_HEADS), jnp.float32))
    (C, n, m), h = lax.scan(_mlstm_chunk, init, (chunks(mq), chunks(mk), chunks(mv), chunks(mli), chunks(mlf)))
    h = h.swapaxes(0, 1).reshape(B, T, M_HEADS, M_HEAD_DIM)
    y = _finish(x, a, _mlstm_output(h, om, hnorm_g), w_out, norm2_g, w_up, w_down)
    return y, (k, v, ki, C, n, m, conv_state)


def _layer_sample(x, ck, cv, ckidx, sC, sn, sm, sconv, p):
    norm1_g, w_in, q_norm_g, k_norm_g, conv_w, conv_b, wq_m, wk_m, b_igate, b_fgate, hnorm_g, w_out, norm2_g, w_up, w_down = p
    B, T, _ = x.shape
    P = ck.shape[1]
    pos = P + jnp.arange(T)
    q, k, v, qi, ki, wi, u, vm, om, ig, fg = _project(x, pos, norm1_g, w_in, q_norm_g, k_norm_g)
    k_all = jnp.concatenate([ck.astype(k.dtype), k], axis=1)
    v_all = jnp.concatenate([cv.astype(v.dtype), v], axis=1)
    ki_all = jnp.concatenate([ckidx.astype(ki.dtype), ki], axis=1)
    L = P + T
    n_sel = min(TOPK_MAX, L // 4)
    limit = jnp.full((T,), L, dtype=jnp.int32)
    a = _sparse_attend(q, qi, wi, limit, k_all, v_all, ki_all, n_sel)
    uc, conv_state = _causal_conv(u, sconv, conv_w, conv_b)
    mq, mk, mv, mli, mlf = _mlstm_inputs(uc, vm, ig, fg, wq_m, wk_m, b_igate, b_fgate)
    state0 = (sC.astype(jnp.float32), sn.astype(jnp.float32), sm.astype(jnp.float32))
    (C, n, m), h = _mlstm_chunk(state0, (mq, mk, mv, mli, mlf))
    y = _finish(x, a, _mlstm_output(h, om, hnorm_g), w_out, norm2_g, w_up, w_down)
    return y, (k, v, ki, C, n, m, conv_state)


def setup_inputs(seed: int = 0) -> dict:
    key = jax.random.key(seed)
    ks = jax.random.split(key, 24)
    f32 = jnp.float32

    def nrm(k, shape, scale=1.0):
        return jax.random.normal(k, shape, f32) * scale

    def gain(k, shape):
        return 1.0 + 0.02 * jax.random.normal(k, shape, f32)

    return {
        'x_prompt': nrm(ks[0], (BATCH, SEQ, D_MODEL)),
        'x_sample': nrm(ks[1], (DEC_BATCH, DEC_SEQ, D_MODEL)),
        'cache_k': nrm(ks[2], (DEPTH, DEC_BATCH, PAST_LEN, A_KV_HEADS, A_HEAD_DIM)),
        'cache_v': nrm(ks[3], (DEPTH, DEC_BATCH, PAST_LEN, A_KV_HEADS, A_HEAD_DIM)),
        'cache_kidx': nrm(ks[4], (DEPTH, DEC_BATCH, PAST_LEN, IDX_DIM)),
        'state_mlstm_C': nrm(ks[5], (DEPTH, DEC_BATCH, M_HEADS, M_HEAD_DIM, M_HEAD_DIM), 0.1),
        'state_mlstm_n': nrm(ks[6], (DEPTH, DEC_BATCH, M_HEADS, M_HEAD_DIM), 0.1),
        'state_mlstm_m': nrm(ks[7], (DEPTH, DEC_BATCH, M_HEADS), 0.5),
        'state_conv': nrm(ks[8], (DEPTH, DEC_BATCH, CONV_W - 1, M_WIDTH)),
        'norm1_g': gain(ks[9], (DEPTH, D_MODEL)),
        'w_in': nrm(ks[10], (DEPTH, D_MODEL, IN_WIDTH), D_MODEL ** -0.5),
        'q_norm_g': gain(ks[11], (DEPTH, A_HEAD_DIM)),
        'k_norm_g': gain(ks[12], (DEPTH, A_HEAD_DIM)),
        'conv_w': nrm(ks[13], (DEPTH, CONV_W, M_WIDTH), CONV_W ** -0.5),
        'conv_b': nrm(ks[14], (DEPTH, M_WIDTH), 0.01),
        'wq_m': nrm(ks[15], (DEPTH, M_HEADS, M_HEAD_DIM, M_HEAD_DIM), M_HEAD_DIM ** -0.5),
        'wk_m': nrm(ks[16], (DEPTH, M_HEADS, M_HEAD_DIM, M_HEAD_DIM), M_HEAD_DIM ** -0.5),
        'b_igate': nrm(ks[17], (DEPTH, M_HEADS), 0.1),
        'b_fgate': 3.0 + nrm(ks[18], (DEPTH, M_HEADS), 0.5),
        'hnorm_g': gain(ks[19], (DEPTH, M_WIDTH)),
        'w_out': nrm(ks[20], (DEPTH, MIX_WIDTH, D_MODEL), MIX_WIDTH ** -0.5),
        'norm2_g': gain(ks[21], (DEPTH, D_MODEL)),
        'w_up': nrm(ks[22], (DEPTH, D_MODEL, D_FF), D_MODEL ** -0.5),
        'w_down': nrm(ks[23], (DEPTH, D_FF, D_MODEL), D_FF ** -0.5),
    }


def reference(x_prompt, x_sample, cache_k, cache_v, cache_kidx, state_mlstm_C, state_mlstm_n, state_mlstm_m, state_conv, norm1_g, w_in, q_norm_g, k_norm_g, conv_w, conv_b, wq_m, wk_m, b_igate, b_fgate, hnorm_g, w_out, norm2_g, w_up, w_down):
    y_prompt = x_prompt
    y_sample = x_sample
    new_p = []
    new_s = []
    for l in range(DEPTH):
        p = (norm1_g[l], w_in[l], q_norm_g[l], k_norm_g[l], conv_w[l], conv_b[l], wq_m[l], wk_m[l],
             b_igate[l], b_fgate[l], hnorm_g[l], w_out[l], norm2_g[l], w_up[l], w_down[l])
        y_prompt, st_p = _layer_prompt(y_prompt, p)
        y_sample, st_s = _layer_sample(y_sample, cache_k[l], cache_v[l], cache_kidx[l], state_mlstm_C[l],
                                       state_mlstm_n[l], state_mlstm_m[l], state_conv[l], p)
        new_p.append(st_p)
        new_s.append(st_s)

    def stack(states, i):
        return jnp.stack([s[i] for s in states])

    return (y_prompt, y_sample,
            stack(new_p, 0), stack(new_p, 1), stack(new_p, 2), stack(new_p, 3), stack(new_p, 4), stack(new_p, 5), stack(new_p, 6),
            stack(new_s, 0), stack(new_s, 1), stack(new_s, 2), stack(new_s, 3), stack(new_s, 4), stack(new_s, 5), stack(new_s, 6))
```

```python
import functools

import jax
import jax.numpy as jnp
import numpy as np
from jax import lax
from jax.experimental import pallas as pl
from jax.experimental.pallas import tpu as pltpu

F32 = jnp.float32
BF16 = jnp.bfloat16
I32 = jnp.int32

EPS = 1e-6
ROPE_THETA = 500000.0
CHUNK = 64
CHUNK_LOG2 = 6
TOPK_MAX = 256
A_HEAD_DIM = 128
A_KV_HEADS = 2
IDX_HEADS = 8
IDX_DIM = 64
M_HEAD_DIM = 128
CONV_W = 4

LANES = 128
SUBLANES = 8
VMEM_LIMIT = 56 * 1024 * 1024
NEG_BIG = -1e30
INT_MIN = -(2 ** 31)
BIG_J = 2 ** 30


def _rms(x, g):
    r = lax.rsqrt(jnp.mean(x * x, axis=-1, keepdims=True) + EPS)
    return x * r * g


def _rope(t, c, sa, sb, half):
    return t * c + pltpu.roll(t, LANES - half, 1) * sa + pltpu.roll(t, half, 1) * sb


def _sigmoid(x):
    return 1.0 / (1.0 + jnp.exp(-x))


def _dot(a, b):
    return jnp.dot(a, b, preferred_element_type=F32)


def _dot_t(a, b):
    return lax.dot_general(a, b, (((1,), (1,)), ((), ())), preferred_element_type=F32)


def _proj_kernel(x_ref, g1_ref, wa_ref, wb_ref, qg_ref, kg_ref, ra_ref, ri_ref,
                 q_out, k_out, v_out, ki_out, kb_out, vb_out, kiab_out, qi_out,
                 wi_out, ig_out, fg_out, u_out, vm_out, om_out, *, n_q_heads):
    xn = _rms(x_ref[...], g1_ref[...]).astype(BF16)
    ra = ra_ref[...]
    ca, saa, sba = ra[:, :LANES], ra[:, LANES:2 * LANES], ra[:, 2 * LANES:]
    ri = ri_ref[...]
    ci, sai, sbi = ri[:, :LANES], ri[:, LANES:2 * LANES], ri[:, 2 * LANES:]
    qg = qg_ref[...]
    kg = kg_ref[...]
    aw = n_q_heads * A_HEAD_DIM
    kvw = A_KV_HEADS * A_HEAD_DIM
    q_scale = A_HEAD_DIM ** -0.5
    i_scale = IDX_DIM ** -0.5

    for c in range(0, aw, 512):
        z = _dot(xn, wa_ref[:, c:c + 512])
        for j in range(0, 512, LANES):
            qh = _rope(_rms(z[:, j:j + LANES], qg), ca, saa, sba, A_HEAD_DIM // 8) * q_scale
            q_out[:, c + j:c + j + LANES] = qh.astype(BF16)
    off = aw
    z = _dot(xn, wa_ref[:, off:off + 2 * kvw])
    for j in range(0, kvw, LANES):
        kh = _rope(_rms(z[:, j:j + LANES], kg), ca, saa, sba, A_HEAD_DIM // 8)
        k_out[:, j:j + LANES] = kh
        kb_out[:, j:j + LANES] = kh.astype(BF16)
    vv = z[:, kvw:]
    v_out[...] = vv
    vb_out[...] = vv.astype(BF16)
    off += 2 * kvw
    iw = IDX_HEADS * IDX_DIM
    z = _dot(xn, wa_ref[:, off:off + iw])
    for j in range(0, iw, LANES):
        qi = _rope(z[:, j:j + LANES], ci, sai, sbi, IDX_DIM // 8) * i_scale
        qi_out[:, j:j + LANES] = qi.astype(BF16)
    off += iw
    z = _dot(xn, wa_ref[:, off:off + 4 * LANES])
    ki2 = _rope(z[:, :LANES], ci, sai, sbi, IDX_DIM // 8)
    ki_out[...] = ki2[:, :IDX_DIM]
    lane = lax.broadcasted_iota(I32, (1, LANES), 1)
    lo = lane < IDX_DIM
    kiab_out[:, :LANES] = jnp.where(lo, ki2, 0.0).astype(BF16)
    kiab_out[:, LANES:] = jnp.where(lo, 0.0, ki2).astype(BF16)
    wi_out[...] = z[:, LANES:2 * LANES]
    ig_out[...] = z[:, 2 * LANES:3 * LANES]
    fg_out[...] = z[:, 3 * LANES:]
    mw = u_out.shape[1]
    for c in range(0, mw, 512):
        u_out[:, c:c + 512] = _dot(xn, wb_ref[:, c:c + 512])
        vm_out[:, c:c + 512] = _dot(xn, wb_ref[:, mw + c:mw + c + 512]).astype(BF16)
        om_out[:, c:c + 512] = _dot(xn, wb_ref[:, 2 * mw + c:2 * mw + c + 512])


def _proj(x, g1, wa, wb, qg, kg, ra, ri, *, tm, n_q_heads, m_width):
    m, d = x.shape
    n_tab = ra.shape[0] // tm
    aw = n_q_heads * A_HEAD_DIM
    kvw = A_KV_HEADS * A_HEAD_DIM
    iw = IDX_HEADS * IDX_DIM
    row = lambda w: pl.BlockSpec((tm, w), lambda i: (i, 0))
    const = lambda a: pl.BlockSpec(a.shape, lambda i: (0, 0), pipeline_mode=pl.Buffered(1))
    tab = pl.BlockSpec((tm, 3 * LANES), lambda i: (i % n_tab, 0))
    outs = [
        (aw, BF16), (kvw, F32), (kvw, F32), (IDX_DIM, F32), (kvw, BF16), (kvw, BF16),
        (2 * LANES, BF16), (iw, BF16), (LANES, F32), (LANES, F32), (LANES, F32),
        (m_width, F32), (m_width, BF16), (m_width, F32),
    ]
    return pl.pallas_call(
        functools.partial(_proj_kernel, n_q_heads=n_q_heads),
        grid=(m // tm,),
        in_specs=[row(d), const(g1), const(wa), const(wb), const(qg), const(kg), tab, tab],
        out_specs=[row(w) for w, _ in outs],
        out_shape=[jax.ShapeDtypeStruct((m, w), dt) for w, dt in outs],
        compiler_params=pltpu.CompilerParams(
            dimension_semantics=("arbitrary",), vmem_limit_bytes=VMEM_LIMIT),
        name="proj",
    )(x, g1, wa, wb, qg, kg, ra, ri)


def _attn_kernel(q_ref, qi_ref, wi_ref, k_ref, v_ref, ki_ref, o_ref,
                 key_scr, wb_scr, qs_scr, t_scr, j_scr, m_scr, l_scr, acc_scr,
                 *, tq, tk, causal, s_valid, n_sel, n_tiles, group):
    start = pl.program_id(1) * tq
    nl = tk // LANES
    row = lax.broadcasted_iota(I32, (tq, LANES), 0)
    lane = lax.broadcasted_iota(I32, (tq, LANES), 1)
    if causal:
        limit = (lax.shift_right_logical(start + row, CHUNK_LOG2) + 1) * CHUNK
        nkt = jnp.minimum((start + tq + tk - 1) // tk, n_tiles)
    else:
        limit = jnp.full((tq, LANES), s_valid, I32)
        nkt = n_tiles

    w = wi_ref[0][:, :IDX_HEADS] * (IDX_HEADS ** -0.5)
    for h in range(IDX_HEADS):
        wb_scr[h] = jnp.broadcast_to(w[:, h:h + 1], (tq, LANES))
    for g in range(A_KV_HEADS):
        for r in range(group):
            h = g * group + r
            qs_scr[g, r * tq:(r + 1) * tq, :] = q_ref[0][:, h * LANES:(h + 1) * LANES]

    def score_tile(kt, carry):
        off = pl.multiple_of(kt * tk, tk)
        kiab = ki_ref[0, pl.ds(off, tk), :]
        acc = [jnp.zeros((tq, LANES), F32) for _ in range(nl)]
        for h in range(IDX_HEADS):
            qh = qi_ref[0][:, (h // 2) * LANES:(h // 2 + 1) * LANES]
            kk = kiab[:, (h % 2) * LANES:(h % 2 + 1) * LANES]
            d = jnp.maximum(_dot_t(qh, kk), 0.0)
            wbh = wb_scr[h]
            for c in range(nl):
                acc[c] = acc[c] + wbh * d[:, c * LANES:(c + 1) * LANES]
        for c in range(nl):
            bits = lax.bitcast_convert_type(acc[c], I32)
            key = bits ^ (lax.shift_right_arithmetic(bits, 31) & 0x7FFFFFFF)
            key = jnp.where(key == -1, 0, key)
            col = off + c * LANES + lane
            key_scr[kt, :, c * LANES:(c + 1) * LANES] = jnp.where(col < limit, key, INT_MIN)
        return carry

    lax.fori_loop(0, nkt, score_tile, 0)

    def count_tiles(pred):
        def body(kt, acc):
            key = key_scr[kt]
            off = kt * tk
            for c in range(nl):
                acc = acc + jnp.where(pred(key[:, c * LANES:(c + 1) * LANES], off + c * LANES + lane), 1.0, 0.0)
            return acc
        acc = lax.fori_loop(0, nkt, body, jnp.zeros((tq, LANES), F32))
        return jnp.broadcast_to(jnp.sum(acc, axis=-1, keepdims=True), (tq, LANES))

    def bit_step(i, carry):
        t_cur, c_cur = carry
        cand = t_cur + lax.shift_left(jnp.int32(1), 31 - i)
        cnt = count_tiles(lambda key, col: key >= cand)
        ok = cnt >= n_sel
        return jnp.where(ok, cand, t_cur), jnp.where(ok, cnt, c_cur)

    t_fin, c_fin = lax.fori_loop(
        0, 32, bit_step,
        (jnp.full((tq, LANES), INT_MIN, I32), jnp.full((tq, LANES), float(n_sel), F32)))
    is_min = t_fin == INT_MIN
    needs = jnp.logical_and(jnp.logical_not(is_min), c_fin > n_sel)
    t_scr[...] = t_fin
    j_scr[...] = jnp.where(is_min, 0, BIG_J)

    @pl.when(jnp.max(jnp.where(needs, 1.0, 0.0)) > 0.0)
    def _():
        rem = n_sel - count_tiles(lambda key, col: key > t_fin)

        def j_step(i, j_cur):
            cand = j_cur + lax.shift_left(jnp.int32(1), 14 - i)
            f = count_tiles(lambda key, col: jnp.logical_and(key == t_fin, col < cand))
            return jnp.where(f <= rem, cand, j_cur)

        j_fin = lax.fori_loop(0, 15, j_step, jnp.zeros((tq, LANES), I32))
        j_scr[...] = jnp.where(needs, j_fin, j_scr[...])

    m_scr[...] = jnp.full(m_scr.shape, NEG_BIG, F32)
    l_scr[...] = jnp.zeros(l_scr.shape, F32)
    acc_scr[...] = jnp.zeros(acc_scr.shape, F32)

    def attend_tile(kt, carry):
        off = pl.multiple_of(kt * tk, tk)
        key = key_scr[kt]
        t_b = t_scr[...]
        j_b = j_scr[...]
        bias = []
        for c in range(nl):
            kc = key[:, c * LANES:(c + 1) * LANES]
            col = off + c * LANES + lane
            sel = jnp.logical_or(kc > t_b, jnp.logical_and(kc == t_b, col < j_b))
            bias.append(jnp.where(sel, 0.0, -jnp.inf))
        bias = jnp.concatenate(bias, axis=1)
        for g in range(A_KV_HEADS):
            k_g = k_ref[0, pl.ds(off, tk), g * LANES:(g + 1) * LANES]
            v_g = v_ref[0, pl.ds(off, tk), g * LANES:(g + 1) * LANES]
            s = _dot_t(qs_scr[g], k_g)
            s = (s.reshape(group, tq, tk) + bias[None]).reshape(group * tq, tk)
            m_old = m_scr[g]
            m_new = jnp.maximum(m_old, jnp.max(s, axis=-1, keepdims=True))
            alpha = jnp.exp(m_old - m_new)
            p = jnp.exp(s - m_new)
            l_scr[g] = alpha * l_scr[g] + jnp.sum(p, axis=-1, keepdims=True)
            acc_scr[g] = alpha * acc_scr[g] + _dot(p.astype(BF16), v_g)
            m_scr[g] = m_new
        return carry

    lax.fori_loop(0, nkt, attend_tile, 0)
    for g in range(A_KV_HEADS):
        out = acc_scr[g] / l_scr[g]
        for r in range(group):
            h = g * group + r
            o_ref[0, :, h * LANES:(h + 1) * LANES] = out[r * tq:(r + 1) * tq].astype(BF16)


def _attn(q, qi, wi, kb, vb, kiab, *, tq, tk, causal, s_valid, n_sel):
    b, t, aw = q.shape
    s_pad = kb.shape[1]
    n_tiles = s_pad // tk
    n_heads = aw // A_HEAD_DIM
    group = n_heads // A_KV_HEADS
    qspec = lambda w: pl.BlockSpec((1, tq, w), lambda bi, qi_: (bi, qi_, 0))
    kspec = lambda w: pl.BlockSpec((1, s_pad, w), lambda bi, qi_: (bi, 0, 0))
    kern = functools.partial(_attn_kernel, tq=tq, tk=tk, causal=causal, s_valid=s_valid,
                             n_sel=n_sel, n_tiles=n_tiles, group=group)
    return pl.pallas_call(
        kern,
        grid=(b, t // tq),
        in_specs=[qspec(aw), qspec(qi.shape[2]), qspec(LANES),
                  kspec(kb.shape[2]), kspec(vb.shape[2]), kspec(kiab.shape[2])],
        out_specs=qspec(aw),
        out_shape=jax.ShapeDtypeStruct((b, t, aw), BF16),
        scratch_shapes=[
            pltpu.VMEM((n_tiles, tq, tk), I32),
            pltpu.VMEM((IDX_HEADS, tq, LANES), F32),
            pltpu.VMEM((A_KV_HEADS, group * tq, LANES), BF16),
            pltpu.VMEM((tq, LANES), I32),
            pltpu.VMEM((tq, LANES), I32),
            pltpu.VMEM((A_KV_HEADS, group * tq, 1), F32),
            pltpu.VMEM((A_KV_HEADS, group * tq, 1), F32),
            pltpu.VMEM((A_KV_HEADS, group * tq, LANES), F32),
        ],
        compiler_params=pltpu.CompilerParams(
            dimension_semantics=("arbitrary", "arbitrary"), vmem_limit_bytes=VMEM_LIMIT),
        name="attn",
    )(q, qi, wi, kb, vb, kiab)


def _pad_rows(x, n):
    if x.shape[0] == n:
        return x
    return jnp.concatenate([x, jnp.zeros((n - x.shape[0],) + x.shape[1:], x.dtype)], axis=0)


def _mlstm_kernel(u_ref, vm_ref, om_ref, ig_ref, fg_ref, c0_ref, n0_ref, m0_ref, cv0_ref,
                  cw_ref, cb_ref, wq_ref, wk_ref, wkt_ref, bg_ref, hg_ref, tri_ref,
                  mo_ref, c_out, n_out, m_out, cv_out,
                  c_scr, n_scr, m_scr, prev_scr, ubuf,
                  *, L, lb, n_heads):
    t = pl.program_id(1)
    nt = pl.num_programs(1)
    hd = M_HEAD_DIM
    pad = SUBLANES

    @pl.when(t == 0)
    def _():
        c_scr[...] = c0_ref[0]
        n_scr[...] = n0_ref[0]
        m_scr[...] = m0_ref[0]
        prev_scr[...] = cv0_ref[0]

    ubuf[0:pad, :] = prev_scr[...]
    ubuf[pad:pad + L, :] = _pad_rows(u_ref[...], L)
    uc = cb_ref[...]
    for i in range(CONV_W):
        uc = uc + ubuf[pad - (CONV_W - 1) + i:pad - (CONV_W - 1) + i + L, :] * cw_ref[i:i + 1, :]
    prev_scr[...] = ubuf[lb:lb + pad, :]
    uh = (uc * _sigmoid(uc)).astype(BF16)

    vm = _pad_rows(vm_ref[...], L)
    om = _pad_rows(om_ref[...], L)
    rowv = lax.broadcasted_iota(I32, (L, LANES), 0) < lb
    li = jnp.where(rowv, _pad_rows(ig_ref[...], L) + bg_ref[0:1, :], NEG_BIG)
    xf = _pad_rows(fg_ref[...], L) + bg_ref[1:2, :]
    lf = jnp.where(rowv, jnp.minimum(xf, 0.0) - jnp.log(1.0 + jnp.exp(-jnp.abs(xf))), 0.0)
    b = jnp.dot(tri_ref[...], lf, precision=lax.Precision.HIGHEST, preferred_element_type=F32)
    c = li - b
    c_t = c.T
    m_prev = m_scr[...]
    inter = b + m_prev
    b_last = b[L - 1:L, :]
    dec = b_last - b + li
    m_new = jnp.maximum(b_last + m_prev, jnp.max(dec, axis=0, keepdims=True))
    wts = jnp.exp(dec - m_new)
    wts_t = wts.T
    sc = jnp.exp(b_last + m_prev - m_new)
    causal = lax.broadcasted_iota(I32, (L, L), 0) >= lax.broadcasted_iota(I32, (L, L), 1)
    k_scale = hd ** -0.5

    for h in range(n_heads):
        sl = slice(h * hd, (h + 1) * hd)
        uh_h = uh[:, sl]
        q_h = _dot(uh_h, wq_ref[h])
        k_h = _dot(uh_h, wk_ref[h]) * k_scale
        kt_h = _dot_t(wkt_ref[h], uh_h) * k_scale
        v_h = vm[:, sl]
        dmat = jnp.where(causal, b[:, h:h + 1] + c_t[h:h + 1, :], -jnp.inf)
        inter_h = inter[:, h:h + 1]
        m_t = jnp.maximum(inter_h, jnp.max(dmat, axis=-1, keepdims=True))
        q_b = q_h.astype(BF16)
        sw = jnp.exp(dmat - m_t) * _dot_t(q_b, k_h.astype(BF16))
        sp = jnp.exp(inter_h - m_t)
        c_h = c_scr[h]
        n_h = n_scr[h:h + 1, :]
        num = sp * _dot(q_b, c_h.astype(BF16)) + _dot(sw.astype(BF16), v_h)
        den = sp * jnp.sum(q_h * n_h, axis=-1, keepdims=True) + jnp.sum(sw, axis=-1, keepdims=True)
        hh = num / jnp.maximum(jnp.abs(den), jnp.exp(-m_t))
        hn = _rms(hh, hg_ref[:, sl])
        out = _sigmoid(om[:, sl]) * hn
        mo_ref[:, sl] = out[:lb].astype(BF16)
        sc_h = sc[:, h:h + 1]
        c_scr[h] = sc_h * c_h + _dot((kt_h * wts_t[h:h + 1, :]).astype(BF16), v_h)
        n_scr[h:h + 1, :] = sc_h * n_h + jnp.sum(k_h * wts[:, h:h + 1], axis=0, keepdims=True)
    m_scr[...] = m_new

    @pl.when(t == nt - 1)
    def _():
        c_out[0] = c_scr[...]
        n_out[0] = n_scr[...]
        m_out[0] = m_scr[...]
        cv_out[0] = prev_scr[...]


def _mlstm(u, vm, om, ig, fg, c0, n0, m0, cv0, cw, cb, wq, wk, wkt, bg, hg, tri, *, b, L, lb):
    m, mw = u.shape
    nt = m // (b * lb)
    n_heads = mw // M_HEAD_DIM
    row = lambda w: pl.BlockSpec((lb, w), lambda bi, ti: (bi * nt + ti, 0))
    const = lambda a: pl.BlockSpec(a.shape, lambda bi, ti: (0,) * a.ndim)
    perb = lambda a: pl.BlockSpec((1,) + a.shape[1:], lambda bi, ti: (bi,) + (0,) * (a.ndim - 1))
    out_shapes = [
        jax.ShapeDtypeStruct((m, mw), BF16),
        jax.ShapeDtypeStruct(c0.shape, F32),
        jax.ShapeDtypeStruct(n0.shape, F32),
        jax.ShapeDtypeStruct(m0.shape, F32),
        jax.ShapeDtypeStruct(cv0.shape, F32),
    ]
    return pl.pallas_call(
        functools.partial(_mlstm_kernel, L=L, lb=lb, n_heads=n_heads),
        grid=(b, nt),
        in_specs=[row(mw), row(mw), row(mw), row(LANES), row(LANES),
                  perb(c0), perb(n0), perb(m0), perb(cv0),
                  const(cw), const(cb), const(wq), const(wk), const(wkt), const(bg), const(hg), const(tri)],
        out_specs=[row(mw), perb(c0), perb(n0), perb(m0), perb(cv0)],
        out_shape=out_shapes,
        scratch_shapes=[
            pltpu.VMEM(c0.shape[1:], F32),
            pltpu.VMEM(n0.shape[1:], F32),
            pltpu.VMEM(m0.shape[1:], F32),
            pltpu.VMEM((SUBLANES, mw), F32),
            pltpu.VMEM((L + SUBLANES, mw), F32),
        ],
        compiler_params=pltpu.CompilerParams(
            dimension_semantics=("arbitrary", "arbitrary"), vmem_limit_bytes=VMEM_LIMIT),
        name="mlstm",
    )(u, vm, om, ig, fg, c0, n0, m0, cv0, cw, cb, wq, wk, wkt, bg, hg, tri)


def _ffn_kernel(x_ref, a_ref, mo_ref, wo_ref, g2_ref, wu_ref, wd_ref, y_ref, hn_scr):
    j = pl.program_id(1)

    @pl.when(j == 0)
    def _():
        aw = a_ref.shape[1]
        h = x_ref[...] + _dot(a_ref[...], wo_ref[:aw, :]) + _dot(mo_ref[...], wo_ref[aw:, :])
        y_ref[...] = h
        hn_scr[...] = _rms(h, g2_ref[...]).astype(BF16)

    f = jnp.maximum(_dot(hn_scr[...], wu_ref[...]), 0.0)
    y_ref[...] += _dot((f * f).astype(BF16), wd_ref[...])


def _ffn(x, a, mo, wo, g2, wu, wd, *, tm, tf):
    m, d = x.shape
    dff = wu.shape[1]
    row = lambda w: pl.BlockSpec((tm, w), lambda i, j: (i, 0))
    const = lambda arr: pl.BlockSpec(arr.shape, lambda i, j: (0, 0), pipeline_mode=pl.Buffered(1))
    return pl.pallas_call(
        _ffn_kernel,
        grid=(m // tm, dff // tf),
        in_specs=[row(d), row(a.shape[1]), row(mo.shape[1]), const(wo), const(g2),
                  pl.BlockSpec((d, tf), lambda i, j: (0, j)),
                  pl.BlockSpec((tf, d), lambda i, j: (j, 0))],
        out_specs=row(d),
        out_shape=jax.ShapeDtypeStruct((m, d), F32),
        scratch_shapes=[pltpu.VMEM((tm, d), BF16)],
        compiler_params=pltpu.CompilerParams(
            dimension_semantics=("arbitrary", "arbitrary"), vmem_limit_bytes=VMEM_LIMIT),
        name="ffn",
    )(x, a, mo, wo, g2, wu, wd)


def _rope_tables(pos, rot, width):
    half = rot // 2
    inv_freq = ROPE_THETA ** (-jnp.arange(half, dtype=F32) / half)
    ang = pos.astype(F32)[:, None] * inv_freq[None, :]
    cos, sin = jnp.cos(ang), jnp.sin(ang)
    n = pos.shape[0]
    rest1 = jnp.ones((n, width - rot), F32)
    rest0 = jnp.zeros((n, width - rot), F32)
    z = jnp.zeros_like(sin)
    c = jnp.concatenate([cos, cos, rest1], axis=1)
    sa = jnp.concatenate([-sin, z, rest0], axis=1)
    sb = jnp.concatenate([z, sin, rest0], axis=1)
    rep = LANES // width
    return jnp.concatenate([jnp.tile(c, (1, rep)), jnp.tile(sa, (1, rep)), jnp.tile(sb, (1, rep))], axis=1)


def _pad_cols(w, n):
    return jnp.pad(w, ((0, 0), (0, n - w.shape[1])))


def _layer(x, pos_rows, b, mode, state, params, splits, *, tm_proj, tm_ffn, tf, tq, tk, L, lb):
    (norm1_g, w_in, q_norm_g, k_norm_g, conv_w, conv_b, wq_m, wk_m, b_igate, b_fgate,
     hnorm_g, w_out, norm2_g, w_up, w_down) = params
    bsz, t, d = x.shape
    m = bsz * t
    x2 = x.reshape(m, d)
    offs = np.cumsum((0,) + tuple(splits))
    col = lambda i: w_in[:, int(offs[i]):int(offs[i + 1])]
    aw, mw = splits[0], splits[6]
    n_q_heads = aw // A_HEAD_DIM
    n_m_heads = mw // M_HEAD_DIM
    wa = jnp.concatenate(
        [col(0), col(1), col(2), col(3), col(4), col(4),
         _pad_cols(col(5), LANES), _pad_cols(col(9), LANES), _pad_cols(col(10), LANES)], axis=1).astype(BF16)
    wb = jnp.concatenate([col(6), col(7), col(8)], axis=1).astype(BF16)
    ra = _rope_tables(pos_rows, A_HEAD_DIM // 4, A_HEAD_DIM)
    ri = _rope_tables(pos_rows, IDX_DIM // 4, IDX_DIM)

    (q, k, v, ki, kb, vb, kiab, qi, wi, ig, fg, u, vm, om) = _proj(
        x2, norm1_g[None], wa, wb, q_norm_g[None], k_norm_g[None], ra, ri,
        tm=tm_proj, n_q_heads=n_q_heads, m_width=mw)

    kvw = A_KV_HEADS * A_HEAD_DIM
    r3 = lambda a_: a_.reshape(bsz, t, a_.shape[1])
    if mode == "prompt":
        c0 = jnp.zeros((bsz, n_m_heads, M_HEAD_DIM, M_HEAD_DIM), F32)
        n0 = jnp.zeros((bsz, n_m_heads, M_HEAD_DIM), F32)
        m0 = jnp.zeros((bsz, 1, LANES), F32)
        cv0 = jnp.zeros((bsz, SUBLANES, mw), F32)
        n_sel = min(TOPK_MAX, t // 4)
        a = _attn(r3(q), r3(qi), r3(wi), r3(kb), r3(vb), r3(kiab),
                  tq=tq, tk=tk, causal=True, s_valid=t, n_sel=n_sel)
    else:
        ck, cv, ckidx, s_c, s_n, s_m, s_conv = state
        p = ck.shape[1]
        s_valid = p + t
        s_pad = -(-s_valid // tk) * tk
        padk = lambda a_: jnp.pad(a_, ((0, 0), (0, s_pad - s_valid), (0, 0)))
        k_all = padk(jnp.concatenate([ck.reshape(bsz, p, kvw).astype(BF16), r3(kb)], axis=1))
        v_all = padk(jnp.concatenate([cv.reshape(bsz, p, kvw).astype(BF16), r3(vb)], axis=1))
        cki = ckidx.astype(BF16)
        zki = jnp.zeros_like(cki)
        ki_c = jnp.concatenate([cki, zki, zki, cki], axis=2)
        ki_all = padk(jnp.concatenate([ki_c, r3(kiab)], axis=1))
        c0, n0 = s_c.astype(F32), s_n.astype(F32)
        m0 = jnp.pad(s_m.astype(F32), ((0, 0), (0, LANES - n_m_heads)))[:, None, :]
        cv0 = jnp.pad(s_conv.astype(F32), ((0, 0), (SUBLANES - (CONV_W - 1), 0), (0, 0)))
        n_sel = min(TOPK_MAX, s_valid // 4)
        a = _attn(r3(q), r3(qi), r3(wi), k_all, v_all, ki_all,
                  tq=tq, tk=tk, causal=False, s_valid=s_valid, n_sel=n_sel)

    bg = jnp.stack([_pad_cols(b_igate[None], LANES)[0], _pad_cols(b_fgate[None], LANES)[0]])
    tri = (np.arange(L)[:, None] >= np.arange(L)[None, :]).astype(np.float32)
    mo, c_new, n_new, m_new, cv_new = _mlstm(
        u, vm, om, ig, fg, c0, n0, m0, cv0,
        conv_w, conv_b[None], wq_m.astype(BF16), wk_m.astype(BF16),
        jnp.swapaxes(wk_m, 1, 2).astype(BF16), bg, hnorm_g[None], jnp.asarray(tri),
        b=bsz, L=L, lb=lb)

    y = _ffn(x2, a.reshape(m, aw), mo, w_out.astype(BF16), norm2_g[None],
             w_up.astype(BF16), w_down.astype(BF16), tm=tm_ffn, tf=tf)
    new_state = (
        k.reshape(bsz, t, A_KV_HEADS, A_HEAD_DIM), v.reshape(bsz, t, A_KV_HEADS, A_HEAD_DIM),
        ki.reshape(bsz, t, IDX_DIM), c_new, n_new, m_new[:, 0, :n_m_heads],
        cv_new[:, SUBLANES - (CONV_W - 1):, :])
    return y.reshape(bsz, t, d), new_state


def kernel(x_prompt, x_sample, cache_k, cache_v, cache_kidx, state_mlstm_C, state_mlstm_n, state_mlstm_m,
           state_conv, norm1_g, w_in, q_norm_g, k_norm_g, conv_w, conv_b, wq_m, wk_m, b_igate, b_fgate,
           hnorm_g, w_out, norm2_g, w_up, w_down):
    depth = w_in.shape[0]
    d = x_prompt.shape[-1]
    mix = w_out.shape[1]
    aw = mix // 2
    mw = mix - aw
    kvw = A_KV_HEADS * A_HEAD_DIM
    n_m_heads = mw // M_HEAD_DIM
    splits = (aw, kvw, kvw, IDX_HEADS * IDX_DIM, IDX_DIM, IDX_HEADS, mw, mw, mw, n_m_heads, n_m_heads)
    t_p = x_prompt.shape[1]
    b_s, t_s = x_sample.shape[0], x_sample.shape[1]
    past = cache_k.shape[2]

    y_p, y_s = x_prompt, x_sample
    new_p, new_s = [], []
    for l in range(depth):
        params = (norm1_g[l], w_in[l], q_norm_g[l], k_norm_g[l], conv_w[l], conv_b[l], wq_m[l], wk_m[l],
                  b_igate[l], b_fgate[l], hnorm_g[l], w_out[l], norm2_g[l], w_up[l], w_down[l])
        y_p, st_p = _layer(
            y_p, jnp.arange(t_p), None, "prompt", None, params, splits,
            tm_proj=256, tm_ffn=512, tf=512, tq=128, tk=512, L=128, lb=128)
        y_s, st_s = _layer(
            y_s, past + (jnp.arange(b_s * t_s) % t_s), None, "sample",
            (cache_k[l], cache_v[l], cache_kidx[l], state_mlstm_C[l], state_mlstm_n[l],
             state_mlstm_m[l], state_conv[l]), params, splits,
            tm_proj=b_s * t_s, tm_ffn=b_s * t_s, tf=512, tq=t_s, tk=384, L=128, lb=t_s)
        new_p.append(st_p)
        new_s.append(st_s)

    stack = lambda states, i: jnp.stack([s[i] for s in states])
    return (y_p, y_s,
            *[stack(new_p, i) for i in range(7)],
            *[stack(new_s, i) for i in range(7)])
```

```python
import functools

import jax
import jax.numpy as jnp
import numpy as np
from jax import lax
from jax.experimental import pallas as pl
from jax.experimental.pallas import tpu as pltpu

F32 = jnp.float32
BF16 = jnp.bfloat16
I32 = jnp.int32

EPS = 1e-6
ROPE_THETA = 500000.0
CHUNK = 64
CHUNK_LOG2 = 6
TOPK_MAX = 256
A_HEAD_DIM = 128
A_KV_HEADS = 2
IDX_HEADS = 8
IDX_DIM = 64
M_HEAD_DIM = 128
CONV_W = 4

LANES = 128
SUBLANES = 8
VMEM_LIMIT = 56 * 1024 * 1024
NEG_BIG = -1e30
INT_MIN = -(2 ** 31)
BIG_J = 2 ** 30


def _rms(x, g):
    r = lax.rsqrt(jnp.mean(x * x, axis=-1, keepdims=True) + EPS)
    return x * r * g


def _rope(t, c, sa, sb, half):
    return t * c + pltpu.roll(t, LANES - half, 1) * sa + pltpu.roll(t, half, 1) * sb


def _sigmoid(x):
    return 1.0 / (1.0 + jnp.exp(-x))


def _dot(a, b):
    return jnp.dot(a, b, preferred_element_type=F32)


def _dot_t(a, b):
    return lax.dot_general(a, b, (((1,), (1,)), ((), ())), preferred_element_type=F32)


def _proj_kernel(x_ref, g1_ref, wa_ref, wb_ref, qg_ref, kg_ref, ra_ref, ri_ref,
                 q_out, k_out, v_out, ki_out, kb_out, vb_out, kiab_out, qi_out,
                 wi_out, ig_out, fg_out, u_out, vm_out, om_out, *, n_q_heads):
    xn = _rms(x_ref[...], g1_ref[...]).astype(BF16)
    ra = ra_ref[...]
    ca, saa, sba = ra[:, :LANES], ra[:, LANES:2 * LANES], ra[:, 2 * LANES:]
    ri = ri_ref[...]
    ci, sai, sbi = ri[:, :LANES], ri[:, LANES:2 * LANES], ri[:, 2 * LANES:]
    qg = qg_ref[...]
    kg = kg_ref[...]
    aw = n_q_heads * A_HEAD_DIM
    kvw = A_KV_HEADS * A_HEAD_DIM
    q_scale = A_HEAD_DIM ** -0.5
    i_scale = IDX_DIM ** -0.5

    for c in range(0, aw, 512):
        z = _dot(xn, wa_ref[:, c:c + 512])
        for j in range(0, 512, LANES):
            qh = _rope(_rms(z[:, j:j + LANES], qg), ca, saa, sba, A_HEAD_DIM // 8) * q_scale
            q_out[:, c + j:c + j + LANES] = qh.astype(BF16)
    off = aw
    z = _dot(xn, wa_ref[:, off:off + 2 * kvw])
    for j in range(0, kvw, LANES):
        kh = _rope(_rms(z[:, j:j + LANES], kg), ca, saa, sba, A_HEAD_DIM // 8)
        k_out[:, j:j + LANES] = kh
        kb_out[:, j:j + LANES] = kh.astype(BF16)
    vv = z[:, kvw:]
    v_out[...] = vv
    vb_out[...] = vv.astype(BF16)
    off += 2 * kvw
    iw = IDX_HEADS * IDX_DIM
    z = _dot(xn, wa_ref[:, off:off + iw])
    for j in range(0, iw, LANES):
        qi = _rope(z[:, j:j + LANES], ci, sai, sbi, IDX_DIM // 8) * i_scale
        qi_out[:, j:j + LANES] = qi.astype(BF16)
    off += iw
    z = _dot(xn, wa_ref[:, off:off + 4 * LANES])
    ki2 = _rope(z[:, :LANES], ci, sai, sbi, IDX_DIM // 8)
    ki_out[...] = ki2[:, :IDX_DIM]
    lane = lax.broadcasted_iota(I32, (1, LANES), 1)
    lo = lane < IDX_DIM
    kiab_out[:, :LANES] = jnp.where(lo, ki2, 0.0).astype(BF16)
    kiab_out[:, LANES:] = jnp.where(lo, 0.0, ki2).astype(BF16)
    wi_out[...] = z[:, LANES:2 * LANES]
    ig_out[...] = z[:, 2 * LANES:3 * LANES]
    fg_out[...] = z[:, 3 * LANES:]
    mw = u_out.shape[1]
    for c in range(0, mw, 512):
        u_out[:, c:c + 512] = _dot(xn, wb_ref[:, c:c + 512])
        vm_out[:, c:c + 512] = _dot(xn, wb_ref[:, mw + c:mw + c + 512]).astype(BF16)
        om_out[:, c:c + 512] = _dot(xn, wb_ref[:, 2 * mw + c:2 * mw + c + 512])


def _proj(x, g1, wa, wb, qg, kg, ra, ri, *, tm, n_q_heads, m_width):
    m, d = x.shape
    n_tab = ra.shape[0] // tm
    aw = n_q_heads * A_HEAD_DIM
    kvw = A_KV_HEADS * A_HEAD_DIM
    iw = IDX_HEADS * IDX_DIM
    row = lambda w: pl.BlockSpec((tm, w), lambda i: (i, 0))
    const = lambda a: pl.BlockSpec(a.shape, lambda i: (0, 0), pipeline_mode=pl.Buffered(1))
    tab = pl.BlockSpec((tm, 3 * LANES), lambda i: (i % n_tab, 0))
    outs = [
        (aw, BF16), (kvw, F32), (kvw, F32), (IDX_DIM, F32), (kvw, BF16), (kvw, BF16),
        (2 * LANES, BF16), (iw, BF16), (LANES, F32), (LANES, F32), (LANES, F32),
        (m_width, F32), (m_width, BF16), (m_width, F32),
    ]
    return pl.pallas_call(
        functools.partial(_proj_kernel, n_q_heads=n_q_heads),
        grid=(m // tm,),
        in_specs=[row(d), const(g1), const(wa), const(wb), const(qg), const(kg), tab, tab],
        out_specs=[row(w) for w, _ in outs],
        out_shape=[jax.ShapeDtypeStruct((m, w), dt) for w, dt in outs],
        compiler_params=pltpu.CompilerParams(
            dimension_semantics=("arbitrary",), vmem_limit_bytes=VMEM_LIMIT),
        name="proj",
    )(x, g1, wa, wb, qg, kg, ra, ri)


def _pad_rows(x, n):
    if x.shape[0] == n:
        return x
    return jnp.concatenate([x, jnp.zeros((n - x.shape[0],) + x.shape[1:], x.dtype)], axis=0)


V_EXT_ROWS = A_HEAD_DIM + 16


def _attn_kernel(q_ref, qi_ref, wi_ref, k_ref, vt_ref, ki_ref, o_ref,
                 key_scr, qs_scr, qis_scr, m_scr, acc_scr,
                 *, tq, tqb, tk, causal, s_valid, n_sel, n_tiles, group):
    start = pl.program_id(1) * tqb
    nr = tk // SUBLANES
    n_acc = 8
    lane = lax.broadcasted_iota(I32, (1, tq), 1)
    rowi = lax.broadcasted_iota(I32, (tk, 1), 0)
    row3 = (lax.broadcasted_iota(I32, (nr, SUBLANES, 1), 0) * SUBLANES
            + lax.broadcasted_iota(I32, (nr, SUBLANES, 1), 1))
    if causal:
        limit = (lax.shift_right_logical(start + lane, CHUNK_LOG2) + 1) * CHUNK
        nkt = jnp.minimum((start + tqb + tk - 1) // tk, n_tiles)
    else:
        limit = jnp.full((1, tq), s_valid, I32)
        nkt = n_tiles

    q = _pad_rows(q_ref[0], tq)
    for g in range(A_KV_HEADS):
        for r in range(group):
            h = g * group + r
            qs_scr[g, r * tq:(r + 1) * tq, :] = q[:, h * LANES:(h + 1) * LANES]
    qi = _pad_rows(qi_ref[0], tq)
    for j in range(IDX_HEADS // 2):
        qis_scr[j * tq:(j + 1) * tq, :] = qi[:, j * LANES:(j + 1) * LANES]
    w_t = jnp.transpose(_pad_rows(wi_ref[0], tq))[:IDX_HEADS] * (IDX_HEADS ** -0.5)

    def score_tile(kt, carry):
        off = pl.multiple_of(kt * tk, tk)
        kiab = ki_ref[0, pl.ds(off, tk), :]
        d_even = _dot_t(kiab[:, :LANES], qis_scr[...])
        d_odd = _dot_t(kiab[:, LANES:], qis_scr[...])
        acc = jnp.zeros((tk, tq), F32)
        for h in range(IDX_HEADS):
            d = (d_even if h % 2 == 0 else d_odd)[:, (h // 2) * tq:(h // 2 + 1) * tq]
            acc = acc + w_t[h:h + 1, :] * jnp.maximum(d, 0.0)
        bits = lax.bitcast_convert_type(acc, I32)
        key = bits ^ (lax.shift_right_arithmetic(bits, 31) & 0x7FFFFFFF)
        key = jnp.where(key == -1, 0, key)
        key_scr[kt] = jnp.where(off + rowi < limit, key, INT_MIN)
        return carry

    lax.fori_loop(0, nkt, score_tile, 0)

    def count_tiles(pred):
        def body(kt, acc):
            hit = pred(key_scr[kt].reshape(nr, SUBLANES, tq), kt * tk + row3)
            ones = jnp.where(hit, 1.0, 0.0).reshape(nr // n_acc, n_acc, SUBLANES, tq)
            return acc + jnp.sum(ones, axis=0)
        acc = lax.fori_loop(0, nkt, body, jnp.zeros((n_acc, SUBLANES, tq), F32))
        return jnp.sum(jnp.sum(acc, axis=0), axis=0, keepdims=True)

    def bit_step(i, carry):
        t_cur, c_cur = carry
        cand = t_cur + lax.shift_left(jnp.int32(1), 31 - i)
        cnt = count_tiles(lambda key, col: key >= cand[None])
        ok = cnt >= n_sel
        return jnp.where(ok, cand, t_cur), jnp.where(ok, cnt, c_cur)

    t_fin, c_fin = lax.fori_loop(
        0, 32, bit_step,
        (jnp.full((1, tq), INT_MIN, I32), jnp.full((1, tq), float(n_sel), F32)))
    is_min = t_fin == INT_MIN
    needs = jnp.logical_and(jnp.logical_and(jnp.logical_not(is_min), c_fin > n_sel), lane < tqb)
    j_fast = jnp.where(is_min, 0, BIG_J)

    def tie_search():
        rem = n_sel - count_tiles(lambda key, col: key > t_fin[None])

        def j_step(i, j_cur):
            cand = j_cur + lax.shift_left(jnp.int32(1), 14 - i)
            f = count_tiles(lambda key, col: jnp.logical_and(key == t_fin[None], col < cand[None]))
            return jnp.where(f <= rem, cand, j_cur)

        j_slow = lax.fori_loop(0, 15, j_step, jnp.zeros((1, tq), I32))
        return jnp.where(needs, j_slow, j_fast)

    j_fin = lax.cond(jnp.max(jnp.where(needs, 1.0, 0.0)) > 0.0, tie_search, lambda: j_fast)

    m_scr[...] = jnp.full(m_scr.shape, NEG_BIG, F32)
    acc_scr[...] = jnp.zeros(acc_scr.shape, F32)

    def attend_tile(kt, carry):
        off = pl.multiple_of(kt * tk, tk)
        key = key_scr[kt]
        sel = jnp.logical_or(key > t_fin, jnp.logical_and(key == t_fin, off + rowi < j_fin))
        bias = jnp.where(sel, 0.0, -jnp.inf)
        bias = jnp.concatenate([bias] * group, axis=1)
        for g in range(A_KV_HEADS):
            k_g = k_ref[0, pl.ds(off, tk), g * LANES:(g + 1) * LANES]
            s = _dot_t(k_g, qs_scr[g]) + bias
            m_old = m_scr[g]
            m_new = jnp.maximum(m_old, jnp.max(s, axis=0, keepdims=True))
            alpha = jnp.exp(m_old - m_new)
            p = jnp.exp(s - m_new).astype(BF16)
            acc_scr[g] = alpha * acc_scr[g] + _dot(vt_ref[0, kt, g], p)
            m_scr[g] = m_new
        return carry

    lax.fori_loop(0, nkt, attend_tile, 0)
    for g in range(A_KV_HEADS):
        acc = acc_scr[g]
        out_t = acc[:A_HEAD_DIM] / acc[A_HEAD_DIM:A_HEAD_DIM + 1]
        for r in range(group):
            h = g * group + r
            o = jnp.transpose(out_t[:, r * tq:(r + 1) * tq])
            o_ref[0, :, h * LANES:(h + 1) * LANES] = o[:tqb].astype(BF16)


def _attn(q, qi, wi, kb, vb, kiab, *, tq, tk, causal, s_valid, n_sel):
    b, t, aw = q.shape
    tqb = min(tq, t)
    s_pad = kb.shape[1]
    n_tiles = s_pad // tk
    n_heads = aw // A_HEAD_DIM
    group = n_heads // A_KV_HEADS
    vt = jnp.transpose(vb.reshape(b, s_pad, A_KV_HEADS, A_HEAD_DIM), (0, 2, 3, 1))
    vt = jnp.concatenate([vt, jnp.ones((b, A_KV_HEADS, V_EXT_ROWS - A_HEAD_DIM, s_pad), BF16)], axis=2)
    vt = jnp.transpose(vt.reshape(b, A_KV_HEADS, V_EXT_ROWS, n_tiles, tk), (0, 3, 1, 2, 4))
    qspec = lambda w: pl.BlockSpec((1, tqb, w), lambda bi, qi_: (bi, qi_, 0))
    kspec = lambda w: pl.BlockSpec((1, s_pad, w), lambda bi, qi_: (bi, 0, 0))
    kern = functools.partial(_attn_kernel, tq=tq, tqb=tqb, tk=tk, causal=causal, s_valid=s_valid,
                             n_sel=n_sel, n_tiles=n_tiles, group=group)
    return pl.pallas_call(
        kern,
        grid=(b, t // tqb),
        in_specs=[qspec(aw), qspec(qi.shape[2]), qspec(LANES),
                  kspec(kb.shape[2]),
                  pl.BlockSpec((1,) + vt.shape[1:], lambda bi, qi_: (bi, 0, 0, 0, 0)),
                  kspec(kiab.shape[2])],
        out_specs=qspec(aw),
        out_shape=jax.ShapeDtypeStruct((b, t, aw), BF16),
        scratch_shapes=[
            pltpu.VMEM((n_tiles, tk, tq), I32),
            pltpu.VMEM((A_KV_HEADS, group * tq, LANES), BF16),
            pltpu.VMEM((IDX_HEADS // 2 * tq, LANES), BF16),
            pltpu.VMEM((A_KV_HEADS, 1, group * tq), F32),
            pltpu.VMEM((A_KV_HEADS, V_EXT_ROWS, group * tq), F32),
        ],
        compiler_params=pltpu.CompilerParams(
            dimension_semantics=("arbitrary", "arbitrary"), vmem_limit_bytes=VMEM_LIMIT),
        name="attn",
    )(q, qi, wi, kb, vt, kiab)


def _mlstm_kernel(u_ref, vm_ref, om_ref, ig_ref, fg_ref, c0_ref, n0_ref, m0_ref, cv0_ref,
                  cw_ref, cb_ref, wq_ref, wk_ref, wkt_ref, bg_ref, hg_ref, tri_ref,
                  mo_ref, c_out, n_out, m_out, cv_out,
                  c_scr, n_scr, m_scr, prev_scr, ubuf,
                  *, L, lb, n_heads):
    t = pl.program_id(1)
    nt = pl.num_programs(1)
    hd = M_HEAD_DIM
    pad = SUBLANES

    @pl.when(t == 0)
    def _():
        c_scr[...] = c0_ref[0]
        n_scr[...] = n0_ref[0]
        m_scr[...] = m0_ref[0]
        prev_scr[...] = cv0_ref[0]

    ubuf[0:pad, :] = prev_scr[...]
    ubuf[pad:pad + L, :] = _pad_rows(u_ref[...], L)
    uc = cb_ref[...]
    for i in range(CONV_W):
        uc = uc + ubuf[pad - (CONV_W - 1) + i:pad - (CONV_W - 1) + i + L, :] * cw_ref[i:i + 1, :]
    prev_scr[...] = ubuf[lb:lb + pad, :]
    uh = (uc * _sigmoid(uc)).astype(BF16)

    vm = _pad_rows(vm_ref[...], L)
    om = _pad_rows(om_ref[...], L)
    rowv = lax.broadcasted_iota(I32, (L, LANES), 0) < lb
    li = jnp.where(rowv, _pad_rows(ig_ref[...], L) + bg_ref[0:1, :], NEG_BIG)
    xf = _pad_rows(fg_ref[...], L) + bg_ref[1:2, :]
    lf = jnp.where(rowv, jnp.minimum(xf, 0.0) - jnp.log(1.0 + jnp.exp(-jnp.abs(xf))), 0.0)
    b = jnp.dot(tri_ref[...], lf, precision=lax.Precision.HIGHEST, preferred_element_type=F32)
    c = li - b
    c_t = c.T
    m_prev = m_scr[...]
    inter = b + m_prev
    b_last = b[L - 1:L, :]
    dec = b_last - b + li
    m_new = jnp.maximum(b_last + m_prev, jnp.max(dec, axis=0, keepdims=True))
    wts = jnp.exp(dec - m_new)
    wts_t = wts.T
    sc = jnp.exp(b_last + m_prev - m_new)
    causal = lax.broadcasted_iota(I32, (L, L), 0) >= lax.broadcasted_iota(I32, (L, L), 1)
    k_scale = hd ** -0.5

    for h in range(n_heads):
        sl = slice(h * hd, (h + 1) * hd)
        uh_h = uh[:, sl]
        q_h = _dot(uh_h, wq_ref[h])
        k_h = _dot(uh_h, wk_ref[h]) * k_scale
        kt_h = _dot_t(wkt_ref[h], uh_h) * k_scale
        v_h = vm[:, sl]
        dmat = jnp.where(causal, b[:, h:h + 1] + c_t[h:h + 1, :], -jnp.inf)
        inter_h = inter[:, h:h + 1]
        m_t = jnp.maximum(inter_h, jnp.max(dmat, axis=-1, keepdims=True))
        q_b = q_h.astype(BF16)
        sw = jnp.exp(dmat - m_t) * _dot_t(q_b, k_h.astype(BF16))
        sp = jnp.exp(inter_h - m_t)
        c_h = c_scr[h]
        n_h = n_scr[h:h + 1, :]
        num = sp * _dot(q_b, c_h.astype(BF16)) + _dot(sw.astype(BF16), v_h)
        den = sp * jnp.sum(q_h * n_h, axis=-1, keepdims=True) + jnp.sum(sw, axis=-1, keepdims=True)
        hh = num / jnp.maximum(jnp.abs(den), jnp.exp(-m_t))
        hn = _rms(hh, hg_ref[:, sl])
        out = _sigmoid(om[:, sl]) * hn
        mo_ref[:, sl] = out[:lb].astype(BF16)
        sc_h = sc[:, h:h + 1]
        c_scr[h] = sc_h * c_h + _dot((kt_h * wts_t[h:h + 1, :]).astype(BF16), v_h)
        n_scr[h:h + 1, :] = sc_h * n_h + jnp.sum(k_h * wts[:, h:h + 1], axis=0, keepdims=True)
    m_scr[...] = m_new

    @pl.when(t == nt - 1)
    def _():
        c_out[0] = c_scr[...]
        n_out[0] = n_scr[...]
        m_out[0] = m_scr[...]
        cv_out[0] = prev_scr[...]


def _mlstm(u, vm, om, ig, fg, c0, n0, m0, cv0, cw, cb, wq, wk, wkt, bg, hg, tri, *, b, L, lb):
    m, mw = u.shape
    nt = m // (b * lb)
    n_heads = mw // M_HEAD_DIM
    row = lambda w: pl.BlockSpec((lb, w), lambda bi, ti: (bi * nt + ti, 0))
    const = lambda a: pl.BlockSpec(a.shape, lambda bi, ti: (0,) * a.ndim)
    perb = lambda a: pl.BlockSpec((1,) + a.shape[1:], lambda bi, ti: (bi,) + (0,) * (a.ndim - 1))
    out_shapes = [
        jax.ShapeDtypeStruct((m, mw), BF16),
        jax.ShapeDtypeStruct(c0.shape, F32),
        jax.ShapeDtypeStruct(n0.shape, F32),
        jax.ShapeDtypeStruct(m0.shape, F32),
        jax.ShapeDtypeStruct(cv0.shape, F32),
    ]
    return pl.pallas_call(
        functools.partial(_mlstm_kernel, L=L, lb=lb, n_heads=n_heads),
        grid=(b, nt),
        in_specs=[row(mw), row(mw), row(mw), row(LANES), row(LANES),
                  perb(c0), perb(n0), perb(m0), perb(cv0),
                  const(cw), const(cb), const(wq), const(wk), const(wkt), const(bg), const(hg), const(tri)],
        out_specs=[row(mw), perb(c0), perb(n0), perb(m0), perb(cv0)],
        out_shape=out_shapes,
        scratch_shapes=[
            pltpu.VMEM(c0.shape[1:], F32),
            pltpu.VMEM(n0.shape[1:], F32),
            pltpu.VMEM(m0.shape[1:], F32),
            pltpu.VMEM((SUBLANES, mw), F32),
            pltpu.VMEM((L + SUBLANES, mw), F32),
        ],
        compiler_params=pltpu.CompilerParams(
            dimension_semantics=("arbitrary", "arbitrary"), vmem_limit_bytes=VMEM_LIMIT),
        name="mlstm",
    )(u, vm, om, ig, fg, c0, n0, m0, cv0, cw, cb, wq, wk, wkt, bg, hg, tri)


def _ffn_kernel(x_ref, a_ref, mo_ref, wo_ref, g2_ref, wu_ref, wd_ref, y_ref, hn_scr):
    j = pl.program_id(1)

    @pl.when(j == 0)
    def _():
        aw = a_ref.shape[1]
        h = x_ref[...] + _dot(a_ref[...], wo_ref[:aw, :]) + _dot(mo_ref[...], wo_ref[aw:, :])
        y_ref[...] = h
        hn_scr[...] = _rms(h, g2_ref[...]).astype(BF16)

    f = jnp.maximum(_dot(hn_scr[...], wu_ref[...]), 0.0)
    y_ref[...] += _dot((f * f).astype(BF16), wd_ref[...])


def _ffn(x, a, mo, wo, g2, wu, wd, *, tm, tf):
    m, d = x.shape
    dff = wu.shape[1]
    row = lambda w: pl.BlockSpec((tm, w), lambda i, j: (i, 0))
    const = lambda arr: pl.BlockSpec(arr.shape, lambda i, j: (0, 0), pipeline_mode=pl.Buffered(1))
    return pl.pallas_call(
        _ffn_kernel,
        grid=(m // tm, dff // tf),
        in_specs=[row(d), row(a.shape[1]), row(mo.shape[1]), const(wo), const(g2),
                  pl.BlockSpec((d, tf), lambda i, j: (0, j)),
                  pl.BlockSpec((tf, d), lambda i, j: (j, 0))],
        out_specs=row(d),
        out_shape=jax.ShapeDtypeStruct((m, d), F32),
        scratch_shapes=[pltpu.VMEM((tm, d), BF16)],
        compiler_params=pltpu.CompilerParams(
            dimension_semantics=("arbitrary", "arbitrary"), vmem_limit_bytes=VMEM_LIMIT),
        name="ffn",
    )(x, a, mo, wo, g2, wu, wd)


def _rope_tables(pos, rot, width):
    half = rot // 2
    inv_freq = ROPE_THETA ** (-jnp.arange(half, dtype=F32) / half)
    ang = pos.astype(F32)[:, None] * inv_freq[None, :]
    cos, sin = jnp.cos(ang), jnp.sin(ang)
    n = pos.shape[0]
    rest1 = jnp.ones((n, width - rot), F32)
    rest0 = jnp.zeros((n, width - rot), F32)
    z = jnp.zeros_like(sin)
    c = jnp.concatenate([cos, cos, rest1], axis=1)
    sa = jnp.concatenate([-sin, z, rest0], axis=1)
    sb = jnp.concatenate([z, sin, rest0], axis=1)
    rep = LANES // width
    return jnp.concatenate([jnp.tile(c, (1, rep)), jnp.tile(sa, (1, rep)), jnp.tile(sb, (1, rep))], axis=1)


def _pad_cols(w, n):
    return jnp.pad(w, ((0, 0), (0, n - w.shape[1])))


def _layer(x, pos_rows, b, mode, state, params, splits, *, tm_proj, tm_ffn, tf, tq, tk, L, lb):
    (norm1_g, w_in, q_norm_g, k_norm_g, conv_w, conv_b, wq_m, wk_m, b_igate, b_fgate,
     hnorm_g, w_out, norm2_g, w_up, w_down) = params
    bsz, t, d = x.shape
    m = bsz * t
    x2 = x.reshape(m, d)
    offs = np.cumsum((0,) + tuple(splits))
    col = lambda i: w_in[:, int(offs[i]):int(offs[i + 1])]
    aw, mw = splits[0], splits[6]
    n_q_heads = aw // A_HEAD_DIM
    n_m_heads = mw // M_HEAD_DIM
    wa = jnp.concatenate(
        [col(0), col(1), col(2), col(3), col(4), col(4),
         _pad_cols(col(5), LANES), _pad_cols(col(9), LANES), _pad_cols(col(10), LANES)], axis=1).astype(BF16)
    wb = jnp.concatenate([col(6), col(7), col(8)], axis=1).astype(BF16)
    ra = _rope_tables(pos_rows, A_HEAD_DIM // 4, A_HEAD_DIM)
    ri = _rope_tables(pos_rows, IDX_DIM // 4, IDX_DIM)

    (q, k, v, ki, kb, vb, kiab, qi, wi, ig, fg, u, vm, om) = _proj(
        x2, norm1_g[None], wa, wb, q_norm_g[None], k_norm_g[None], ra, ri,
        tm=tm_proj, n_q_heads=n_q_heads, m_width=mw)

    kvw = A_KV_HEADS * A_HEAD_DIM
    r3 = lambda a_: a_.reshape(bsz, t, a_.shape[1])
    if mode == "prompt":
        c0 = jnp.zeros((bsz, n_m_heads, M_HEAD_DIM, M_HEAD_DIM), F32)
        n0 = jnp.zeros((bsz, n_m_heads, M_HEAD_DIM), F32)
        m0 = jnp.zeros((bsz, 1, LANES), F32)
        cv0 = jnp.zeros((bsz, SUBLANES, mw), F32)
        n_sel = min(TOPK_MAX, t // 4)
        a = _attn(r3(q), r3(qi), r3(wi), r3(kb), r3(vb), r3(kiab),
                  tq=tq, tk=tk, causal=True, s_valid=t, n_sel=n_sel)
    else:
        ck, cv, ckidx, s_c, s_n, s_m, s_conv = state
        p = ck.shape[1]
        s_valid = p + t
        s_pad = -(-s_valid // tk) * tk
        padk = lambda a_: jnp.pad(a_, ((0, 0), (0, s_pad - s_valid), (0, 0)))
        k_all = padk(jnp.concatenate([ck.reshape(bsz, p, kvw).astype(BF16), r3(kb)], axis=1))
        v_all = padk(jnp.concatenate([cv.reshape(bsz, p, kvw).astype(BF16), r3(vb)], axis=1))
        cki = ckidx.astype(BF16)
        zki = jnp.zeros_like(cki)
        ki_c = jnp.concatenate([cki, zki, zki, cki], axis=2)
        ki_all = padk(jnp.concatenate([ki_c, r3(kiab)], axis=1))
        c0, n0 = s_c.astype(F32), s_n.astype(F32)
        m0 = jnp.pad(s_m.astype(F32), ((0, 0), (0, LANES - n_m_heads)))[:, None, :]
        cv0 = jnp.pad(s_conv.astype(F32), ((0, 0), (SUBLANES - (CONV_W - 1), 0), (0, 0)))
        n_sel = min(TOPK_MAX, s_valid // 4)
        a = _attn(r3(q), r3(qi), r3(wi), k_all, v_all, ki_all,
                  tq=tq, tk=tk, causal=False, s_valid=s_valid, n_sel=n_sel)

    bg = jnp.stack([_pad_cols(b_igate[None], LANES)[0], _pad_cols(b_fgate[None], LANES)[0]])
    tri = (np.arange(L)[:, None] >= np.arange(L)[None, :]).astype(np.float32)
    mo, c_new, n_new, m_new, cv_new = _mlstm(
        u, vm, om, ig, fg, c0, n0, m0, cv0,
        conv_w, conv_b[None], wq_m.astype(BF16), wk_m.astype(BF16),
        jnp.swapaxes(wk_m, 1, 2).astype(BF16), bg, hnorm_g[None], jnp.asarray(tri),
        b=bsz, L=L, lb=lb)

    y = _ffn(x2, a.reshape(m, aw), mo, w_out.astype(BF16), norm2_g[None],
             w_up.astype(BF16), w_down.astype(BF16), tm=tm_ffn, tf=tf)
    new_state = (
        k.reshape(bsz, t, A_KV_HEADS, A_HEAD_DIM), v.reshape(bsz, t, A_KV_HEADS, A_HEAD_DIM),
        ki.reshape(bsz, t, IDX_DIM), c_new, n_new, m_new[:, 0, :n_m_heads],
        cv_new[:, SUBLANES - (CONV_W - 1):, :])
    return y.reshape(bsz, t, d), new_state


def kernel(x_prompt, x_sample, cache_k, cache_v, cache_kidx, state_mlstm_C, state_mlstm_n, state_mlstm_m,
           state_conv, norm1_g, w_in, q_norm_g, k_norm_g, conv_w, conv_b, wq_m, wk_m, b_igate, b_fgate,
           hnorm_g, w_out, norm2_g, w_up, w_down):
    depth = w_in.shape[0]
    d = x_prompt.shape[-1]
    mix = w_out.shape[1]
    aw = mix // 2
    mw = mix - aw
    kvw = A_KV_HEADS * A_HEAD_DIM
    n_m_heads = mw // M_HEAD_DIM
    splits = (aw, kvw, kvw, IDX_HEADS * IDX_DIM, IDX_DIM, IDX_HEADS, mw, mw, mw, n_m_heads, n_m_heads)
    t_p = x_prompt.shape[1]
    b_s, t_s = x_sample.shape[0], x_sample.shape[1]
    past = cache_k.shape[2]

    y_p, y_s = x_prompt, x_sample
    new_p, new_s = [], []
    for l in range(depth):
        params = (norm1_g[l], w_in[l], q_norm_g[l], k_norm_g[l], conv_w[l], conv_b[l], wq_m[l], wk_m[l],
                  b_igate[l], b_fgate[l], hnorm_g[l], w_out[l], norm2_g[l], w_up[l], w_down[l])
        y_p, st_p = _layer(
            y_p, jnp.arange(t_p), None, "prompt", None, params, splits,
            tm_proj=256, tm_ffn=512, tf=512, tq=128, tk=512, L=128, lb=128)
        y_s, st_s = _layer(
            y_s, past + (jnp.arange(b_s * t_s) % t_s), None, "sample",
            (cache_k[l], cache_v[l], cache_kidx[l], state_mlstm_C[l], state_mlstm_n[l],
             state_mlstm_m[l], state_conv[l]), params, splits,
            tm_proj=b_s * t_s, tm_ffn=b_s * t_s, tf=512, tq=LANES, tk=384, L=128, lb=t_s)
        new_p.append(st_p)
        new_s.append(st_s)

    stack = lambda states, i: jnp.stack([s[i] for s in states])
    return (y_p, y_s,
            *[stack(new_p, i) for i in range(7)],
            *[stack(new_s, i) for i in range(7)])
```

```python
import functools

import jax
import jax.numpy as jnp
import numpy as np
from jax import lax
from jax.experimental import pallas as pl
from jax.experimental.pallas import tpu as pltpu

F32 = jnp.float32
BF16 = jnp.bfloat16
I32 = jnp.int32
I16 = jnp.int16

EPS = 1e-6
ROPE_THETA = 500000.0
CHUNK = 64
CHUNK_LOG2 = 6
TOPK_MAX = 256
A_HEAD_DIM = 128
A_KV_HEADS = 2
IDX_HEADS = 8
IDX_DIM = 64
M_HEAD_DIM = 128
CONV_W = 4

LANES = 128
SUBLANES = 8
VMEM_LIMIT = 56 * 1024 * 1024
NEG_BIG = -1e30
INT_MIN = -(2 ** 31)
BIG_J = 2 ** 30


def _rms(x, g):
    r = lax.rsqrt(jnp.mean(x * x, axis=-1, keepdims=True) + EPS)
    return x * r * g


def _rope(t, c, sa, sb, half):
    return t * c + pltpu.roll(t, LANES - half, 1) * sa + pltpu.roll(t, half, 1) * sb


def _sigmoid(x):
    return 1.0 / (1.0 + jnp.exp(-x))


def _dot(a, b):
    return jnp.dot(a, b, preferred_element_type=F32)


def _dot_t(a, b):
    return lax.dot_general(a, b, (((1,), (1,)), ((), ())), preferred_element_type=F32)


def _proj_kernel(x_ref, g1_ref, wa_ref, wb_ref, qg_ref, kg_ref, ra_ref, ri_ref,
                 q_out, k_out, v_out, ki_out, kb_out, vb_out, kiab_out, qi_out,
                 wi_out, ig_out, fg_out, u_out, vm_out, om_out, *, n_q_heads):
    xn = _rms(x_ref[...], g1_ref[...]).astype(BF16)
    ra = ra_ref[...]
    ca, saa, sba = ra[:, :LANES], ra[:, LANES:2 * LANES], ra[:, 2 * LANES:]
    ri = ri_ref[...]
    ci, sai, sbi = ri[:, :LANES], ri[:, LANES:2 * LANES], ri[:, 2 * LANES:]
    qg = qg_ref[...]
    kg = kg_ref[...]
    aw = n_q_heads * A_HEAD_DIM
    kvw = A_KV_HEADS * A_HEAD_DIM
    q_scale = A_HEAD_DIM ** -0.5
    i_scale = IDX_DIM ** -0.5

    for c in range(0, aw, 512):
        z = _dot(xn, wa_ref[:, c:c + 512])
        for j in range(0, 512, LANES):
            qh = _rope(_rms(z[:, j:j + LANES], qg), ca, saa, sba, A_HEAD_DIM // 8) * q_scale
            q_out[:, c + j:c + j + LANES] = qh.astype(BF16)
    off = aw
    z = _dot(xn, wa_ref[:, off:off + 2 * kvw])
    for j in range(0, kvw, LANES):
        kh = _rope(_rms(z[:, j:j + LANES], kg), ca, saa, sba, A_HEAD_DIM // 8)
        k_out[:, j:j + LANES] = kh
        kb_out[:, j:j + LANES] = kh.astype(BF16)
    vv = z[:, kvw:]
    v_out[...] = vv
    vb_out[...] = vv.astype(BF16)
    off += 2 * kvw
    iw = IDX_HEADS * IDX_DIM
    z = _dot(xn, wa_ref[:, off:off + iw])
    for j in range(0, iw, LANES):
        qi = _rope(z[:, j:j + LANES], ci, sai, sbi, IDX_DIM // 8) * i_scale
        qi_out[:, j:j + LANES] = qi.astype(BF16)
    off += iw
    z = _dot(xn, wa_ref[:, off:off + 4 * LANES])
    ki2 = _rope(z[:, :LANES], ci, sai, sbi, IDX_DIM // 8)
    ki_out[...] = ki2[:, :IDX_DIM]
    lane = lax.broadcasted_iota(I32, (1, LANES), 1)
    lo = lane < IDX_DIM
    kiab_out[:, :LANES] = jnp.where(lo, ki2, 0.0).astype(BF16)
    kiab_out[:, LANES:] = jnp.where(lo, 0.0, ki2).astype(BF16)
    wi_out[...] = z[:, LANES:2 * LANES]
    ig_out[...] = z[:, 2 * LANES:3 * LANES]
    fg_out[...] = z[:, 3 * LANES:]
    mw = u_out.shape[1]
    for c in range(0, mw, 512):
        u_out[:, c:c + 512] = _dot(xn, wb_ref[:, c:c + 512])
        vm_out[:, c:c + 512] = _dot(xn, wb_ref[:, mw + c:mw + c + 512]).astype(BF16)
        om_out[:, c:c + 512] = _dot(xn, wb_ref[:, 2 * mw + c:2 * mw + c + 512])


def _proj(x, g1, wa, wb, qg, kg, ra, ri, *, tm, n_q_heads, m_width):
    m, d = x.shape
    n_tab = ra.shape[0] // tm
    aw = n_q_heads * A_HEAD_DIM
    kvw = A_KV_HEADS * A_HEAD_DIM
    iw = IDX_HEADS * IDX_DIM
    row = lambda w: pl.BlockSpec((tm, w), lambda i: (i, 0))
    const = lambda a: pl.BlockSpec(a.shape, lambda i: (0, 0), pipeline_mode=pl.Buffered(1))
    tab = pl.BlockSpec((tm, 3 * LANES), lambda i: (i % n_tab, 0))
    outs = [
        (aw, BF16), (kvw, F32), (kvw, F32), (IDX_DIM, F32), (kvw, BF16), (kvw, BF16),
        (2 * LANES, BF16), (iw, BF16), (LANES, F32), (LANES, F32), (LANES, F32),
        (m_width, F32), (m_width, BF16), (m_width, F32),
    ]
    return pl.pallas_call(
        functools.partial(_proj_kernel, n_q_heads=n_q_heads),
        grid=(m // tm,),
        in_specs=[row(d), const(g1), const(wa), const(wb), const(qg), const(kg), tab, tab],
        out_specs=[row(w) for w, _ in outs],
        out_shape=[jax.ShapeDtypeStruct((m, w), dt) for w, dt in outs],
        compiler_params=pltpu.CompilerParams(
            dimension_semantics=("arbitrary",), vmem_limit_bytes=VMEM_LIMIT),
        name="proj",
    )(x, g1, wa, wb, qg, kg, ra, ri)


def _pad_rows(x, n):
    if x.shape[0] == n:
        return x
    return jnp.concatenate([x, jnp.zeros((n - x.shape[0],) + x.shape[1:], x.dtype)], axis=0)


V_EXT_ROWS = A_HEAD_DIM + 16


def _attn_kernel(q_ref, qi_ref, wi_ref, k_ref, vt_ref, ki_ref, o_ref,
                 key_scr, hi_scr, lo_scr, qs_scr, qis_scr, m_scr, acc_scr,
                 *, tq, tqb, tk, causal, s_valid, n_sel, n_tiles, group):
    start = pl.program_id(1) * tqb
    nr = tk // SUBLANES
    n_acc = 8
    lane = lax.broadcasted_iota(I32, (1, tq), 1)
    rowi = lax.broadcasted_iota(I32, (tk, 1), 0)
    row3 = (lax.broadcasted_iota(I32, (nr, SUBLANES, 1), 0) * SUBLANES
            + lax.broadcasted_iota(I32, (nr, SUBLANES, 1), 1))
    if causal:
        limit = (lax.shift_right_logical(start + lane, CHUNK_LOG2) + 1) * CHUNK
        nkt = jnp.minimum((start + tqb + tk - 1) // tk, n_tiles)
    else:
        limit = jnp.full((1, tq), s_valid, I32)
        nkt = n_tiles

    q = _pad_rows(q_ref[0], tq)
    for g in range(A_KV_HEADS):
        for r in range(group):
            h = g * group + r
            qs_scr[g, r * tq:(r + 1) * tq, :] = q[:, h * LANES:(h + 1) * LANES]
    qi = _pad_rows(qi_ref[0], tq)
    for j in range(IDX_HEADS // 2):
        qis_scr[j * tq:(j + 1) * tq, :] = qi[:, j * LANES:(j + 1) * LANES]
    w_t = jnp.transpose(_pad_rows(wi_ref[0], tq))[:IDX_HEADS] * (IDX_HEADS ** -0.5)

    def score_tile(kt, carry):
        off = pl.multiple_of(kt * tk, tk)
        kiab = ki_ref[0, pl.ds(off, tk), :]
        acc = jnp.zeros((tk, tq), F32)
        for half in range(IDX_HEADS // 4):
            qh = qis_scr[half * 2 * tq:(half + 1) * 2 * tq, :]
            for par in range(2):
                d = _dot_t(kiab[:, par * LANES:(par + 1) * LANES], qh)
                for jj in range(2):
                    h = 2 * (2 * half + jj) + par
                    acc = acc + w_t[h:h + 1, :] * jnp.maximum(d[:, jj * tq:(jj + 1) * tq], 0.0)
        bits = lax.bitcast_convert_type(acc, I32)
        key = bits ^ (lax.shift_right_arithmetic(bits, 31) & 0x7FFFFFFF)
        key = jnp.where(key == -1, 0, key)
        key = jnp.where(off + rowi < limit, key, INT_MIN)
        key_scr[kt] = key
        hi_scr[kt] = lax.shift_right_arithmetic(key, 16).astype(I16)
        lo_scr[kt] = ((key & 0xFFFF) - 32768).astype(I16)
        return carry

    lax.fori_loop(0, nkt, score_tile, 0)

    def count_tiles(pred):
        def body(kt, acc):
            hit = pred(key_scr[kt].reshape(nr, SUBLANES, tq), kt * tk + row3)
            ones = jnp.where(hit, 1.0, 0.0).reshape(nr // n_acc, n_acc, SUBLANES, tq)
            return acc + jnp.sum(ones, axis=0)
        acc = lax.fori_loop(0, nkt, body, jnp.zeros((n_acc, SUBLANES, tq), F32))
        return jnp.sum(jnp.sum(acc, axis=0), axis=0, keepdims=True)

    def count16(ref, pred):
        def body(kt, acc):
            ones = jnp.where(pred(ref[kt].reshape(tk // (8 * 16), 8, 16, tq)), jnp.int16(1), jnp.int16(0))
            for i in range(ones.shape[0]):
                acc = acc + ones[i]
            return acc
        acc = lax.fori_loop(0, nkt, body, jnp.zeros((8, 16, tq), I16)).astype(I32)
        return jnp.sum(jnp.sum(acc, axis=0), axis=0, keepdims=True)

    def search16(ref, target, c_init):
        def step(i, carry):
            t_cur, c_cur = carry
            cand = t_cur + lax.shift_left(jnp.int32(1), 15 - i)
            c16 = cand.astype(I16)
            cnt = count16(ref, lambda x: x >= c16[None, None])
            ok = cnt >= target
            return jnp.where(ok, cand, t_cur), jnp.where(ok, cnt, c_cur)
        return lax.fori_loop(0, 16, step, (jnp.full((1, tq), -32768, I32), c_init))

    t_hi, c_hi = search16(hi_scr, n_sel, jnp.full((1, tq), n_sel, I32))
    t_hi16 = t_hi.astype(I16)
    c_gt = count16(hi_scr, lambda x: x > t_hi16[None, None])

    def mask_lo(kt, carry):
        lo_scr[kt] = jnp.where(hi_scr[kt] == t_hi16, lo_scr[kt], jnp.int16(-32768))
        return carry

    lax.fori_loop(0, nkt, mask_lo, 0)
    t_lo, c_lo = search16(lo_scr, n_sel - c_gt, c_hi - c_gt)
    t_fin = t_hi * 65536 + (t_lo + 32768)
    c_fin = c_gt + c_lo
    is_min = t_fin == INT_MIN
    needs = jnp.logical_and(jnp.logical_and(jnp.logical_not(is_min), c_fin > n_sel), lane < tqb)
    j_fast = jnp.where(is_min, 0, BIG_J)

    def tie_search():
        rem = n_sel - count_tiles(lambda key, col: key > t_fin[None])

        def j_step(i, j_cur):
            cand = j_cur + lax.shift_left(jnp.int32(1), 14 - i)
            f = count_tiles(lambda key, col: jnp.logical_and(key == t_fin[None], col < cand[None]))
            return jnp.where(f <= rem, cand, j_cur)

        j_slow = lax.fori_loop(0, 15, j_step, jnp.zeros((1, tq), I32))
        return jnp.where(needs, j_slow, j_fast)

    j_fin = lax.cond(jnp.max(jnp.where(needs, 1.0, 0.0)) > 0.0, tie_search, lambda: j_fast)

    m_scr[...] = jnp.full(m_scr.shape, NEG_BIG, F32)
    acc_scr[...] = jnp.zeros(acc_scr.shape, F32)

    def attend_tile(kt, carry):
        off = pl.multiple_of(kt * tk, tk)
        key = key_scr[kt]
        sel = jnp.logical_or(key > t_fin, jnp.logical_and(key == t_fin, off + rowi < j_fin))
        bias = jnp.where(sel, 0.0, -jnp.inf)
        bias = jnp.concatenate([bias] * group, axis=1)
        for g in range(A_KV_HEADS):
            k_g = k_ref[0, pl.ds(off, tk), g * LANES:(g + 1) * LANES]
            s = _dot_t(k_g, qs_scr[g]) + bias
            m_old = m_scr[g]
            m_new = jnp.maximum(m_old, jnp.max(s, axis=0, keepdims=True))
            alpha = jnp.exp(m_old - m_new)
            p = jnp.exp(s - m_new).astype(BF16)
            acc_scr[g] = alpha * acc_scr[g] + _dot(vt_ref[0, kt, g], p)
            m_scr[g] = m_new
        return carry

    lax.fori_loop(0, nkt, attend_tile, 0)
    for g in range(A_KV_HEADS):
        acc = acc_scr[g]
        out_t = acc[:A_HEAD_DIM] / acc[A_HEAD_DIM:A_HEAD_DIM + 1]
        for r in range(group):
            h = g * group + r
            o = jnp.transpose(out_t[:, r * tq:(r + 1) * tq])
            o_ref[0, :, h * LANES:(h + 1) * LANES] = o[:tqb].astype(BF16)


def _attn(q, qi, wi, kb, vb, kiab, *, tq, tk, causal, s_valid, n_sel):
    b, t, aw = q.shape
    tqb = min(tq, t)
    s_pad = kb.shape[1]
    n_tiles = s_pad // tk
    n_heads = aw // A_HEAD_DIM
    group = n_heads // A_KV_HEADS
    vt = jnp.transpose(vb.reshape(b, s_pad, A_KV_HEADS, A_HEAD_DIM), (0, 2, 3, 1))
    vt = jnp.concatenate([vt, jnp.ones((b, A_KV_HEADS, V_EXT_ROWS - A_HEAD_DIM, s_pad), BF16)], axis=2)
    vt = jnp.transpose(vt.reshape(b, A_KV_HEADS, V_EXT_ROWS, n_tiles, tk), (0, 3, 1, 2, 4))
    qspec = lambda w: pl.BlockSpec((1, tqb, w), lambda bi, qi_: (bi, qi_, 0))
    kspec = lambda w: pl.BlockSpec((1, s_pad, w), lambda bi, qi_: (bi, 0, 0))
    kern = functools.partial(_attn_kernel, tq=tq, tqb=tqb, tk=tk, causal=causal, s_valid=s_valid,
                             n_sel=n_sel, n_tiles=n_tiles, group=group)
    return pl.pallas_call(
        kern,
        grid=(b, t // tqb),
        in_specs=[qspec(aw), qspec(qi.shape[2]), qspec(LANES),
                  kspec(kb.shape[2]),
                  pl.BlockSpec((1,) + vt.shape[1:], lambda bi, qi_: (bi, 0, 0, 0, 0)),
                  kspec(kiab.shape[2])],
        out_specs=qspec(aw),
        out_shape=jax.ShapeDtypeStruct((b, t, aw), BF16),
        scratch_shapes=[
            pltpu.VMEM((n_tiles, tk, tq), I32),
            pltpu.VMEM((n_tiles, tk, tq), I16),
            pltpu.VMEM((n_tiles, tk, tq), I16),
            pltpu.VMEM((A_KV_HEADS, group * tq, LANES), BF16),
            pltpu.VMEM((IDX_HEADS // 2 * tq, LANES), BF16),
            pltpu.VMEM((A_KV_HEADS, 1, group * tq), F32),
            pltpu.VMEM((A_KV_HEADS, V_EXT_ROWS, group * tq), F32),
        ],
        compiler_params=pltpu.CompilerParams(
            dimension_semantics=("arbitrary", "arbitrary"), vmem_limit_bytes=VMEM_LIMIT),
        name="attn",
    )(q, qi, wi, kb, vt, kiab)


def _mlstm_kernel(u_ref, vm_ref, om_ref, ig_ref, fg_ref, c0_ref, n0_ref, m0_ref, cv0_ref,
                  cw_ref, cb_ref, wq_ref, wk_ref, wkt_ref, bg_ref, hg_ref, tri_ref,
                  mo_ref, c_out, n_out, m_out, cv_out,
                  c_scr, n_scr, m_scr, prev_scr, ubuf,
                  *, L, lb, n_heads):
    t = pl.program_id(1)
    nt = pl.num_programs(1)
    hd = M_HEAD_DIM
    pad = SUBLANES

    @pl.when(t == 0)
    def _():
        c_scr[...] = c0_ref[0]
        n_scr[...] = n0_ref[0]
        m_scr[...] = m0_ref[0]
        prev_scr[...] = cv0_ref[0]

    ubuf[0:pad, :] = prev_scr[...]
    ubuf[pad:pad + L, :] = _pad_rows(u_ref[...], L)
    uc = cb_ref[...]
    for i in range(CONV_W):
        uc = uc + ubuf[pad - (CONV_W - 1) + i:pad - (CONV_W - 1) + i + L, :] * cw_ref[i:i + 1, :]
    prev_scr[...] = ubuf[lb:lb + pad, :]
    uh = (uc * _sigmoid(uc)).astype(BF16)

    vm = _pad_rows(vm_ref[...], L)
    om = _pad_rows(om_ref[...], L)
    rowv = lax.broadcasted_iota(I32, (L, LANES), 0) < lb
    li = jnp.where(rowv, _pad_rows(ig_ref[...], L) + bg_ref[0:1, :], NEG_BIG)
    xf = _pad_rows(fg_ref[...], L) + bg_ref[1:2, :]
    lf = jnp.where(rowv, jnp.minimum(xf, 0.0) - jnp.log(1.0 + jnp.exp(-jnp.abs(xf))), 0.0)
    b = jnp.dot(tri_ref[...], lf, precision=lax.Precision.HIGHEST, preferred_element_type=F32)
    c = li - b
    c_t = c.T
    m_prev = m_scr[...]
    inter = b + m_prev
    b_last = b[L - 1:L, :]
    dec = b_last - b + li
    m_new = jnp.maximum(b_last + m_prev, jnp.max(dec, axis=0, keepdims=True))
    wts = jnp.exp(dec - m_new)
    wts_t = wts.T
    sc = jnp.exp(b_last + m_prev - m_new)
    causal = lax.broadcasted_iota(I32, (L, L), 0) >= lax.broadcasted_iota(I32, (L, L), 1)
    k_scale = hd ** -0.5

    for h in range(n_heads):
        sl = slice(h * hd, (h + 1) * hd)
        uh_h = uh[:, sl]
        q_h = _dot(uh_h, wq_ref[h])
        k_h = _dot(uh_h, wk_ref[h]) * k_scale
        kt_h = _dot_t(wkt_ref[h], uh_h) * k_scale
        v_h = vm[:, sl]
        dmat = jnp.where(causal, b[:, h:h + 1] + c_t[h:h + 1, :], -jnp.inf)
        inter_h = inter[:, h:h + 1]
        m_t = jnp.maximum(inter_h, jnp.max(dmat, axis=-1, keepdims=True))
        q_b = q_h.astype(BF16)
        sw = jnp.exp(dmat - m_t) * _dot_t(q_b, k_h.astype(BF16))
        sp = jnp.exp(inter_h - m_t)
        c_h = c_scr[h]
        n_h = n_scr[h:h + 1, :]
        num = sp * _dot(q_b, c_h.astype(BF16)) + _dot(sw.astype(BF16), v_h)
        den = sp * jnp.sum(q_h * n_h, axis=-1, keepdims=True) + jnp.sum(sw, axis=-1, keepdims=True)
        hh = num / jnp.maximum(jnp.abs(den), jnp.exp(-m_t))
        hn = _rms(hh, hg_ref[:, sl])
        out = _sigmoid(om[:, sl]) * hn
        mo_ref[:, sl] = out[:lb].astype(BF16)
        sc_h = sc[:, h:h + 1]
        c_scr[h] = sc_h * c_h + _dot((kt_h * wts_t[h:h + 1, :]).astype(BF16), v_h)
        n_scr[h:h + 1, :] = sc_h * n_h + jnp.sum(k_h * wts[:, h:h + 1], axis=0, keepdims=True)
    m_scr[...] = m_new

    @pl.when(t == nt - 1)
    def _():
        c_out[0] = c_scr[...]
        n_out[0] = n_scr[...]
        m_out[0] = m_scr[...]
        cv_out[0] = prev_scr[...]


def _mlstm(u, vm, om, ig, fg, c0, n0, m0, cv0, cw, cb, wq, wk, wkt, bg, hg, tri, *, b, L, lb):
    m, mw = u.shape
    nt = m // (b * lb)
    n_heads = mw // M_HEAD_DIM
    row = lambda w: pl.BlockSpec((lb, w), lambda bi, ti: (bi * nt + ti, 0))
    const = lambda a: pl.BlockSpec(a.shape, lambda bi, ti: (0,) * a.ndim)
    perb = lambda a: pl.BlockSpec((1,) + a.shape[1:], lambda bi, ti: (bi,) + (0,) * (a.ndim - 1))
    out_shapes = [
        jax.ShapeDtypeStruct((m, mw), BF16),
        jax.ShapeDtypeStruct(c0.shape, F32),
        jax.ShapeDtypeStruct(n0.shape, F32),
        jax.ShapeDtypeStruct(m0.shape, F32),
        jax.ShapeDtypeStruct(cv0.shape, F32),
    ]
    return pl.pallas_call(
        functools.partial(_mlstm_kernel, L=L, lb=lb, n_heads=n_heads),
        grid=(b, nt),
        in_specs=[row(mw), row(mw), row(mw), row(LANES), row(LANES),
                  perb(c0), perb(n0), perb(m0), perb(cv0),
                  const(cw), const(cb), const(wq), const(wk), const(wkt), const(bg), const(hg), const(tri)],
        out_specs=[row(mw), perb(c0), perb(n0), perb(m0), perb(cv0)],
        out_shape=out_shapes,
        scratch_shapes=[
            pltpu.VMEM(c0.shape[1:], F32),
            pltpu.VMEM(n0.shape[1:], F32),
            pltpu.VMEM(m0.shape[1:], F32),
            pltpu.VMEM((SUBLANES, mw), F32),
            pltpu.VMEM((L + SUBLANES, mw), F32),
        ],
        compiler_params=pltpu.CompilerParams(
            dimension_semantics=("arbitrary", "arbitrary"), vmem_limit_bytes=VMEM_LIMIT),
        name="mlstm",
    )(u, vm, om, ig, fg, c0, n0, m0, cv0, cw, cb, wq, wk, wkt, bg, hg, tri)


def _ffn_kernel(x_ref, a_ref, mo_ref, wo_ref, g2_ref, wu_ref, wd_ref, y_ref, hn_scr):
    j = pl.program_id(1)

    @pl.when(j == 0)
    def _():
        aw = a_ref.shape[1]
        h = x_ref[...] + _dot(a_ref[...], wo_ref[:aw, :]) + _dot(mo_ref[...], wo_ref[aw:, :])
        y_ref[...] = h
        hn_scr[...] = _rms(h, g2_ref[...]).astype(BF16)

    f = jnp.maximum(_dot(hn_scr[...], wu_ref[...]), 0.0)
    y_ref[...] += _dot((f * f).astype(BF16), wd_ref[...])


def _ffn(x, a, mo, wo, g2, wu, wd, *, tm, tf):
    m, d = x.shape
    dff = wu.shape[1]
    row = lambda w: pl.BlockSpec((tm, w), lambda i, j: (i, 0))
    const = lambda arr: pl.BlockSpec(arr.shape, lambda i, j: (0, 0), pipeline_mode=pl.Buffered(1))
    return pl.pallas_call(
        _ffn_kernel,
        grid=(m // tm, dff // tf),
        in_specs=[row(d), row(a.shape[1]), row(mo.shape[1]), const(wo), const(g2),
                  pl.BlockSpec((d, tf), lambda i, j: (0, j)),
                  pl.BlockSpec((tf, d), lambda i, j: (j, 0))],
        out_specs=row(d),
        out_shape=jax.ShapeDtypeStruct((m, d), F32),
        scratch_shapes=[pltpu.VMEM((tm, d), BF16)],
        compiler_params=pltpu.CompilerParams(
            dimension_semantics=("arbitrary", "arbitrary"), vmem_limit_bytes=VMEM_LIMIT),
        name="ffn",
    )(x, a, mo, wo, g2, wu, wd)


def _rope_tables(pos, rot, width):
    half = rot // 2
    inv_freq = ROPE_THETA ** (-jnp.arange(half, dtype=F32) / half)
    ang = pos.astype(F32)[:, None] * inv_freq[None, :]
    cos, sin = jnp.cos(ang), jnp.sin(ang)
    n = pos.shape[0]
    rest1 = jnp.ones((n, width - rot), F32)
    rest0 = jnp.zeros((n, width - rot), F32)
    z = jnp.zeros_like(sin)
    c = jnp.concatenate([cos, cos, rest1], axis=1)
    sa = jnp.concatenate([-sin, z, rest0], axis=1)
    sb = jnp.concatenate([z, sin, rest0], axis=1)
    rep = LANES // width
    return jnp.concatenate([jnp.tile(c, (1, rep)), jnp.tile(sa, (1, rep)), jnp.tile(sb, (1, rep))], axis=1)


def _pad_cols(w, n):
    return jnp.pad(w, ((0, 0), (0, n - w.shape[1])))


def _layer(x, pos_rows, b, mode, state, params, splits, *, tm_proj, tm_ffn, tf, tq, tk, L, lb):
    (norm1_g, w_in, q_norm_g, k_norm_g, conv_w, conv_b, wq_m, wk_m, b_igate, b_fgate,
     hnorm_g, w_out, norm2_g, w_up, w_down) = params
    bsz, t, d = x.shape
    m = bsz * t
    x2 = x.reshape(m, d)
    offs = np.cumsum((0,) + tuple(splits))
    col = lambda i: w_in[:, int(offs[i]):int(offs[i + 1])]
    aw, mw = splits[0], splits[6]
    n_q_heads = aw // A_HEAD_DIM
    n_m_heads = mw // M_HEAD_DIM
    wa = jnp.concatenate(
        [col(0), col(1), col(2), col(3), col(4), col(4),
         _pad_cols(col(5), LANES), _pad_cols(col(9), LANES), _pad_cols(col(10), LANES)], axis=1).astype(BF16)
    wb = jnp.concatenate([col(6), col(7), col(8)], axis=1).astype(BF16)
    ra = _rope_tables(pos_rows, A_HEAD_DIM // 4, A_HEAD_DIM)
    ri = _rope_tables(pos_rows, IDX_DIM // 4, IDX_DIM)

    (q, k, v, ki, kb, vb, kiab, qi, wi, ig, fg, u, vm, om) = _proj(
        x2, norm1_g[None], wa, wb, q_norm_g[None], k_norm_g[None], ra, ri,
        tm=tm_proj, n_q_heads=n_q_heads, m_width=mw)

    kvw = A_KV_HEADS * A_HEAD_DIM
    r3 = lambda a_: a_.reshape(bsz, t, a_.shape[1])
    if mode == "prompt":
        c0 = jnp.zeros((bsz, n_m_heads, M_HEAD_DIM, M_HEAD_DIM), F32)
        n0 = jnp.zeros((bsz, n_m_heads, M_HEAD_DIM), F32)
        m0 = jnp.zeros((bsz, 1, LANES), F32)
        cv0 = jnp.zeros((bsz, SUBLANES, mw), F32)
        n_sel = min(TOPK_MAX, t // 4)
        a = _attn(r3(q), r3(qi), r3(wi), r3(kb), r3(vb), r3(kiab),
                  tq=tq, tk=tk, causal=True, s_valid=t, n_sel=n_sel)
    else:
        ck, cv, ckidx, s_c, s_n, s_m, s_conv = state
        p = ck.shape[1]
        s_valid = p + t
        s_pad = -(-s_valid // tk) * tk
        padk = lambda a_: jnp.pad(a_, ((0, 0), (0, s_pad - s_valid), (0, 0)))
        k_all = padk(jnp.concatenate([ck.reshape(bsz, p, kvw).astype(BF16), r3(kb)], axis=1))
        v_all = padk(jnp.concatenate([cv.reshape(bsz, p, kvw).astype(BF16), r3(vb)], axis=1))
        cki = ckidx.astype(BF16)
        zki = jnp.zeros_like(cki)
        ki_c = jnp.concatenate([cki, zki, zki, cki], axis=2)
        ki_all = padk(jnp.concatenate([ki_c, r3(kiab)], axis=1))
        c0, n0 = s_c.astype(F32), s_n.astype(F32)
        m0 = jnp.pad(s_m.astype(F32), ((0, 0), (0, LANES - n_m_heads)))[:, None, :]
        cv0 = jnp.pad(s_conv.astype(F32), ((0, 0), (SUBLANES - (CONV_W - 1), 0), (0, 0)))
        n_sel = min(TOPK_MAX, s_valid // 4)
        a = _attn(r3(q), r3(qi), r3(wi), k_all, v_all, ki_all,
                  tq=tq, tk=tk, causal=False, s_valid=s_valid, n_sel=n_sel)

    bg = jnp.stack([_pad_cols(b_igate[None], LANES)[0], _pad_cols(b_fgate[None], LANES)[0]])
    tri = (np.arange(L)[:, None] >= np.arange(L)[None, :]).astype(np.float32)
    mo, c_new, n_new, m_new, cv_new = _mlstm(
        u, vm, om, ig, fg, c0, n0, m0, cv0,
        conv_w, conv_b[None], wq_m.astype(BF16), wk_m.astype(BF16),
        jnp.swapaxes(wk_m, 1, 2).astype(BF16), bg, hnorm_g[None], jnp.asarray(tri),
        b=bsz, L=L, lb=lb)

    y = _ffn(x2, a.reshape(m, aw), mo, w_out.astype(BF16), norm2_g[None],
             w_up.astype(BF16), w_down.astype(BF16), tm=tm_ffn, tf=tf)
    new_state = (
        k.reshape(bsz, t, A_KV_HEADS, A_HEAD_DIM), v.reshape(bsz, t, A_KV_HEADS, A_HEAD_DIM),
        ki.reshape(bsz, t, IDX_DIM), c_new, n_new, m_new[:, 0, :n_m_heads],
        cv_new[:, SUBLANES - (CONV_W - 1):, :])
    return y.reshape(bsz, t, d), new_state


def kernel(x_prompt, x_sample, cache_k, cache_v, cache_kidx, state_mlstm_C, state_mlstm_n, state_mlstm_m,
           state_conv, norm1_g, w_in, q_norm_g, k_norm_g, conv_w, conv_b, wq_m, wk_m, b_igate, b_fgate,
           hnorm_g, w_out, norm2_g, w_up, w_down):
    depth = w_in.shape[0]
    d = x_prompt.shape[-1]
    mix = w_out.shape[1]
    aw = mix // 2
    mw = mix - aw
    kvw = A_KV_HEADS * A_HEAD_DIM
    n_m_heads = mw // M_HEAD_DIM
    splits = (aw, kvw, kvw, IDX_HEADS * IDX_DIM, IDX_DIM, IDX_HEADS, mw, mw, mw, n_m_heads, n_m_heads)
    t_p = x_prompt.shape[1]
    b_s, t_s = x_sample.shape[0], x_sample.shape[1]
    past = cache_k.shape[2]

    y_p, y_s = x_prompt, x_sample
    new_p, new_s = [], []
    for l in range(depth):
        params = (norm1_g[l], w_in[l], q_norm_g[l], k_norm_g[l], conv_w[l], conv_b[l], wq_m[l], wk_m[l],
                  b_igate[l], b_fgate[l], hnorm_g[l], w_out[l], norm2_g[l], w_up[l], w_down[l])
        y_p, st_p = _layer(
            y_p, jnp.arange(t_p), None, "prompt", None, params, splits,
            tm_proj=256, tm_ffn=512, tf=512, tq=128, tk=512, L=128, lb=128)
        y_s, st_s = _layer(
            y_s, past + (jnp.arange(b_s * t_s) % t_s), None, "sample",
            (cache_k[l], cache_v[l], cache_kidx[l], state_mlstm_C[l], state_mlstm_n[l],
             state_mlstm_m[l], state_conv[l]), params, splits,
            tm_proj=b_s * t_s, tm_ffn=b_s * t_s, tf=512, tq=LANES, tk=384, L=128, lb=t_s)
        new_p.append(st_p)
        new_s.append(st_s)

    stack = lambda states, i: jnp.stack([s[i] for s in states])
    return (y_p, y_s,
            *[stack(new_p, i) for i in range(7)],
            *[stack(new_s, i) for i in range(7)])
```

```python
import functools

import jax
import jax.numpy as jnp
import numpy as np
from jax import lax
from jax.experimental import pallas as pl
from jax.experimental.pallas import tpu as pltpu

F32 = jnp.float32
BF16 = jnp.bfloat16
I32 = jnp.int32

EPS = 1e-6
ROPE_THETA = 500000.0
CHUNK = 64
CHUNK_LOG2 = 6
TOPK_MAX = 256
A_HEAD_DIM = 128
A_KV_HEADS = 2
IDX_HEADS = 8
IDX_DIM = 64
M_HEAD_DIM = 128
CONV_W = 4

LANES = 128
SUBLANES = 8
VMEM_LIMIT = 56 * 1024 * 1024
NEG_BIG = -1e30
INT_MIN = -(2 ** 31)
BIG_J = 2 ** 30


def _rms(x, g):
    r = lax.rsqrt(jnp.mean(x * x, axis=-1, keepdims=True) + EPS)
    return x * r * g


def _rope(t, c, sa, sb, half):
    return t * c + pltpu.roll(t, LANES - half, 1) * sa + pltpu.roll(t, half, 1) * sb


def _sigmoid(x):
    return 1.0 / (1.0 + jnp.exp(-x))


def _dot(a, b):
    return jnp.dot(a, b, preferred_element_type=F32)


def _dot_t(a, b):
    return lax.dot_general(a, b, (((1,), (1,)), ((), ())), preferred_element_type=F32)


def _proj_kernel(x_ref, g1_ref, wa_ref, wb_ref, qg_ref, kg_ref, ra_ref, ri_ref,
                 q_out, k_out, v_out, ki_out, kb_out, vb_out, kiab_out, qi_out,
                 wi_out, ig_out, fg_out, u_out, vm_out, om_out, *, n_q_heads):
    xn = _rms(x_ref[...], g1_ref[...]).astype(BF16)
    ra = ra_ref[...]
    ca, saa, sba = ra[:, :LANES], ra[:, LANES:2 * LANES], ra[:, 2 * LANES:]
    ri = ri_ref[...]
    ci, sai, sbi = ri[:, :LANES], ri[:, LANES:2 * LANES], ri[:, 2 * LANES:]
    qg = qg_ref[...]
    kg = kg_ref[...]
    aw = n_q_heads * A_HEAD_DIM
    kvw = A_KV_HEADS * A_HEAD_DIM
    q_scale = A_HEAD_DIM ** -0.5
    i_scale = IDX_DIM ** -0.5

    for c in range(0, aw, 512):
        z = _dot(xn, wa_ref[:, c:c + 512])
        for j in range(0, 512, LANES):
            qh = _rope(_rms(z[:, j:j + LANES], qg), ca, saa, sba, A_HEAD_DIM // 8) * q_scale
            q_out[:, c + j:c + j + LANES] = qh.astype(BF16)
    off = aw
    z = _dot(xn, wa_ref[:, off:off + 2 * kvw])
    for j in range(0, kvw, LANES):
        kh = _rope(_rms(z[:, j:j + LANES], kg), ca, saa, sba, A_HEAD_DIM // 8)
        k_out[:, j:j + LANES] = kh
        kb_out[:, j:j + LANES] = kh.astype(BF16)
    vv = z[:, kvw:]
    v_out[...] = vv
    vb_out[...] = vv.astype(BF16)
    off += 2 * kvw
    iw = IDX_HEADS * IDX_DIM
    z = _dot(xn, wa_ref[:, off:off + iw])
    for j in range(0, iw, LANES):
        qi = _rope(z[:, j:j + LANES], ci, sai, sbi, IDX_DIM // 8) * i_scale
        qi_out[:, j:j + LANES] = qi.astype(BF16)
    off += iw
    z = _dot(xn, wa_ref[:, off:off + 4 * LANES])
    ki2 = _rope(z[:, :LANES], ci, sai, sbi, IDX_DIM // 8)
    ki_out[...] = ki2[:, :IDX_DIM]
    lane = lax.broadcasted_iota(I32, (1, LANES), 1)
    lo = lane < IDX_DIM
    kiab_out[:, :LANES] = jnp.where(lo, ki2, 0.0).astype(BF16)
    kiab_out[:, LANES:] = jnp.where(lo, 0.0, ki2).astype(BF16)
    wi_out[...] = z[:, LANES:2 * LANES]
    ig_out[...] = z[:, 2 * LANES:3 * LANES]
    fg_out[...] = z[:, 3 * LANES:]
    mw = u_out.shape[1]
    for c in range(0, mw, 512):
        u_out[:, c:c + 512] = _dot(xn, wb_ref[:, c:c + 512])
        vm_out[:, c:c + 512] = _dot(xn, wb_ref[:, mw + c:mw + c + 512]).astype(BF16)
        om_out[:, c:c + 512] = _dot(xn, wb_ref[:, 2 * mw + c:2 * mw + c + 512])


def _proj(x, g1, wa, wb, qg, kg, ra, ri, *, tm, n_q_heads, m_width):
    m, d = x.shape
    n_tab = ra.shape[0] // tm
    aw = n_q_heads * A_HEAD_DIM
    kvw = A_KV_HEADS * A_HEAD_DIM
    iw = IDX_HEADS * IDX_DIM
    row = lambda w: pl.BlockSpec((tm, w), lambda i: (i, 0))
    const = lambda a: pl.BlockSpec(a.shape, lambda i: (0, 0), pipeline_mode=pl.Buffered(1))
    tab = pl.BlockSpec((tm, 3 * LANES), lambda i: (i % n_tab, 0))
    outs = [
        (aw, BF16), (kvw, F32), (kvw, F32), (IDX_DIM, F32), (kvw, BF16), (kvw, BF16),
        (2 * LANES, BF16), (iw, BF16), (LANES, F32), (LANES, F32), (LANES, F32),
        (m_width, F32), (m_width, BF16), (m_width, F32),
    ]
    return pl.pallas_call(
        functools.partial(_proj_kernel, n_q_heads=n_q_heads),
        grid=(m // tm,),
        in_specs=[row(d), const(g1), const(wa), const(wb), const(qg), const(kg), tab, tab],
        out_specs=[row(w) for w, _ in outs],
        out_shape=[jax.ShapeDtypeStruct((m, w), dt) for w, dt in outs],
        compiler_params=pltpu.CompilerParams(
            dimension_semantics=("arbitrary",), vmem_limit_bytes=VMEM_LIMIT),
        name="proj",
    )(x, g1, wa, wb, qg, kg, ra, ri)


def _pad_rows(x, n):
    if x.shape[0] == n:
        return x
    return jnp.concatenate([x, jnp.zeros((n - x.shape[0],) + x.shape[1:], x.dtype)], axis=0)


V_EXT_ROWS = A_HEAD_DIM + 16


def _attn_kernel(q_ref, qi_ref, wi_ref, k_ref, vt_ref, ki_ref, o_ref,
                 key_scr, qs_scr, qis_scr, m_scr, acc_scr, sa_scr, sb_scr,
                 *, tq, tqb, tk, causal, s_valid, n_sel, n_tiles, group):
    start = pl.program_id(1) * tqb
    nr = tk // SUBLANES
    n_acc = 8
    lane = lax.broadcasted_iota(I32, (1, tq), 1)
    rowi = lax.broadcasted_iota(I32, (tk, 1), 0)
    row3 = (lax.broadcasted_iota(I32, (nr, SUBLANES, 1), 0) * SUBLANES
            + lax.broadcasted_iota(I32, (nr, SUBLANES, 1), 1))
    if causal:
        limit = (lax.shift_right_logical(start + lane, CHUNK_LOG2) + 1) * CHUNK
        nkt = jnp.minimum((start + tqb + tk - 1) // tk, n_tiles)
    else:
        limit = jnp.full((1, tq), s_valid, I32)
        nkt = n_tiles

    q = _pad_rows(q_ref[0], tq)
    for g in range(A_KV_HEADS):
        for r in range(group):
            h = g * group + r
            qs_scr[g, r * tq:(r + 1) * tq, :] = q[:, h * LANES:(h + 1) * LANES]
    qi = _pad_rows(qi_ref[0], tq)
    for j in range(IDX_HEADS // 2):
        qis_scr[j * tq:(j + 1) * tq, :] = qi[:, j * LANES:(j + 1) * LANES]
    w_t = jnp.transpose(_pad_rows(wi_ref[0], tq))[:IDX_HEADS] * (IDX_HEADS ** -0.5)

    def score_tile(kt, carry):
        off = pl.multiple_of(kt * tk, tk)
        kiab = ki_ref[0, pl.ds(off, tk), :]
        acc = jnp.zeros((tk, tq), F32)
        for half in range(IDX_HEADS // 4):
            qh = qis_scr[half * 2 * tq:(half + 1) * 2 * tq, :]
            for par in range(2):
                d = _dot_t(kiab[:, par * LANES:(par + 1) * LANES], qh)
                for jj in range(2):
                    h = 2 * (2 * half + jj) + par
                    acc = acc + w_t[h:h + 1, :] * jnp.maximum(d[:, jj * tq:(jj + 1) * tq], 0.0)
        bits = lax.bitcast_convert_type(acc, I32)
        key = bits ^ (lax.shift_right_arithmetic(bits, 31) & 0x7FFFFFFF)
        key = jnp.where(key == -1, 0, key)
        key_scr[kt] = jnp.where(off + rowi < limit, key, INT_MIN)
        return carry

    lax.fori_loop(0, nkt, score_tile, 0)

    def count_tiles(pred):
        def body(kt, acc):
            hit = pred(key_scr[kt].reshape(nr, SUBLANES, tq), kt * tk + row3)
            ones = jnp.where(hit, 1.0, 0.0).reshape(nr // n_acc, n_acc, SUBLANES, tq)
            return acc + jnp.sum(ones, axis=0)
        acc = lax.fori_loop(0, nkt, body, jnp.zeros((n_acc, SUBLANES, tq), F32))
        return jnp.sum(jnp.sum(acc, axis=0), axis=0, keepdims=True)

    def bit_step(i, carry):
        t_cur, c_cur = carry
        cand = t_cur + lax.shift_left(jnp.int32(1), 31 - i)
        cnt = count_tiles(lambda key, col: key >= cand[None])
        ok = cnt >= n_sel
        return jnp.where(ok, cand, t_cur), jnp.where(ok, cnt, c_cur)

    t_fin, c_fin = lax.fori_loop(
        0, 32, bit_step,
        (jnp.full((1, tq), INT_MIN, I32), jnp.full((1, tq), float(n_sel), F32)))
    is_min = t_fin == INT_MIN
    needs = jnp.logical_and(jnp.logical_and(jnp.logical_not(is_min), c_fin > n_sel), lane < tqb)
    j_fast = jnp.where(is_min, 0, BIG_J)

    def tie_search():
        rem = n_sel - count_tiles(lambda key, col: key > t_fin[None])

        def j_step(i, j_cur):
            cand = j_cur + lax.shift_left(jnp.int32(1), 14 - i)
            f = count_tiles(lambda key, col: jnp.logical_and(key == t_fin[None], col < cand[None]))
            return jnp.where(f <= rem, cand, j_cur)

        j_slow = lax.fori_loop(0, 15, j_step, jnp.zeros((1, tq), I32))
        return jnp.where(needs, j_slow, j_fast)

    j_fin = lax.cond(jnp.max(jnp.where(needs, 1.0, 0.0)) > 0.0, tie_search, lambda: j_fast)

    m_scr[...] = jnp.full(m_scr.shape, NEG_BIG, F32)
    acc_scr[...] = jnp.zeros(acc_scr.shape, F32)

    s_bufs = (sa_scr, sb_scr)

    def stage_logits(tile, par):
        t_c = jnp.minimum(tile, nkt - 1)
        off = pl.multiple_of(t_c * tk, tk)
        for g in range(A_KV_HEADS):
            s_bufs[par][g] = _dot_t(k_ref[0, pl.ds(off, tk), g * LANES:(g + 1) * LANES], qs_scr[g])

    def attend_tile(tile, par):
        t_c = jnp.minimum(tile, nkt - 1)
        key = key_scr[t_c]
        sel = jnp.logical_or(key > t_fin, jnp.logical_and(key == t_fin, t_c * tk + rowi < j_fin))
        sel = jnp.logical_and(sel, tile < nkt)
        bias = jnp.where(sel, 0.0, -jnp.inf)
        bias = jnp.concatenate([bias] * group, axis=1)
        for g in range(A_KV_HEADS):
            s = s_bufs[par][g] + bias
            m_old = m_scr[g]
            m_new = jnp.maximum(m_old, jnp.max(s, axis=0, keepdims=True))
            alpha = jnp.exp(m_old - m_new)
            p = jnp.exp(s - m_new).astype(BF16)
            acc_scr[g] = alpha * acc_scr[g] + _dot(vt_ref[0, t_c, g], p)
            m_scr[g] = m_new

    stage_logits(0, 0)

    def attend_pair(i, carry):
        stage_logits(2 * i + 1, 1)
        attend_tile(2 * i, 0)
        stage_logits(2 * i + 2, 0)
        attend_tile(2 * i + 1, 1)
        return carry

    lax.fori_loop(0, (nkt + 1) // 2, attend_pair, 0)
    for g in range(A_KV_HEADS):
        acc = acc_scr[g]
        out_t = acc[:A_HEAD_DIM] / acc[A_HEAD_DIM:A_HEAD_DIM + 1]
        for r in range(group):
            h = g * group + r
            o = jnp.transpose(out_t[:, r * tq:(r + 1) * tq])
            o_ref[0, :, h * LANES:(h + 1) * LANES] = o[:tqb].astype(BF16)


def _attn(q, qi, wi, kb, vb, kiab, *, tq, tk, causal, s_valid, n_sel):
    b, t, aw = q.shape
    tqb = min(tq, t)
    s_pad = kb.shape[1]
    n_tiles = s_pad // tk
    n_heads = aw // A_HEAD_DIM
    group = n_heads // A_KV_HEADS
    vt = jnp.transpose(vb.reshape(b, s_pad, A_KV_HEADS, A_HEAD_DIM), (0, 2, 3, 1))
    vt = jnp.concatenate([vt, jnp.ones((b, A_KV_HEADS, V_EXT_ROWS - A_HEAD_DIM, s_pad), BF16)], axis=2)
    vt = jnp.transpose(vt.reshape(b, A_KV_HEADS, V_EXT_ROWS, n_tiles, tk), (0, 3, 1, 2, 4))
    qspec = lambda w: pl.BlockSpec((1, tqb, w), lambda bi, qi_: (bi, qi_, 0))
    kspec = lambda w: pl.BlockSpec((1, s_pad, w), lambda bi, qi_: (bi, 0, 0))
    kern = functools.partial(_attn_kernel, tq=tq, tqb=tqb, tk=tk, causal=causal, s_valid=s_valid,
                             n_sel=n_sel, n_tiles=n_tiles, group=group)
    return pl.pallas_call(
        kern,
        grid=(b, t // tqb),
        in_specs=[qspec(aw), qspec(qi.shape[2]), qspec(LANES),
                  kspec(kb.shape[2]),
                  pl.BlockSpec((1,) + vt.shape[1:], lambda bi, qi_: (bi, 0, 0, 0, 0)),
                  kspec(kiab.shape[2])],
        out_specs=qspec(aw),
        out_shape=jax.ShapeDtypeStruct((b, t, aw), BF16),
        scratch_shapes=[
            pltpu.VMEM((n_tiles, tk, tq), I32),
            pltpu.VMEM((A_KV_HEADS, group * tq, LANES), BF16),
            pltpu.VMEM((IDX_HEADS // 2 * tq, LANES), BF16),
            pltpu.VMEM((A_KV_HEADS, 1, group * tq), F32),
            pltpu.VMEM((A_KV_HEADS, V_EXT_ROWS, group * tq), F32),
            pltpu.VMEM((A_KV_HEADS, tk, group * tq), F32),
            pltpu.VMEM((A_KV_HEADS, tk, group * tq), F32),
        ],
        compiler_params=pltpu.CompilerParams(
            dimension_semantics=("arbitrary", "arbitrary"), vmem_limit_bytes=VMEM_LIMIT),
        name="attn",
    )(q, qi, wi, kb, vt, kiab)


def _mlstm_kernel(u_ref, vm_ref, om_ref, ig_ref, fg_ref, c0_ref, n0_ref, m0_ref, cv0_ref,
                  cw_ref, cb_ref, wq_ref, wk_ref, wkt_ref, bg_ref, hg_ref, tri_ref,
                  mo_ref, c_out, n_out, m_out, cv_out,
                  c_scr, n_scr, m_scr, prev_scr, ubuf,
                  *, L, lb, n_heads):
    t = pl.program_id(1)
    nt = pl.num_programs(1)
    hd = M_HEAD_DIM
    pad = SUBLANES

    @pl.when(t == 0)
    def _():
        c_scr[...] = c0_ref[0]
        n_scr[...] = n0_ref[0]
        m_scr[...] = m0_ref[0]
        prev_scr[...] = cv0_ref[0]

    ubuf[0:pad, :] = prev_scr[...]
    ubuf[pad:pad + L, :] = _pad_rows(u_ref[...], L)
    uc = cb_ref[...]
    for i in range(CONV_W):
        uc = uc + ubuf[pad - (CONV_W - 1) + i:pad - (CONV_W - 1) + i + L, :] * cw_ref[i:i + 1, :]
    prev_scr[...] = ubuf[lb:lb + pad, :]
    uh = (uc * _sigmoid(uc)).astype(BF16)

    vm = _pad_rows(vm_ref[...], L)
    om = _pad_rows(om_ref[...], L)
    rowv = lax.broadcasted_iota(I32, (L, LANES), 0) < lb
    li = jnp.where(rowv, _pad_rows(ig_ref[...], L) + bg_ref[0:1, :], NEG_BIG)
    xf = _pad_rows(fg_ref[...], L) + bg_ref[1:2, :]
    lf = jnp.where(rowv, jnp.minimum(xf, 0.0) - jnp.log(1.0 + jnp.exp(-jnp.abs(xf))), 0.0)
    b = jnp.dot(tri_ref[...], lf, precision=lax.Precision.HIGHEST, preferred_element_type=F32)
    c = li - b
    c_t = c.T
    m_prev = m_scr[...]
    inter = b + m_prev
    b_last = b[L - 1:L, :]
    dec = b_last - b + li
    m_new = jnp.maximum(b_last + m_prev, jnp.max(dec, axis=0, keepdims=True))
    wts = jnp.exp(dec - m_new)
    wts_t = wts.T
    sc = jnp.exp(b_last + m_prev - m_new)
    causal = lax.broadcasted_iota(I32, (L, L), 0) >= lax.broadcasted_iota(I32, (L, L), 1)
    k_scale = hd ** -0.5

    for h in range(n_heads):
        sl = slice(h * hd, (h + 1) * hd)
        uh_h = uh[:, sl]
        q_h = _dot(uh_h, wq_ref[h])
        k_h = _dot(uh_h, wk_ref[h]) * k_scale
        kt_h = _dot_t(wkt_ref[h], uh_h) * k_scale
        v_h = vm[:, sl]
        dmat = jnp.where(causal, b[:, h:h + 1] + c_t[h:h + 1, :], -jnp.inf)
        inter_h = inter[:, h:h + 1]
        m_t = jnp.maximum(inter_h, jnp.max(dmat, axis=-1, keepdims=True))
        q_b = q_h.astype(BF16)
        sw = jnp.exp(dmat - m_t) * _dot_t(q_b, k_h.astype(BF16))
        sp = jnp.exp(inter_h - m_t)
        c_h = c_scr[h]
        n_h = n_scr[h:h + 1, :]
        num = sp * _dot(q_b, c_h.astype(BF16)) + _dot(sw.astype(BF16), v_h)
        den = sp * jnp.sum(q_h * n_h, axis=-1, keepdims=True) + jnp.sum(sw, axis=-1, keepdims=True)
        hh = num / jnp.maximum(jnp.abs(den), jnp.exp(-m_t))
        hn = _rms(hh, hg_ref[:, sl])
        out = _sigmoid(om[:, sl]) * hn
        mo_ref[:, sl] = out[:lb].astype(BF16)
        sc_h = sc[:, h:h + 1]
        c_scr[h] = sc_h * c_h + _dot((kt_h * wts_t[h:h + 1, :]).astype(BF16), v_h)
        n_scr[h:h + 1, :] = sc_h * n_h + jnp.sum(k_h * wts[:, h:h + 1], axis=0, keepdims=True)
    m_scr[...] = m_new

    @pl.when(t == nt - 1)
    def _():
        c_out[0] = c_scr[...]
        n_out[0] = n_scr[...]
        m_out[0] = m_scr[...]
        cv_out[0] = prev_scr[...]


def _mlstm(u, vm, om, ig, fg, c0, n0, m0, cv0, cw, cb, wq, wk, wkt, bg, hg, tri, *, b, L, lb):
    m, mw = u.shape
    nt = m // (b * lb)
    n_heads = mw // M_HEAD_DIM
    row = lambda w: pl.BlockSpec((lb, w), lambda bi, ti: (bi * nt + ti, 0))
    const = lambda a: pl.BlockSpec(a.shape, lambda bi, ti: (0,) * a.ndim)
    perb = lambda a: pl.BlockSpec((1,) + a.shape[1:], lambda bi, ti: (bi,) + (0,) * (a.ndim - 1))
    out_shapes = [
        jax.ShapeDtypeStruct((m, mw), BF16),
        jax.ShapeDtypeStruct(c0.shape, F32),
        jax.ShapeDtypeStruct(n0.shape, F32),
        jax.ShapeDtypeStruct(m0.shape, F32),
        jax.ShapeDtypeStruct(cv0.shape, F32),
    ]
    return pl.pallas_call(
        functools.partial(_mlstm_kernel, L=L, lb=lb, n_heads=n_heads),
        grid=(b, nt),
        in_specs=[row(mw), row(mw), row(mw), row(LANES), row(LANES),
                  perb(c0), perb(n0), perb(m0), perb(cv0),
                  const(cw), const(cb), const(wq), const(wk), const(wkt), const(bg), const(hg), const(tri)],
        out_specs=[row(mw), perb(c0), perb(n0), perb(m0), perb(cv0)],
        out_shape=out_shapes,
        scratch_shapes=[
            pltpu.VMEM(c0.shape[1:], F32),
            pltpu.VMEM(n0.shape[1:], F32),
            pltpu.VMEM(m0.shape[1:], F32),
            pltpu.VMEM((SUBLANES, mw), F32),
            pltpu.VMEM((L + SUBLANES, mw), F32),
        ],
        compiler_params=pltpu.CompilerParams(
            dimension_semantics=("arbitrary", "arbitrary"), vmem_limit_bytes=VMEM_LIMIT),
        name="mlstm",
    )(u, vm, om, ig, fg, c0, n0, m0, cv0, cw, cb, wq, wk, wkt, bg, hg, tri)


def _ffn_kernel(x_ref, a_ref, mo_ref, wo_ref, g2_ref, wu_ref, wd_ref, y_ref, hn_scr):
    j = pl.program_id(1)

    @pl.when(j == 0)
    def _():
        aw = a_ref.shape[1]
        h = x_ref[...] + _dot(a_ref[...], wo_ref[:aw, :]) + _dot(mo_ref[...], wo_ref[aw:, :])
        y_ref[...] = h
        hn_scr[...] = _rms(h, g2_ref[...]).astype(BF16)

    f = jnp.maximum(_dot(hn_scr[...], wu_ref[...]), 0.0)
    y_ref[...] += _dot((f * f).astype(BF16), wd_ref[...])


def _ffn(x, a, mo, wo, g2, wu, wd, *, tm, tf):
    m, d = x.shape
    dff = wu.shape[1]
    row = lambda w: pl.BlockSpec((tm, w), lambda i, j: (i, 0))
    const = lambda arr: pl.BlockSpec(arr.shape, lambda i, j: (0, 0), pipeline_mode=pl.Buffered(1))
    return pl.pallas_call(
        _ffn_kernel,
        grid=(m // tm, dff // tf),
        in_specs=[row(d), row(a.shape[1]), row(mo.shape[1]), const(wo), const(g2),
                  pl.BlockSpec((d, tf), lambda i, j: (0, j)),
                  pl.BlockSpec((tf, d), lambda i, j: (j, 0))],
        out_specs=row(d),
        out_shape=jax.ShapeDtypeStruct((m, d), F32),
        scratch_shapes=[pltpu.VMEM((tm, d), BF16)],
        compiler_params=pltpu.CompilerParams(
            dimension_semantics=("arbitrary", "arbitrary"), vmem_limit_bytes=VMEM_LIMIT),
        name="ffn",
    )(x, a, mo, wo, g2, wu, wd)


def _rope_tables(pos, rot, width):
    half = rot // 2
    inv_freq = ROPE_THETA ** (-jnp.arange(half, dtype=F32) / half)
    ang = pos.astype(F32)[:, None] * inv_freq[None, :]
    cos, sin = jnp.cos(ang), jnp.sin(ang)
    n = pos.shape[0]
    rest1 = jnp.ones((n, width - rot), F32)
    rest0 = jnp.zeros((n, width - rot), F32)
    z = jnp.zeros_like(sin)
    c = jnp.concatenate([cos, cos, rest1], axis=1)
    sa = jnp.concatenate([-sin, z, rest0], axis=1)
    sb = jnp.concatenate([z, sin, rest0], axis=1)
    rep = LANES // width
    return jnp.concatenate([jnp.tile(c, (1, rep)), jnp.tile(sa, (1, rep)), jnp.tile(sb, (1, rep))], axis=1)


def _pad_cols(w, n):
    return jnp.pad(w, ((0, 0), (0, n - w.shape[1])))


def _layer(x, pos_rows, b, mode, state, params, splits, *, tm_proj, tm_ffn, tf, tq, tk, L, lb):
    (norm1_g, w_in, q_norm_g, k_norm_g, conv_w, conv_b, wq_m, wk_m, b_igate, b_fgate,
     hnorm_g, w_out, norm2_g, w_up, w_down) = params
    bsz, t, d = x.shape
    m = bsz * t
    x2 = x.reshape(m, d)
    offs = np.cumsum((0,) + tuple(splits))
    col = lambda i: w_in[:, int(offs[i]):int(offs[i + 1])]
    aw, mw = splits[0], splits[6]
    n_q_heads = aw // A_HEAD_DIM
    n_m_heads = mw // M_HEAD_DIM
    wa = jnp.concatenate(
        [col(0), col(1), col(2), col(3), col(4), col(4),
         _pad_cols(col(5), LANES), _pad_cols(col(9), LANES), _pad_cols(col(10), LANES)], axis=1).astype(BF16)
    wb = jnp.concatenate([col(6), col(7), col(8)], axis=1).astype(BF16)
    ra = _rope_tables(pos_rows, A_HEAD_DIM // 4, A_HEAD_DIM)
    ri = _rope_tables(pos_rows, IDX_DIM // 4, IDX_DIM)

    (q, k, v, ki, kb, vb, kiab, qi, wi, ig, fg, u, vm, om) = _proj(
        x2, norm1_g[None], wa, wb, q_norm_g[None], k_norm_g[None], ra, ri,
        tm=tm_proj, n_q_heads=n_q_heads, m_width=mw)

    kvw = A_KV_HEADS * A_HEAD_DIM
    r3 = lambda a_: a_.reshape(bsz, t, a_.shape[1])
    if mode == "prompt":
        c0 = jnp.zeros((bsz, n_m_heads, M_HEAD_DIM, M_HEAD_DIM), F32)
        n0 = jnp.zeros((bsz, n_m_heads, M_HEAD_DIM), F32)
        m0 = jnp.zeros((bsz, 1, LANES), F32)
        cv0 = jnp.zeros((bsz, SUBLANES, mw), F32)
        n_sel = min(TOPK_MAX, t // 4)
        a = _attn(r3(q), r3(qi), r3(wi), r3(kb), r3(vb), r3(kiab),
                  tq=tq, tk=tk, causal=True, s_valid=t, n_sel=n_sel)
    else:
        ck, cv, ckidx, s_c, s_n, s_m, s_conv = state
        p = ck.shape[1]
        s_valid = p + t
        s_pad = -(-s_valid // tk) * tk
        padk = lambda a_: jnp.pad(a_, ((0, 0), (0, s_pad - s_valid), (0, 0)))
        k_all = padk(jnp.concatenate([ck.reshape(bsz, p, kvw).astype(BF16), r3(kb)], axis=1))
        v_all = padk(jnp.concatenate([cv.reshape(bsz, p, kvw).astype(BF16), r3(vb)], axis=1))
        cki = ckidx.astype(BF16)
        zki = jnp.zeros_like(cki)
        ki_c = jnp.concatenate([cki, zki, zki, cki], axis=2)
        ki_all = padk(jnp.concatenate([ki_c, r3(kiab)], axis=1))
        c0, n0 = s_c.astype(F32), s_n.astype(F32)
        m0 = jnp.pad(s_m.astype(F32), ((0, 0), (0, LANES - n_m_heads)))[:, None, :]
        cv0 = jnp.pad(s_conv.astype(F32), ((0, 0), (SUBLANES - (CONV_W - 1), 0), (0, 0)))
        n_sel = min(TOPK_MAX, s_valid // 4)
        a = _attn(r3(q), r3(qi), r3(wi), k_all, v_all, ki_all,
                  tq=tq, tk=tk, causal=False, s_valid=s_valid, n_sel=n_sel)

    bg = jnp.stack([_pad_cols(b_igate[None], LANES)[0], _pad_cols(b_fgate[None], LANES)[0]])
    tri = (np.arange(L)[:, None] >= np.arange(L)[None, :]).astype(np.float32)
    mo, c_new, n_new, m_new, cv_new = _mlstm(
        u, vm, om, ig, fg, c0, n0, m0, cv0,
        conv_w, conv_b[None], wq_m.astype(BF16), wk_m.astype(BF16),
        jnp.swapaxes(wk_m, 1, 2).astype(BF16), bg, hnorm_g[None], jnp.asarray(tri),
        b=bsz, L=L, lb=lb)

    y = _ffn(x2, a.reshape(m, aw), mo, w_out.astype(BF16), norm2_g[None],
             w_up.astype(BF16), w_down.astype(BF16), tm=tm_ffn, tf=tf)
    new_state = (
        k.reshape(bsz, t, A_KV_HEADS, A_HEAD_DIM), v.reshape(bsz, t, A_KV_HEADS, A_HEAD_DIM),
        ki.reshape(bsz, t, IDX_DIM), c_new, n_new, m_new[:, 0, :n_m_heads],
        cv_new[:, SUBLANES - (CONV_W - 1):, :])
    return y.reshape(bsz, t, d), new_state


def kernel(x_prompt, x_sample, cache_k, cache_v, cache_kidx, state_mlstm_C, state_mlstm_n, state_mlstm_m,
           state_conv, norm1_g, w_in, q_norm_g, k_norm_g, conv_w, conv_b, wq_m, wk_m, b_igate, b_fgate,
           hnorm_g, w_out, norm2_g, w_up, w_down):
    depth = w_in.shape[0]
    d = x_prompt.shape[-1]
    mix = w_out.shape[1]
    aw = mix // 2
    mw = mix - aw
    kvw = A_KV_HEADS * A_HEAD_DIM
    n_m_heads = mw // M_HEAD_DIM
    splits = (aw, kvw, kvw, IDX_HEADS * IDX_DIM, IDX_DIM, IDX_HEADS, mw, mw, mw, n_m_heads, n_m_heads)
    t_p = x_prompt.shape[1]
    b_s, t_s = x_sample.shape[0], x_sample.shape[1]
    past = cache_k.shape[2]

    y_p, y_s = x_prompt, x_sample
    new_p, new_s = [], []
    for l in range(depth):
        params = (norm1_g[l], w_in[l], q_norm_g[l], k_norm_g[l], conv_w[l], conv_b[l], wq_m[l], wk_m[l],
                  b_igate[l], b_fgate[l], hnorm_g[l], w_out[l], norm2_g[l], w_up[l], w_down[l])
        y_p, st_p = _layer(
            y_p, jnp.arange(t_p), None, "prompt", None, params, splits,
            tm_proj=256, tm_ffn=512, tf=512, tq=128, tk=512, L=128, lb=128)
        y_s, st_s = _layer(
            y_s, past + (jnp.arange(b_s * t_s) % t_s), None, "sample",
            (cache_k[l], cache_v[l], cache_kidx[l], state_mlstm_C[l], state_mlstm_n[l],
             state_mlstm_m[l], state_conv[l]), params, splits,
            tm_proj=b_s * t_s, tm_ffn=b_s * t_s, tf=512, tq=LANES, tk=384, L=128, lb=t_s)
        new_p.append(st_p)
        new_s.append(st_s)

    stack = lambda states, i: jnp.stack([s[i] for s in states])
    return (y_p, y_s,
            *[stack(new_p, i) for i in range(7)],
            *[stack(new_s, i) for i in range(7)])
```

```python
import functools

import jax
import jax.numpy as jnp
import numpy as np
from jax import lax
from jax.experimental import pallas as pl
from jax.experimental.pallas import tpu as pltpu

F32 = jnp.float32
BF16 = jnp.bfloat16
I32 = jnp.int32

EPS = 1e-6
ROPE_THETA = 500000.0
CHUNK = 64
CHUNK_LOG2 = 6
TOPK_MAX = 256
A_HEAD_DIM = 128
A_KV_HEADS = 2
IDX_HEADS = 8
IDX_DIM = 64
M_HEAD_DIM = 128
CONV_W = 4

LANES = 128
SUBLANES = 8
VMEM_LIMIT = 56 * 1024 * 1024
NEG_BIG = -1e30
LOG2_E = 1.4426950408889634
INT_MIN = -(2 ** 31)
BIG_J = 2 ** 30


def _rms(x, g):
    r = lax.rsqrt(jnp.mean(x * x, axis=-1, keepdims=True) + EPS)
    return x * r * g


def _rope(t, c, sa, sb, half):
    return t * c + pltpu.roll(t, LANES - half, 1) * sa + pltpu.roll(t, half, 1) * sb


def _sigmoid(x):
    return 1.0 / (1.0 + jnp.exp(-x))


def _dot(a, b):
    return jnp.dot(a, b, preferred_element_type=F32)


def _dot_t(a, b):
    return lax.dot_general(a, b, (((1,), (1,)), ((), ())), preferred_element_type=F32)


def _proj_kernel(x_ref, g1_ref, wa_ref, wb_ref, qg_ref, kg_ref, ra_ref, ri_ref,
                 q_out, k_out, v_out, ki_out, kb_out, vb_out, kiab_out, qi_out,
                 wi_out, ig_out, fg_out, u_out, vm_out, om_out, *, n_q_heads):
    xn = _rms(x_ref[...], g1_ref[...]).astype(BF16)
    ra = ra_ref[...]
    ca, saa, sba = ra[:, :LANES], ra[:, LANES:2 * LANES], ra[:, 2 * LANES:]
    ri = ri_ref[...]
    ci, sai, sbi = ri[:, :LANES], ri[:, LANES:2 * LANES], ri[:, 2 * LANES:]
    qg = qg_ref[...]
    kg = kg_ref[...]
    aw = n_q_heads * A_HEAD_DIM
    kvw = A_KV_HEADS * A_HEAD_DIM
    q_scale = A_HEAD_DIM ** -0.5 * LOG2_E
    i_scale = IDX_DIM ** -0.5

    for c in range(0, aw, 512):
        z = _dot(xn, wa_ref[:, c:c + 512])
        for j in range(0, 512, LANES):
            qh = _rope(_rms(z[:, j:j + LANES], qg), ca, saa, sba, A_HEAD_DIM // 8) * q_scale
            q_out[:, c + j:c + j + LANES] = qh.astype(BF16)
    off = aw
    z = _dot(xn, wa_ref[:, off:off + 2 * kvw])
    for j in range(0, kvw, LANES):
        kh = _rope(_rms(z[:, j:j + LANES], kg), ca, saa, sba, A_HEAD_DIM // 8)
        k_out[:, j:j + LANES] = kh
        kb_out[:, j:j + LANES] = kh.astype(BF16)
    vv = z[:, kvw:]
    v_out[...] = vv
    vb_out[...] = vv.astype(BF16)
    off += 2 * kvw
    iw = IDX_HEADS * IDX_DIM
    z = _dot(xn, wa_ref[:, off:off + iw])
    for j in range(0, iw, LANES):
        qi = _rope(z[:, j:j + LANES], ci, sai, sbi, IDX_DIM // 8) * i_scale
        qi_out[:, j:j + LANES] = qi.astype(BF16)
    off += iw
    z = _dot(xn, wa_ref[:, off:off + 4 * LANES])
    ki2 = _rope(z[:, :LANES], ci, sai, sbi, IDX_DIM // 8)
    ki_out[...] = ki2[:, :IDX_DIM]
    lane = lax.broadcasted_iota(I32, (1, LANES), 1)
    lo = lane < IDX_DIM
    kiab_out[:, :LANES] = jnp.where(lo, ki2, 0.0).astype(BF16)
    kiab_out[:, LANES:] = jnp.where(lo, 0.0, ki2).astype(BF16)
    wi_out[...] = z[:, LANES:2 * LANES]
    ig_out[...] = z[:, 2 * LANES:3 * LANES]
    fg_out[...] = z[:, 3 * LANES:]
    mw = u_out.shape[1]
    for c in range(0, mw, 512):
        u_out[:, c:c + 512] = _dot(xn, wb_ref[:, c:c + 512])
        vm_out[:, c:c + 512] = _dot(xn, wb_ref[:, mw + c:mw + c + 512]).astype(BF16)
        om_out[:, c:c + 512] = _dot(xn, wb_ref[:, 2 * mw + c:2 * mw + c + 512])


def _proj(x, g1, wa, wb, qg, kg, ra, ri, *, tm, n_q_heads, m_width):
    m, d = x.shape
    n_tab = ra.shape[0] // tm
    aw = n_q_heads * A_HEAD_DIM
    kvw = A_KV_HEADS * A_HEAD_DIM
    iw = IDX_HEADS * IDX_DIM
    row = lambda w: pl.BlockSpec((tm, w), lambda i: (i, 0))
    const = lambda a: pl.BlockSpec(a.shape, lambda i: (0, 0), pipeline_mode=pl.Buffered(1))
    tab = pl.BlockSpec((tm, 3 * LANES), lambda i: (i % n_tab, 0))
    outs = [
        (aw, BF16), (kvw, F32), (kvw, F32), (IDX_DIM, F32), (kvw, BF16), (kvw, BF16),
        (2 * LANES, BF16), (iw, BF16), (LANES, F32), (LANES, F32), (LANES, F32),
        (m_width, F32), (m_width, BF16), (m_width, F32),
    ]
    return pl.pallas_call(
        functools.partial(_proj_kernel, n_q_heads=n_q_heads),
        grid=(m // tm,),
        in_specs=[row(d), const(g1), const(wa), const(wb), const(qg), const(kg), tab, tab],
        out_specs=[row(w) for w, _ in outs],
        out_shape=[jax.ShapeDtypeStruct((m, w), dt) for w, dt in outs],
        compiler_params=pltpu.CompilerParams(
            dimension_semantics=("arbitrary",), vmem_limit_bytes=VMEM_LIMIT),
        name="proj",
    )(x, g1, wa, wb, qg, kg, ra, ri)


def _pad_rows(x, n):
    if x.shape[0] == n:
        return x
    return jnp.concatenate([x, jnp.zeros((n - x.shape[0],) + x.shape[1:], x.dtype)], axis=0)


V_EXT_ROWS = A_HEAD_DIM + 16


def _attn_kernel(q_ref, qi_ref, wi_ref, k_ref, vt_ref, ki_ref, o_ref,
                 key_scr, qs_scr, qis_scr, m_scr, acc_scr, sa_scr, sb_scr, da_scr, db_scr,
                 *, tq, tqb, tk, causal, s_valid, n_sel, n_tiles, group):
    start = pl.program_id(1) * tqb
    nr = tk // SUBLANES
    n_acc = 8
    lane = lax.broadcasted_iota(I32, (1, tq), 1)
    rowi = lax.broadcasted_iota(I32, (tk, 1), 0)
    row3 = (lax.broadcasted_iota(I32, (nr, SUBLANES, 1), 0) * SUBLANES
            + lax.broadcasted_iota(I32, (nr, SUBLANES, 1), 1))
    if causal:
        limit = (lax.shift_right_logical(start + lane, CHUNK_LOG2) + 1) * CHUNK
        nkt = jnp.minimum((start + tqb + tk - 1) // tk, n_tiles)
    else:
        limit = jnp.full((1, tq), s_valid, I32)
        nkt = n_tiles

    q = _pad_rows(q_ref[0], tq)
    for g in range(A_KV_HEADS):
        for r in range(group):
            h = g * group + r
            qs_scr[g, r * tq:(r + 1) * tq, :] = q[:, h * LANES:(h + 1) * LANES]
    qi = _pad_rows(qi_ref[0], tq)
    for j in range(IDX_HEADS // 2):
        qis_scr[j * tq:(j + 1) * tq, :] = qi[:, j * LANES:(j + 1) * LANES]
    w_t = jnp.transpose(_pad_rows(wi_ref[0], tq))[:IDX_HEADS] * (IDX_HEADS ** -0.5)

    d_bufs = (da_scr, db_scr)

    def stage_scores(tile, par):
        t_c = jnp.minimum(tile, nkt - 1)
        off = pl.multiple_of(t_c * tk, tk)
        kiab = ki_ref[0, pl.ds(off, tk), :]
        for half in range(IDX_HEADS // 4):
            qh = qis_scr[half * 2 * tq:(half + 1) * 2 * tq, :]
            for odd in range(2):
                d_bufs[par][2 * half + odd] = _dot_t(kiab[:, odd * LANES:(odd + 1) * LANES], qh)

    def finish_scores(tile, par):
        t_c = jnp.minimum(tile, nkt - 1)
        acc = jnp.zeros((tk, tq), F32)
        for half in range(IDX_HEADS // 4):
            for odd in range(2):
                d = d_bufs[par][2 * half + odd]
                for jj in range(2):
                    h = 2 * (2 * half + jj) + odd
                    acc = acc + w_t[h:h + 1, :] * jnp.maximum(d[:, jj * tq:(jj + 1) * tq], 0.0)
        bits = lax.bitcast_convert_type(acc, I32)
        key = bits ^ (lax.shift_right_arithmetic(bits, 31) & 0x7FFFFFFF)
        key = jnp.where(key == -1, 0, key)
        key_scr[t_c] = jnp.where(t_c * tk + rowi < limit, key, INT_MIN)

    stage_scores(0, 0)

    def score_pair(i, carry):
        stage_scores(2 * i + 1, 1)
        finish_scores(2 * i, 0)
        stage_scores(2 * i + 2, 0)
        finish_scores(2 * i + 1, 1)
        return carry

    lax.fori_loop(0, (nkt + 1) // 2, score_pair, 0)

    def count_tiles(pred):
        def body(kt, acc):
            hit = pred(key_scr[kt].reshape(nr, SUBLANES, tq), kt * tk + row3)
            ones = jnp.where(hit, 1.0, 0.0).reshape(nr // n_acc, n_acc, SUBLANES, tq)
            return acc + jnp.sum(ones, axis=0)
        acc = lax.fori_loop(0, nkt, body, jnp.zeros((n_acc, SUBLANES, tq), F32))
        return jnp.sum(jnp.sum(acc, axis=0), axis=0, keepdims=True)

    def bit_step(i, carry):
        t_cur, c_cur = carry
        cand = t_cur + lax.shift_left(jnp.int32(1), 31 - i)
        cnt = count_tiles(lambda key, col: key >= cand[None])
        ok = cnt >= n_sel
        return jnp.where(ok, cand, t_cur), jnp.where(ok, cnt, c_cur)

    t_fin, c_fin = lax.fori_loop(
        0, 32, bit_step,
        (jnp.full((1, tq), INT_MIN, I32), jnp.full((1, tq), float(n_sel), F32)))
    is_min = t_fin == INT_MIN
    needs = jnp.logical_and(jnp.logical_and(jnp.logical_not(is_min), c_fin > n_sel), lane < tqb)
    j_fast = jnp.where(is_min, 0, BIG_J)

    def tie_search():
        rem = n_sel - count_tiles(lambda key, col: key > t_fin[None])

        def j_step(i, j_cur):
            cand = j_cur + lax.shift_left(jnp.int32(1), 14 - i)
            f = count_tiles(lambda key, col: jnp.logical_and(key == t_fin[None], col < cand[None]))
            return jnp.where(f <= rem, cand, j_cur)

        j_slow = lax.fori_loop(0, 15, j_step, jnp.zeros((1, tq), I32))
        return jnp.where(needs, j_slow, j_fast)

    j_fin = lax.cond(jnp.max(jnp.where(needs, 1.0, 0.0)) > 0.0, tie_search, lambda: j_fast)

    m_scr[...] = jnp.full(m_scr.shape, NEG_BIG, F32)
    acc_scr[...] = jnp.zeros(acc_scr.shape, F32)

    s_bufs = (sa_scr, sb_scr)

    def stage_logits(tile, par):
        t_c = jnp.minimum(tile, nkt - 1)
        off = pl.multiple_of(t_c * tk, tk)
        for g in range(A_KV_HEADS):
            s_bufs[par][g] = _dot_t(k_ref[0, pl.ds(off, tk), g * LANES:(g + 1) * LANES], qs_scr[g])

    def attend_tile(tile, par):
        t_c = jnp.minimum(tile, nkt - 1)
        key = key_scr[t_c]
        sel = jnp.logical_or(key > t_fin, jnp.logical_and(key == t_fin, t_c * tk + rowi < j_fin))
        sel = jnp.logical_and(sel, tile < nkt)
        bias = jnp.where(sel, 0.0, -jnp.inf)
        bias = jnp.concatenate([bias] * group, axis=1)
        for g in range(A_KV_HEADS):
            s = s_bufs[par][g] + bias
            m_old = m_scr[g]
            m_new = jnp.maximum(m_old, jnp.max(s, axis=0, keepdims=True))
            alpha = jnp.exp2(m_old - m_new)
            p = jnp.exp2(s - m_new).astype(BF16)
            acc_scr[g] = alpha * acc_scr[g] + _dot(vt_ref[0, t_c, g], p)
            m_scr[g] = m_new

    stage_logits(0, 0)

    def attend_pair(i, carry):
        stage_logits(2 * i + 1, 1)
        attend_tile(2 * i, 0)
        stage_logits(2 * i + 2, 0)
        attend_tile(2 * i + 1, 1)
        return carry

    lax.fori_loop(0, (nkt + 1) // 2, attend_pair, 0)
    for g in range(A_KV_HEADS):
        acc = acc_scr[g]
        out_t = acc[:A_HEAD_DIM] / acc[A_HEAD_DIM:A_HEAD_DIM + 1]
        for r in range(group):
            h = g * group + r
            o = jnp.transpose(out_t[:, r * tq:(r + 1) * tq])
            o_ref[0, :, h * LANES:(h + 1) * LANES] = o[:tqb].astype(BF16)


def _attn(q, qi, wi, kb, vb, kiab, *, tq, tk, causal, s_valid, n_sel):
    b, t, aw = q.shape
    tqb = min(tq, t)
    s_pad = kb.shape[1]
    n_tiles = s_pad // tk
    n_heads = aw // A_HEAD_DIM
    group = n_heads // A_KV_HEADS
    vt = jnp.transpose(vb.reshape(b, s_pad, A_KV_HEADS, A_HEAD_DIM), (0, 2, 3, 1))
    vt = jnp.concatenate([vt, jnp.ones((b, A_KV_HEADS, V_EXT_ROWS - A_HEAD_DIM, s_pad), BF16)], axis=2)
    vt = jnp.transpose(vt.reshape(b, A_KV_HEADS, V_EXT_ROWS, n_tiles, tk), (0, 3, 1, 2, 4))
    qspec = lambda w: pl.BlockSpec((1, tqb, w), lambda bi, qi_: (bi, qi_, 0))
    kspec = lambda w: pl.BlockSpec((1, s_pad, w), lambda bi, qi_: (bi, 0, 0))
    kern = functools.partial(_attn_kernel, tq=tq, tqb=tqb, tk=tk, causal=causal, s_valid=s_valid,
                             n_sel=n_sel, n_tiles=n_tiles, group=group)
    return pl.pallas_call(
        kern,
        grid=(b, t // tqb),
        in_specs=[qspec(aw), qspec(qi.shape[2]), qspec(LANES),
                  kspec(kb.shape[2]),
                  pl.BlockSpec((1,) + vt.shape[1:], lambda bi, qi_: (bi, 0, 0, 0, 0)),
                  kspec(kiab.shape[2])],
        out_specs=qspec(aw),
        out_shape=jax.ShapeDtypeStruct((b, t, aw), BF16),
        scratch_shapes=[
            pltpu.VMEM((n_tiles, tk, tq), I32),
            pltpu.VMEM((A_KV_HEADS, group * tq, LANES), BF16),
            pltpu.VMEM((IDX_HEADS // 2 * tq, LANES), BF16),
            pltpu.VMEM((A_KV_HEADS, 1, group * tq), F32),
            pltpu.VMEM((A_KV_HEADS, V_EXT_ROWS, group * tq), F32),
            pltpu.VMEM((A_KV_HEADS, tk, group * tq), F32),
            pltpu.VMEM((A_KV_HEADS, tk, group * tq), F32),
            pltpu.VMEM((IDX_HEADS // 2, tk, 2 * tq), F32),
            pltpu.VMEM((IDX_HEADS // 2, tk, 2 * tq), F32),
        ],
        compiler_params=pltpu.CompilerParams(
            dimension_semantics=("arbitrary", "arbitrary"), vmem_limit_bytes=VMEM_LIMIT),
        name="attn",
    )(q, qi, wi, kb, vt, kiab)


def _mlstm_kernel(u_ref, vm_ref, om_ref, ig_ref, fg_ref, c0_ref, n0_ref, m0_ref, cv0_ref,
                  cw_ref, cb_ref, wq_ref, wk_ref, wkt_ref, bg_ref, hg_ref, tri_ref,
                  mo_ref, c_out, n_out, m_out, cv_out,
                  c_scr, n_scr, m_scr, prev_scr, ubuf,
                  *, L, lb, n_heads):
    t = pl.program_id(1)
    nt = pl.num_programs(1)
    hd = M_HEAD_DIM
    pad = SUBLANES

    @pl.when(t == 0)
    def _():
        c_scr[...] = c0_ref[0]
        n_scr[...] = n0_ref[0]
        m_scr[...] = m0_ref[0]
        prev_scr[...] = cv0_ref[0]

    ubuf[0:pad, :] = prev_scr[...]
    ubuf[pad:pad + L, :] = _pad_rows(u_ref[...], L)
    uc = cb_ref[...]
    for i in range(CONV_W):
        uc = uc + ubuf[pad - (CONV_W - 1) + i:pad - (CONV_W - 1) + i + L, :] * cw_ref[i:i + 1, :]
    prev_scr[...] = ubuf[lb:lb + pad, :]
    uh = (uc * _sigmoid(uc)).astype(BF16)

    vm = _pad_rows(vm_ref[...], L)
    om = _pad_rows(om_ref[...], L)
    rowv = lax.broadcasted_iota(I32, (L, LANES), 0) < lb
    li = jnp.where(rowv, _pad_rows(ig_ref[...], L) + bg_ref[0:1, :], NEG_BIG)
    xf = _pad_rows(fg_ref[...], L) + bg_ref[1:2, :]
    lf = jnp.where(rowv, jnp.minimum(xf, 0.0) - jnp.log(1.0 + jnp.exp(-jnp.abs(xf))), 0.0)
    b = jnp.dot(tri_ref[...], lf, precision=lax.Precision.HIGHEST, preferred_element_type=F32)
    c = li - b
    c_t = c.T
    m_prev = m_scr[...]
    inter = b + m_prev
    b_last = b[L - 1:L, :]
    dec = b_last - b + li
    m_new = jnp.maximum(b_last + m_prev, jnp.max(dec, axis=0, keepdims=True))
    wts = jnp.exp(dec - m_new)
    wts_t = wts.T
    sc = jnp.exp(b_last + m_prev - m_new)
    causal = lax.broadcasted_iota(I32, (L, L), 0) >= lax.broadcasted_iota(I32, (L, L), 1)
    k_scale = hd ** -0.5

    for h in range(n_heads):
        sl = slice(h * hd, (h + 1) * hd)
        uh_h = uh[:, sl]
        q_h = _dot(uh_h, wq_ref[h])
        k_h = _dot(uh_h, wk_ref[h]) * k_scale
        kt_h = _dot_t(wkt_ref[h], uh_h) * k_scale
        v_h = vm[:, sl]
        dmat = jnp.where(causal, b[:, h:h + 1] + c_t[h:h + 1, :], -jnp.inf)
        inter_h = inter[:, h:h + 1]
        m_t = jnp.maximum(inter_h, jnp.max(dmat, axis=-1, keepdims=True))
        q_b = q_h.astype(BF16)
        sw = jnp.exp(dmat - m_t) * _dot_t(q_b, k_h.astype(BF16))
        sp = jnp.exp(inter_h - m_t)
        c_h = c_scr[h]
        n_h = n_scr[h:h + 1, :]
        num = sp * _dot(q_b, c_h.astype(BF16)) + _dot(sw.astype(BF16), v_h)
        den = sp * jnp.sum(q_h * n_h, axis=-1, keepdims=True) + jnp.sum(sw, axis=-1, keepdims=True)
        hh = num / jnp.maximum(jnp.abs(den), jnp.exp(-m_t))
        hn = _rms(hh, hg_ref[:, sl])
        out = _sigmoid(om[:, sl]) * hn
        mo_ref[:, sl] = out[:lb].astype(BF16)
        sc_h = sc[:, h:h + 1]
        c_scr[h] = sc_h * c_h + _dot((kt_h * wts_t[h:h + 1, :]).astype(BF16), v_h)
        n_scr[h:h + 1, :] = sc_h * n_h + jnp.sum(k_h * wts[:, h:h + 1], axis=0, keepdims=True)
    m_scr[...] = m_new

    @pl.when(t == nt - 1)
    def _():
        c_out[0] = c_scr[...]
        n_out[0] = n_scr[...]
        m_out[0] = m_scr[...]
        cv_out[0] = prev_scr[...]


def _mlstm(u, vm, om, ig, fg, c0, n0, m0, cv0, cw, cb, wq, wk, wkt, bg, hg, tri, *, b, L, lb):
    m, mw = u.shape
    nt = m // (b * lb)
    n_heads = mw // M_HEAD_DIM
    row = lambda w: pl.BlockSpec((lb, w), lambda bi, ti: (bi * nt + ti, 0))
    const = lambda a: pl.BlockSpec(a.shape, lambda bi, ti: (0,) * a.ndim)
    perb = lambda a: pl.BlockSpec((1,) + a.shape[1:], lambda bi, ti: (bi,) + (0,) * (a.ndim - 1))
    out_shapes = [
        jax.ShapeDtypeStruct((m, mw), BF16),
        jax.ShapeDtypeStruct(c0.shape, F32),
        jax.ShapeDtypeStruct(n0.shape, F32),
        jax.ShapeDtypeStruct(m0.shape, F32),
        jax.ShapeDtypeStruct(cv0.shape, F32),
    ]
    return pl.pallas_call(
        functools.partial(_mlstm_kernel, L=L, lb=lb, n_heads=n_heads),
        grid=(b, nt),
        in_specs=[row(mw), row(mw), row(mw), row(LANES), row(LANES),
                  perb(c0), perb(n0), perb(m0), perb(cv0),
                  const(cw), const(cb), const(wq), const(wk), const(wkt), const(bg), const(hg), const(tri)],
        out_specs=[row(mw), perb(c0), perb(n0), perb(m0), perb(cv0)],
        out_shape=out_shapes,
        scratch_shapes=[
            pltpu.VMEM(c0.shape[1:], F32),
            pltpu.VMEM(n0.shape[1:], F32),
            pltpu.VMEM(m0.shape[1:], F32),
            pltpu.VMEM((SUBLANES, mw), F32),
            pltpu.VMEM((L + SUBLANES, mw), F32),
        ],
        compiler_params=pltpu.CompilerParams(
            dimension_semantics=("arbitrary", "arbitrary"), vmem_limit_bytes=VMEM_LIMIT),
        name="mlstm",
    )(u, vm, om, ig, fg, c0, n0, m0, cv0, cw, cb, wq, wk, wkt, bg, hg, tri)


def _ffn_kernel(x_ref, a_ref, mo_ref, wo_ref, g2_ref, wu_ref, wd_ref, y_ref, hn_scr):
    j = pl.program_id(1)

    @pl.when(j == 0)
    def _():
        aw = a_ref.shape[1]
        h = x_ref[...] + _dot(a_ref[...], wo_ref[:aw, :]) + _dot(mo_ref[...], wo_ref[aw:, :])
        y_ref[...] = h
        hn_scr[...] = _rms(h, g2_ref[...]).astype(BF16)

    f = jnp.maximum(_dot(hn_scr[...], wu_ref[...]), 0.0)
    y_ref[...] += _dot((f * f).astype(BF16), wd_ref[...])


def _ffn(x, a, mo, wo, g2, wu, wd, *, tm, tf):
    m, d = x.shape
    dff = wu.shape[1]
    row = lambda w: pl.BlockSpec((tm, w), lambda i, j: (i, 0))
    const = lambda arr: pl.BlockSpec(arr.shape, lambda i, j: (0, 0), pipeline_mode=pl.Buffered(1))
    return pl.pallas_call(
        _ffn_kernel,
        grid=(m // tm, dff // tf),
        in_specs=[row(d), row(a.shape[1]), row(mo.shape[1]), const(wo), const(g2),
                  pl.BlockSpec((d, tf), lambda i, j: (0, j)),
                  pl.BlockSpec((tf, d), lambda i, j: (j, 0))],
        out_specs=row(d),
        out_shape=jax.ShapeDtypeStruct((m, d), F32),
        scratch_shapes=[pltpu.VMEM((tm, d), BF16)],
        compiler_params=pltpu.CompilerParams(
            dimension_semantics=("arbitrary", "arbitrary"), vmem_limit_bytes=VMEM_LIMIT),
        name="ffn",
    )(x, a, mo, wo, g2, wu, wd)


def _rope_tables(pos, rot, width):
    half = rot // 2
    inv_freq = ROPE_THETA ** (-jnp.arange(half, dtype=F32) / half)
    ang = pos.astype(F32)[:, None] * inv_freq[None, :]
    cos, sin = jnp.cos(ang), jnp.sin(ang)
    n = pos.shape[0]
    rest1 = jnp.ones((n, width - rot), F32)
    rest0 = jnp.zeros((n, width - rot), F32)
    z = jnp.zeros_like(sin)
    c = jnp.concatenate([cos, cos, rest1], axis=1)
    sa = jnp.concatenate([-sin, z, rest0], axis=1)
    sb = jnp.concatenate([z, sin, rest0], axis=1)
    rep = LANES // width
    return jnp.concatenate([jnp.tile(c, (1, rep)), jnp.tile(sa, (1, rep)), jnp.tile(sb, (1, rep))], axis=1)


def _pad_cols(w, n):
    return jnp.pad(w, ((0, 0), (0, n - w.shape[1])))


def _layer(x, pos_rows, b, mode, state, params, splits, *, tm_proj, tm_ffn, tf, tq, tk, L, lb):
    (norm1_g, w_in, q_norm_g, k_norm_g, conv_w, conv_b, wq_m, wk_m, b_igate, b_fgate,
     hnorm_g, w_out, norm2_g, w_up, w_down) = params
    bsz, t, d = x.shape
    m = bsz * t
    x2 = x.reshape(m, d)
    offs = np.cumsum((0,) + tuple(splits))
    col = lambda i: w_in[:, int(offs[i]):int(offs[i + 1])]
    aw, mw = splits[0], splits[6]
    n_q_heads = aw // A_HEAD_DIM
    n_m_heads = mw // M_HEAD_DIM
    wa = jnp.concatenate(
        [col(0), col(1), col(2), col(3), col(4), col(4),
         _pad_cols(col(5), LANES), _pad_cols(col(9), LANES), _pad_cols(col(10), LANES)], axis=1).astype(BF16)
    wb = jnp.concatenate([col(6), col(7), col(8)], axis=1).astype(BF16)
    ra = _rope_tables(pos_rows, A_HEAD_DIM // 4, A_HEAD_DIM)
    ri = _rope_tables(pos_rows, IDX_DIM // 4, IDX_DIM)

    (q, k, v, ki, kb, vb, kiab, qi, wi, ig, fg, u, vm, om) = _proj(
        x2, norm1_g[None], wa, wb, q_norm_g[None], k_norm_g[None], ra, ri,
        tm=tm_proj, n_q_heads=n_q_heads, m_width=mw)

    kvw = A_KV_HEADS * A_HEAD_DIM
    r3 = lambda a_: a_.reshape(bsz, t, a_.shape[1])
    if mode == "prompt":
        c0 = jnp.zeros((bsz, n_m_heads, M_HEAD_DIM, M_HEAD_DIM), F32)
        n0 = jnp.zeros((bsz, n_m_heads, M_HEAD_DIM), F32)
        m0 = jnp.zeros((bsz, 1, LANES), F32)
        cv0 = jnp.zeros((bsz, SUBLANES, mw), F32)
        n_sel = min(TOPK_MAX, t // 4)
        a = _attn(r3(q), r3(qi), r3(wi), r3(kb), r3(vb), r3(kiab),
                  tq=tq, tk=tk, causal=True, s_valid=t, n_sel=n_sel)
    else:
        ck, cv, ckidx, s_c, s_n, s_m, s_conv = state
        p = ck.shape[1]
        s_valid = p + t
        s_pad = -(-s_valid // tk) * tk
        padk = lambda a_: jnp.pad(a_, ((0, 0), (0, s_pad - s_valid), (0, 0)))
        k_all = padk(jnp.concatenate([ck.reshape(bsz, p, kvw).astype(BF16), r3(kb)], axis=1))
        v_all = padk(jnp.concatenate([cv.reshape(bsz, p, kvw).astype(BF16), r3(vb)], axis=1))
        cki = ckidx.astype(BF16)
        zki = jnp.zeros_like(cki)
        ki_c = jnp.concatenate([cki, zki, zki, cki], axis=2)
        ki_all = padk(jnp.concatenate([ki_c, r3(kiab)], axis=1))
        c0, n0 = s_c.astype(F32), s_n.astype(F32)
        m0 = jnp.pad(s_m.astype(F32), ((0, 0), (0, LANES - n_m_heads)))[:, None, :]
        cv0 = jnp.pad(s_conv.astype(F32), ((0, 0), (SUBLANES - (CONV_W - 1), 0), (0, 0)))
        n_sel = min(TOPK_MAX, s_valid // 4)
        a = _attn(r3(q), r3(qi), r3(wi), k_all, v_all, ki_all,
                  tq=tq, tk=tk, causal=False, s_valid=s_valid, n_sel=n_sel)

    bg = jnp.stack([_pad_cols(b_igate[None], LANES)[0], _pad_cols(b_fgate[None], LANES)[0]])
    tri = (np.arange(L)[:, None] >= np.arange(L)[None, :]).astype(np.float32)
    mo, c_new, n_new, m_new, cv_new = _mlstm(
        u, vm, om, ig, fg, c0, n0, m0, cv0,
        conv_w, conv_b[None], wq_m.astype(BF16), wk_m.astype(BF16),
        jnp.swapaxes(wk_m, 1, 2).astype(BF16), bg, hnorm_g[None], jnp.asarray(tri),
        b=bsz, L=L, lb=lb)

    y = _ffn(x2, a.reshape(m, aw), mo, w_out.astype(BF16), norm2_g[None],
             w_up.astype(BF16), w_down.astype(BF16), tm=tm_ffn, tf=tf)
    new_state = (
        k.reshape(bsz, t, A_KV_HEADS, A_HEAD_DIM), v.reshape(bsz, t, A_KV_HEADS, A_HEAD_DIM),
        ki.reshape(bsz, t, IDX_DIM), c_new, n_new, m_new[:, 0, :n_m_heads],
        cv_new[:, SUBLANES - (CONV_W - 1):, :])
    return y.reshape(bsz, t, d), new_state


def kernel(x_prompt, x_sample, cache_k, cache_v, cache_kidx, state_mlstm_C, state_mlstm_n, state_mlstm_m,
           state_conv, norm1_g, w_in, q_norm_g, k_norm_g, conv_w, conv_b, wq_m, wk_m, b_igate, b_fgate,
           hnorm_g, w_out, norm2_g, w_up, w_down):
    depth = w_in.shape[0]
    d = x_prompt.shape[-1]
    mix = w_out.shape[1]
    aw = mix // 2
    mw = mix - aw
    kvw = A_KV_HEADS * A_HEAD_DIM
    n_m_heads = mw // M_HEAD_DIM
    splits = (aw, kvw, kvw, IDX_HEADS * IDX_DIM, IDX_DIM, IDX_HEADS, mw, mw, mw, n_m_heads, n_m_heads)
    t_p = x_prompt.shape[1]
    b_s, t_s = x_sample.shape[0], x_sample.shape[1]
    past = cache_k.shape[2]

    y_p, y_s = x_prompt, x_sample
    new_p, new_s = [], []
    for l in range(depth):
        params = (norm1_g[l], w_in[l], q_norm_g[l], k_norm_g[l], conv_w[l], conv_b[l], wq_m[l], wk_m[l],
                  b_igate[l], b_fgate[l], hnorm_g[l], w_out[l], norm2_g[l], w_up[l], w_down[l])
        y_p, st_p = _layer(
            y_p, jnp.arange(t_p), None, "prompt", None, params, splits,
            tm_proj=256, tm_ffn=512, tf=1024, tq=128, tk=512, L=128, lb=128)
        y_s, st_s = _layer(
            y_s, past + (jnp.arange(b_s * t_s) % t_s), None, "sample",
            (cache_k[l], cache_v[l], cache_kidx[l], state_mlstm_C[l], state_mlstm_n[l],
             state_mlstm_m[l], state_conv[l]), params, splits,
            tm_proj=b_s * t_s, tm_ffn=b_s * t_s, tf=512, tq=LANES, tk=384, L=128, lb=t_s)
        new_p.append(st_p)
        new_s.append(st_s)

    stack = lambda states, i: jnp.stack([s[i] for s in states])
    return (y_p, y_s,
            *[stack(new_p, i) for i in range(7)],
            *[stack(new_s, i) for i in range(7)])
```

```python
import functools

import jax
import jax.numpy as jnp
import numpy as np
from jax import lax
from jax.experimental import pallas as pl
from jax.experimental.pallas import tpu as pltpu

F32 = jnp.float32
BF16 = jnp.bfloat16
I32 = jnp.int32

EPS = 1e-6
ROPE_THETA = 500000.0
CHUNK = 64
CHUNK_LOG2 = 6
TOPK_MAX = 256
A_HEAD_DIM = 128
A_KV_HEADS = 2
IDX_HEADS = 8
IDX_DIM = 64
M_HEAD_DIM = 128
CONV_W = 4

LANES = 128
SUBLANES = 8
VMEM_LIMIT = 56 * 1024 * 1024
NEG_BIG = -1e30
LOG2_E = 1.4426950408889634
INT_MIN = -(2 ** 31)
BIG_J = 2 ** 30
UNSETTLED = 1e9
SEARCH_EARLY_BITS = 26


def _rms(x, g):
    r = lax.rsqrt(jnp.mean(x * x, axis=-1, keepdims=True) + EPS)
    return x * r * g


def _rope(t, c, sa, sb, half):
    return t * c + pltpu.roll(t, LANES - half, 1) * sa + pltpu.roll(t, half, 1) * sb


def _sigmoid(x):
    return 1.0 / (1.0 + jnp.exp(-x))


def _dot(a, b):
    return jnp.dot(a, b, preferred_element_type=F32)


def _dot_t(a, b):
    return lax.dot_general(a, b, (((1,), (1,)), ((), ())), preferred_element_type=F32)


def _proj_kernel(x_ref, g1_ref, wa_ref, wb_ref, qg_ref, kg_ref, ra_ref, ri_ref,
                 q_out, k_out, v_out, ki_out, kb_out, vb_out, kiab_out, qi_out,
                 wi_out, ig_out, fg_out, u_out, vm_out, om_out, *, n_q_heads):
    xn = _rms(x_ref[...], g1_ref[...]).astype(BF16)
    ra = ra_ref[...]
    ca, saa, sba = ra[:, :LANES], ra[:, LANES:2 * LANES], ra[:, 2 * LANES:]
    ri = ri_ref[...]
    ci, sai, sbi = ri[:, :LANES], ri[:, LANES:2 * LANES], ri[:, 2 * LANES:]
    qg = qg_ref[...]
    kg = kg_ref[...]
    aw = n_q_heads * A_HEAD_DIM
    kvw = A_KV_HEADS * A_HEAD_DIM
    q_scale = A_HEAD_DIM ** -0.5 * LOG2_E
    i_scale = IDX_DIM ** -0.5

    for c in range(0, aw, 512):
        z = _dot(xn, wa_ref[:, c:c + 512])
        for j in range(0, 512, LANES):
            qh = _rope(_rms(z[:, j:j + LANES], qg), ca, saa, sba, A_HEAD_DIM // 8) * q_scale
            q_out[:, c + j:c + j + LANES] = qh.astype(BF16)
    off = aw
    z = _dot(xn, wa_ref[:, off:off + 2 * kvw])
    for j in range(0, kvw, LANES):
        kh = _rope(_rms(z[:, j:j + LANES], kg), ca, saa, sba, A_HEAD_DIM // 8)
        k_out[:, j:j + LANES] = kh
        kb_out[:, j:j + LANES] = kh.astype(BF16)
    vv = z[:, kvw:]
    v_out[...] = vv
    vb_out[...] = vv.astype(BF16)
    off += 2 * kvw
    iw = IDX_HEADS * IDX_DIM
    z = _dot(xn, wa_ref[:, off:off + iw])
    for j in range(0, iw, LANES):
        qi = _rope(z[:, j:j + LANES], ci, sai, sbi, IDX_DIM // 8) * i_scale
        qi_out[:, j:j + LANES] = qi.astype(BF16)
    off += iw
    z = _dot(xn, wa_ref[:, off:off + 4 * LANES])
    ki2 = _rope(z[:, :LANES], ci, sai, sbi, IDX_DIM // 8)
    ki_out[...] = ki2[:, :IDX_DIM]
    lane = lax.broadcasted_iota(I32, (1, LANES), 1)
    lo = lane < IDX_DIM
    kiab_out[:, :LANES] = jnp.where(lo, ki2, 0.0).astype(BF16)
    kiab_out[:, LANES:] = jnp.where(lo, 0.0, ki2).astype(BF16)
    wi_out[...] = z[:, LANES:2 * LANES]
    ig_out[...] = z[:, 2 * LANES:3 * LANES]
    fg_out[...] = z[:, 3 * LANES:]
    mw = u_out.shape[1]
    for c in range(0, mw, 512):
        u_out[:, c:c + 512] = _dot(xn, wb_ref[:, c:c + 512])
        vm_out[:, c:c + 512] = _dot(xn, wb_ref[:, mw + c:mw + c + 512]).astype(BF16)
        om_out[:, c:c + 512] = _dot(xn, wb_ref[:, 2 * mw + c:2 * mw + c + 512])


def _proj(x, g1, wa, wb, qg, kg, ra, ri, *, tm, n_q_heads, m_width):
    m, d = x.shape
    n_tab = ra.shape[0] // tm
    aw = n_q_heads * A_HEAD_DIM
    kvw = A_KV_HEADS * A_HEAD_DIM
    iw = IDX_HEADS * IDX_DIM
    row = lambda w: pl.BlockSpec((tm, w), lambda i: (i, 0))
    const = lambda a: pl.BlockSpec(a.shape, lambda i: (0, 0), pipeline_mode=pl.Buffered(1))
    tab = pl.BlockSpec((tm, 3 * LANES), lambda i: (i % n_tab, 0))
    outs = [
        (aw, BF16), (kvw, F32), (kvw, F32), (IDX_DIM, F32), (kvw, BF16), (kvw, BF16),
        (2 * LANES, BF16), (iw, BF16), (LANES, F32), (LANES, F32), (LANES, F32),
        (m_width, F32), (m_width, BF16), (m_width, F32),
    ]
    return pl.pallas_call(
        functools.partial(_proj_kernel, n_q_heads=n_q_heads),
        grid=(m // tm,),
        in_specs=[row(d), const(g1), const(wa), const(wb), const(qg), const(kg), tab, tab],
        out_specs=[row(w) for w, _ in outs],
        out_shape=[jax.ShapeDtypeStruct((m, w), dt) for w, dt in outs],
        compiler_params=pltpu.CompilerParams(
            dimension_semantics=("arbitrary",), vmem_limit_bytes=VMEM_LIMIT),
        name="proj",
    )(x, g1, wa, wb, qg, kg, ra, ri)


def _pad_rows(x, n):
    if x.shape[0] == n:
        return x
    return jnp.concatenate([x, jnp.zeros((n - x.shape[0],) + x.shape[1:], x.dtype)], axis=0)


V_EXT_ROWS = A_HEAD_DIM + 16


def _attn_kernel(q_ref, qi_ref, wi_ref, k_ref, vt_ref, ki_ref, o_ref,
                 key_scr, tie_scr, qs_scr, qis_scr, m_scr, acc_scr, sa_scr, sb_scr, da_scr, db_scr,
                 *, tq, tqb, tk, causal, s_valid, n_sel, n_tiles, group):
    start = pl.program_id(1) * tqb
    nr = tk // SUBLANES
    n_acc = 8
    lane = lax.broadcasted_iota(I32, (1, tq), 1)
    rowi = lax.broadcasted_iota(I32, (tk, 1), 0)
    if causal:
        limit = (lax.shift_right_logical(start + lane, CHUNK_LOG2) + 1) * CHUNK
        nkt = jnp.minimum((start + tqb + tk - 1) // tk, n_tiles)
    else:
        limit = jnp.full((1, tq), s_valid, I32)
        nkt = n_tiles

    q = _pad_rows(q_ref[0], tq)
    for g in range(A_KV_HEADS):
        for r in range(group):
            h = g * group + r
            qs_scr[g, r * tq:(r + 1) * tq, :] = q[:, h * LANES:(h + 1) * LANES]
    qi = _pad_rows(qi_ref[0], tq)
    for j in range(IDX_HEADS // 2):
        qis_scr[j * tq:(j + 1) * tq, :] = qi[:, j * LANES:(j + 1) * LANES]
    w_t = jnp.transpose(_pad_rows(wi_ref[0], tq))[:IDX_HEADS] * (IDX_HEADS ** -0.5)

    d_bufs = (da_scr, db_scr)

    def stage_scores(tile, par):
        t_c = jnp.minimum(tile, nkt - 1)
        off = pl.multiple_of(t_c * tk, tk)
        kiab = ki_ref[0, pl.ds(off, tk), :]
        for half in range(IDX_HEADS // 4):
            qh = qis_scr[half * 2 * tq:(half + 1) * 2 * tq, :]
            for odd in range(2):
                d_bufs[par][2 * half + odd] = _dot_t(kiab[:, odd * LANES:(odd + 1) * LANES], qh)

    def finish_scores(tile, par):
        t_c = jnp.minimum(tile, nkt - 1)
        acc = jnp.zeros((tk, tq), F32)
        for half in range(IDX_HEADS // 4):
            for odd in range(2):
                d = d_bufs[par][2 * half + odd]
                for jj in range(2):
                    h = 2 * (2 * half + jj) + odd
                    acc = acc + w_t[h:h + 1, :] * jnp.maximum(d[:, jj * tq:(jj + 1) * tq], 0.0)
        bits = lax.bitcast_convert_type(acc, I32)
        key = bits ^ (lax.shift_right_arithmetic(bits, 31) & 0x7FFFFFFF)
        key = jnp.where(key == -1, 0, key)
        key_scr[t_c] = jnp.where(t_c * tk + rowi < limit, key, INT_MIN)

    stage_scores(0, 0)

    def score_pair(i, carry):
        stage_scores(2 * i + 1, 1)
        finish_scores(2 * i, 0)
        stage_scores(2 * i + 2, 0)
        finish_scores(2 * i + 1, 1)
        return carry

    lax.fori_loop(0, (nkt + 1) // 2, score_pair, 0)

    def count_ref(ref, pred):
        def body(kt, acc):
            hit = pred(ref[kt].reshape(nr, SUBLANES, tq))
            ones = jnp.where(hit, 1.0, 0.0).reshape(nr // n_acc, n_acc, SUBLANES, tq)
            return acc + jnp.sum(ones, axis=0)
        acc = lax.fori_loop(0, nkt, body, jnp.zeros((n_acc, SUBLANES, tq), F32))
        return jnp.sum(jnp.sum(acc, axis=0), axis=0, keepdims=True)

    def bit_step(i, carry):
        t_cur, c_cur = carry
        cand = t_cur + lax.shift_left(jnp.int32(1), 31 - i)
        cnt = count_ref(key_scr, lambda key: key >= cand[None])
        ok = cnt >= n_sel
        return jnp.where(ok, cand, t_cur), jnp.where(ok, cnt, c_cur)

    def any_lane(mask):
        return jnp.max(jnp.where(mask, 1.0, 0.0)) > 0.0

    valid = lane < tqb
    n_self = float(n_sel)

    def settle(t_cur, c_cur):
        pend = jnp.logical_and(valid, c_cur != n_self)
        return lax.cond(any_lane(pend), lambda: count_ref(key_scr, lambda key: key > t_cur[None]),
                        lambda: jnp.zeros((1, tq), F32))

    t_e, c_e = lax.fori_loop(
        0, SEARCH_EARLY_BITS, bit_step,
        (jnp.full((1, tq), INT_MIN, I32), jnp.where(limit < n_sel, n_self, UNSETTLED)))
    g_e = settle(t_e, c_e)
    unresolved = jnp.logical_and(jnp.logical_and(valid, c_e != n_self), g_e >= n_self)

    def finish_search():
        t_l, c_l = lax.fori_loop(SEARCH_EARLY_BITS, 32, bit_step, (t_e, c_e))
        return t_l, c_l, settle(t_l, c_l)

    t_fin, c_fin, c_gt = lax.cond(any_lane(unresolved), finish_search, lambda: (t_e, c_e, g_e))
    is_min = t_fin == INT_MIN
    needs = jnp.logical_and(jnp.logical_and(jnp.logical_not(is_min), c_fin > n_sel), valid)
    j_fast = jnp.where(is_min, 0, BIG_J)

    def tie_search():
        rem = n_self - c_gt

        def mark(kt, carry):
            tie_scr[kt] = jnp.where(key_scr[kt] == t_fin, kt * tk + rowi, BIG_J)
            return carry

        lax.fori_loop(0, nkt, mark, 0)
        j_bits = (n_tiles * tk).bit_length()

        def j_step(i, j_cur):
            cand = j_cur + lax.shift_left(jnp.int32(1), j_bits - 1 - i)
            f = count_ref(tie_scr, lambda col: col < cand[None])
            return jnp.where(f <= rem, cand, j_cur)

        j_slow = lax.fori_loop(0, j_bits, j_step, jnp.zeros((1, tq), I32))
        return jnp.where(needs, j_slow, j_fast)

    j_fin = lax.cond(any_lane(needs), tie_search, lambda: j_fast)

    m_scr[...] = jnp.full(m_scr.shape, NEG_BIG, F32)
    acc_scr[...] = jnp.zeros(acc_scr.shape, F32)

    s_bufs = (sa_scr, sb_scr)

    def stage_logits(tile, par):
        t_c = jnp.minimum(tile, nkt - 1)
        off = pl.multiple_of(t_c * tk, tk)
        for g in range(A_KV_HEADS):
            s_bufs[par][g] = _dot_t(k_ref[0, pl.ds(off, tk), g * LANES:(g + 1) * LANES], qs_scr[g])

    def attend_tile(tile, par):
        t_c = jnp.minimum(tile, nkt - 1)
        key = key_scr[t_c]
        sel = jnp.logical_or(key > t_fin, jnp.logical_and(key == t_fin, t_c * tk + rowi < j_fin))
        sel = jnp.logical_and(sel, tile < nkt)
        bias = jnp.where(sel, 0.0, -jnp.inf)
        bias = jnp.concatenate([bias] * group, axis=1)
        for g in range(A_KV_HEADS):
            s = s_bufs[par][g] + bias
            m_old = m_scr[g]
            m_new = jnp.maximum(m_old, jnp.max(s, axis=0, keepdims=True))
            alpha = jnp.exp2(m_old - m_new)
            p = jnp.exp2(s - m_new).astype(BF16)
            acc_scr[g] = alpha * acc_scr[g] + _dot(vt_ref[0, t_c, g], p)
            m_scr[g] = m_new

    stage_logits(0, 0)

    def attend_pair(i, carry):
        stage_logits(2 * i + 1, 1)
        attend_tile(2 * i, 0)
        stage_logits(2 * i + 2, 0)
        attend_tile(2 * i + 1, 1)
        return carry

    lax.fori_loop(0, (nkt + 1) // 2, attend_pair, 0)
    for g in range(A_KV_HEADS):
        acc = acc_scr[g]
        out_t = acc[:A_HEAD_DIM] / acc[A_HEAD_DIM:A_HEAD_DIM + 1]
        for r in range(group):
            h = g * group + r
            o = jnp.transpose(out_t[:, r * tq:(r + 1) * tq])
            o_ref[0, :, h * LANES:(h + 1) * LANES] = o[:tqb].astype(BF16)


def _attn(q, qi, wi, kb, vb, kiab, *, tq, tk, causal, s_valid, n_sel):
    b, t, aw = q.shape
    tqb = min(tq, t)
    s_pad = kb.shape[1]
    n_tiles = s_pad // tk
    n_heads = aw // A_HEAD_DIM
    group = n_heads // A_KV_HEADS
    vt = jnp.transpose(vb.reshape(b, s_pad, A_KV_HEADS, A_HEAD_DIM), (0, 2, 3, 1))
    vt = jnp.concatenate([vt, jnp.ones((b, A_KV_HEADS, V_EXT_ROWS - A_HEAD_DIM, s_pad), BF16)], axis=2)
    vt = jnp.transpose(vt.reshape(b, A_KV_HEADS, V_EXT_ROWS, n_tiles, tk), (0, 3, 1, 2, 4))
    qspec = lambda w: pl.BlockSpec((1, tqb, w), lambda bi, qi_: (bi, qi_, 0))
    kspec = lambda w: pl.BlockSpec((1, s_pad, w), lambda bi, qi_: (bi, 0, 0))
    kern = functools.partial(_attn_kernel, tq=tq, tqb=tqb, tk=tk, causal=causal, s_valid=s_valid,
                             n_sel=n_sel, n_tiles=n_tiles, group=group)
    return pl.pallas_call(
        kern,
        grid=(b, t // tqb),
        in_specs=[qspec(aw), qspec(qi.shape[2]), qspec(LANES),
                  kspec(kb.shape[2]),
                  pl.BlockSpec((1,) + vt.shape[1:], lambda bi, qi_: (bi, 0, 0, 0, 0)),
                  kspec(kiab.shape[2])],
        out_specs=qspec(aw),
        out_shape=jax.ShapeDtypeStruct((b, t, aw), BF16),
        scratch_shapes=[
            pltpu.VMEM((n_tiles, tk, tq), I32),
            pltpu.VMEM((n_tiles, tk, tq), I32),
            pltpu.VMEM((A_KV_HEADS, group * tq, LANES), BF16),
            pltpu.VMEM((IDX_HEADS // 2 * tq, LANES), BF16),
            pltpu.VMEM((A_KV_HEADS, 1, group * tq), F32),
            pltpu.VMEM((A_KV_HEADS, V_EXT_ROWS, group * tq), F32),
            pltpu.VMEM((A_KV_HEADS, tk, group * tq), F32),
            pltpu.VMEM((A_KV_HEADS, tk, group * tq), F32),
            pltpu.VMEM((IDX_HEADS // 2, tk, 2 * tq), F32),
            pltpu.VMEM((IDX_HEADS // 2, tk, 2 * tq), F32),
        ],
        compiler_params=pltpu.CompilerParams(
            dimension_semantics=("arbitrary", "arbitrary"), vmem_limit_bytes=VMEM_LIMIT),
        name="attn",
    )(q, qi, wi, kb, vt, kiab)


def _mlstm_kernel(u_ref, vm_ref, om_ref, ig_ref, fg_ref, c0_ref, n0_ref, m0_ref, cv0_ref,
                  cw_ref, cb_ref, wq_ref, wk_ref, wkt_ref, bg_ref, hg_ref, tri_ref,
                  mo_ref, c_out, n_out, m_out, cv_out,
                  c_scr, n_scr, m_scr, prev_scr, ubuf,
                  *, L, lb, n_heads):
    t = pl.program_id(1)
    nt = pl.num_programs(1)
    hd = M_HEAD_DIM
    pad = SUBLANES

    @pl.when(t == 0)
    def _():
        c_scr[...] = c0_ref[0]
        n_scr[...] = n0_ref[0]
        m_scr[...] = m0_ref[0]
        prev_scr[...] = cv0_ref[0]

    ubuf[0:pad, :] = prev_scr[...]
    ubuf[pad:pad + L, :] = _pad_rows(u_ref[...], L)
    uc = cb_ref[...]
    for i in range(CONV_W):
        uc = uc + ubuf[pad - (CONV_W - 1) + i:pad - (CONV_W - 1) + i + L, :] * cw_ref[i:i + 1, :]
    prev_scr[...] = ubuf[lb:lb + pad, :]
    uh = (uc * _sigmoid(uc)).astype(BF16)

    vm = _pad_rows(vm_ref[...], L)
    om = _pad_rows(om_ref[...], L)
    rowv = lax.broadcasted_iota(I32, (L, LANES), 0) < lb
    li = jnp.where(rowv, _pad_rows(ig_ref[...], L) + bg_ref[0:1, :], NEG_BIG)
    xf = _pad_rows(fg_ref[...], L) + bg_ref[1:2, :]
    lf = jnp.where(rowv, jnp.minimum(xf, 0.0) - jnp.log(1.0 + jnp.exp(-jnp.abs(xf))), 0.0)
    b = jnp.dot(tri_ref[...], lf, precision=lax.Precision.HIGHEST, preferred_element_type=F32)
    c = li - b
    c_t = c.T
    m_prev = m_scr[...]
    inter = b + m_prev
    b_last = b[L - 1:L, :]
    dec = b_last - b + li
    m_new = jnp.maximum(b_last + m_prev, jnp.max(dec, axis=0, keepdims=True))
    wts = jnp.exp(dec - m_new)
    wts_t = wts.T
    sc = jnp.exp(b_last + m_prev - m_new)
    causal = lax.broadcasted_iota(I32, (L, L), 0) >= lax.broadcasted_iota(I32, (L, L), 1)
    k_scale = hd ** -0.5

    for h in range(n_heads):
        sl = slice(h * hd, (h + 1) * hd)
        uh_h = uh[:, sl]
        q_h = _dot(uh_h, wq_ref[h])
        k_h = _dot(uh_h, wk_ref[h]) * k_scale
        kt_h = _dot_t(wkt_ref[h], uh_h) * k_scale
        v_h = vm[:, sl]
        dmat = jnp.where(causal, b[:, h:h + 1] + c_t[h:h + 1, :], -jnp.inf)
        inter_h = inter[:, h:h + 1]
        m_t = jnp.maximum(inter_h, jnp.max(dmat, axis=-1, keepdims=True))
        q_b = q_h.astype(BF16)
        sw = jnp.exp(dmat - m_t) * _dot_t(q_b, k_h.astype(BF16))
        sp = jnp.exp(inter_h - m_t)
        c_h = c_scr[h]
        n_h = n_scr[h:h + 1, :]
        num = sp * _dot(q_b, c_h.astype(BF16)) + _dot(sw.astype(BF16), v_h)
        den = sp * jnp.sum(q_h * n_h, axis=-1, keepdims=True) + jnp.sum(sw, axis=-1, keepdims=True)
        hh = num / jnp.maximum(jnp.abs(den), jnp.exp(-m_t))
        hn = _rms(hh, hg_ref[:, sl])
        out = _sigmoid(om[:, sl]) * hn
        mo_ref[:, sl] = out[:lb].astype(BF16)
        sc_h = sc[:, h:h + 1]
        c_scr[h] = sc_h * c_h + _dot((kt_h * wts_t[h:h + 1, :]).astype(BF16), v_h)
        n_scr[h:h + 1, :] = sc_h * n_h + jnp.sum(k_h * wts[:, h:h + 1], axis=0, keepdims=True)
    m_scr[...] = m_new

    @pl.when(t == nt - 1)
    def _():
        c_out[0] = c_scr[...]
        n_out[0] = n_scr[...]
        m_out[0] = m_scr[...]
        cv_out[0] = prev_scr[...]


def _mlstm(u, vm, om, ig, fg, c0, n0, m0, cv0, cw, cb, wq, wk, wkt, bg, hg, tri, *, b, L, lb):
    m, mw = u.shape
    nt = m // (b * lb)
    n_heads = mw // M_HEAD_DIM
    row = lambda w: pl.BlockSpec((lb, w), lambda bi, ti: (bi * nt + ti, 0))
    const = lambda a: pl.BlockSpec(a.shape, lambda bi, ti: (0,) * a.ndim)
    perb = lambda a: pl.BlockSpec((1,) + a.shape[1:], lambda bi, ti: (bi,) + (0,) * (a.ndim - 1))
    out_shapes = [
        jax.ShapeDtypeStruct((m, mw), BF16),
        jax.ShapeDtypeStruct(c0.shape, F32),
        jax.ShapeDtypeStruct(n0.shape, F32),
        jax.ShapeDtypeStruct(m0.shape, F32),
        jax.ShapeDtypeStruct(cv0.shape, F32),
    ]
    return pl.pallas_call(
        functools.partial(_mlstm_kernel, L=L, lb=lb, n_heads=n_heads),
        grid=(b, nt),
        in_specs=[row(mw), row(mw), row(mw), row(LANES), row(LANES),
                  perb(c0), perb(n0), perb(m0), perb(cv0),
                  const(cw), const(cb), const(wq), const(wk), const(wkt), const(bg), const(hg), const(tri)],
        out_specs=[row(mw), perb(c0), perb(n0), perb(m0), perb(cv0)],
        out_shape=out_shapes,
        scratch_shapes=[
            pltpu.VMEM(c0.shape[1:], F32),
            pltpu.VMEM(n0.shape[1:], F32),
            pltpu.VMEM(m0.shape[1:], F32),
            pltpu.VMEM((SUBLANES, mw), F32),
            pltpu.VMEM((L + SUBLANES, mw), F32),
        ],
        compiler_params=pltpu.CompilerParams(
            dimension_semantics=("arbitrary", "arbitrary"), vmem_limit_bytes=VMEM_LIMIT),
        name="mlstm",
    )(u, vm, om, ig, fg, c0, n0, m0, cv0, cw, cb, wq, wk, wkt, bg, hg, tri)


def _ffn_kernel(x_ref, a_ref, mo_ref, wo_ref, g2_ref, wu_ref, wd_ref, y_ref, hn_scr):
    j = pl.program_id(1)

    @pl.when(j == 0)
    def _():
        aw = a_ref.shape[1]
        h = x_ref[...] + _dot(a_ref[...], wo_ref[:aw, :]) + _dot(mo_ref[...], wo_ref[aw:, :])
        y_ref[...] = h
        hn_scr[...] = _rms(h, g2_ref[...]).astype(BF16)

    f = jnp.maximum(_dot(hn_scr[...], wu_ref[...]), 0.0)
    y_ref[...] += _dot((f * f).astype(BF16), wd_ref[...])


def _ffn(x, a, mo, wo, g2, wu, wd, *, tm, tf):
    m, d = x.shape
    dff = wu.shape[1]
    row = lambda w: pl.BlockSpec((tm, w), lambda i, j: (i, 0))
    const = lambda arr: pl.BlockSpec(arr.shape, lambda i, j: (0, 0), pipeline_mode=pl.Buffered(1))
    return pl.pallas_call(
        _ffn_kernel,
        grid=(m // tm, dff // tf),
        in_specs=[row(d), row(a.shape[1]), row(mo.shape[1]), const(wo), const(g2),
                  pl.BlockSpec((d, tf), lambda i, j: (0, j)),
                  pl.BlockSpec((tf, d), lambda i, j: (j, 0))],
        out_specs=row(d),
        out_shape=jax.ShapeDtypeStruct((m, d), F32),
        scratch_shapes=[pltpu.VMEM((tm, d), BF16)],
        compiler_params=pltpu.CompilerParams(
            dimension_semantics=("arbitrary", "arbitrary"), vmem_limit_bytes=VMEM_LIMIT),
        name="ffn",
    )(x, a, mo, wo, g2, wu, wd)


def _rope_tables(pos, rot, width):
    half = rot // 2
    inv_freq = ROPE_THETA ** (-jnp.arange(half, dtype=F32) / half)
    ang = pos.astype(F32)[:, None] * inv_freq[None, :]
    cos, sin = jnp.cos(ang), jnp.sin(ang)
    n = pos.shape[0]
    rest1 = jnp.ones((n, width - rot), F32)
    rest0 = jnp.zeros((n, width - rot), F32)
    z = jnp.zeros_like(sin)
    c = jnp.concatenate([cos, cos, rest1], axis=1)
    sa = jnp.concatenate([-sin, z, rest0], axis=1)
    sb = jnp.concatenate([z, sin, rest0], axis=1)
    rep = LANES // width
    return jnp.concatenate([jnp.tile(c, (1, rep)), jnp.tile(sa, (1, rep)), jnp.tile(sb, (1, rep))], axis=1)


def _pad_cols(w, n):
    return jnp.pad(w, ((0, 0), (0, n - w.shape[1])))


def _layer(x, pos_rows, b, mode, state, params, splits, *, tm_proj, tm_ffn, tf, tq, tk, L, lb):
    (norm1_g, w_in, q_norm_g, k_norm_g, conv_w, conv_b, wq_m, wk_m, b_igate, b_fgate,
     hnorm_g, w_out, norm2_g, w_up, w_down) = params
    bsz, t, d = x.shape
    m = bsz * t
    x2 = x.reshape(m, d)
    offs = np.cumsum((0,) + tuple(splits))
    col = lambda i: w_in[:, int(offs[i]):int(offs[i + 1])]
    aw, mw = splits[0], splits[6]
    n_q_heads = aw // A_HEAD_DIM
    n_m_heads = mw // M_HEAD_DIM
    wa = jnp.concatenate(
        [col(0), col(1), col(2), col(3), col(4), col(4),
         _pad_cols(col(5), LANES), _pad_cols(col(9), LANES), _pad_cols(col(10), LANES)], axis=1).astype(BF16)
    wb = jnp.concatenate([col(6), col(7), col(8)], axis=1).astype(BF16)
    ra = _rope_tables(pos_rows, A_HEAD_DIM // 4, A_HEAD_DIM)
    ri = _rope_tables(pos_rows, IDX_DIM // 4, IDX_DIM)

    (q, k, v, ki, kb, vb, kiab, qi, wi, ig, fg, u, vm, om) = _proj(
        x2, norm1_g[None], wa, wb, q_norm_g[None], k_norm_g[None], ra, ri,
        tm=tm_proj, n_q_heads=n_q_heads, m_width=mw)

    kvw = A_KV_HEADS * A_HEAD_DIM
    r3 = lambda a_: a_.reshape(bsz, t, a_.shape[1])
    if mode == "prompt":
        c0 = jnp.zeros((bsz, n_m_heads, M_HEAD_DIM, M_HEAD_DIM), F32)
        n0 = jnp.zeros((bsz, n_m_heads, M_HEAD_DIM), F32)
        m0 = jnp.zeros((bsz, 1, LANES), F32)
        cv0 = jnp.zeros((bsz, SUBLANES, mw), F32)
        n_sel = min(TOPK_MAX, t // 4)
        a = _attn(r3(q), r3(qi), r3(wi), r3(kb), r3(vb), r3(kiab),
                  tq=tq, tk=tk, causal=True, s_valid=t, n_sel=n_sel)
    else:
        ck, cv, ckidx, s_c, s_n, s_m, s_conv = state
        p = ck.shape[1]
        s_valid = p + t
        s_pad = -(-s_valid // tk) * tk
        padk = lambda a_: jnp.pad(a_, ((0, 0), (0, s_pad - s_valid), (0, 0)))
        k_all = padk(jnp.concatenate([ck.reshape(bsz, p, kvw).astype(BF16), r3(kb)], axis=1))
        v_all = padk(jnp.concatenate([cv.reshape(bsz, p, kvw).astype(BF16), r3(vb)], axis=1))
        cki = ckidx.astype(BF16)
        zki = jnp.zeros_like(cki)
        ki_c = jnp.concatenate([cki, zki, zki, cki], axis=2)
        ki_all = padk(jnp.concatenate([ki_c, r3(kiab)], axis=1))
        c0, n0 = s_c.astype(F32), s_n.astype(F32)
        m0 = jnp.pad(s_m.astype(F32), ((0, 0), (0, LANES - n_m_heads)))[:, None, :]
        cv0 = jnp.pad(s_conv.astype(F32), ((0, 0), (SUBLANES - (CONV_W - 1), 0), (0, 0)))
        n_sel = min(TOPK_MAX, s_valid // 4)
        a = _attn(r3(q), r3(qi), r3(wi), k_all, v_all, ki_all,
                  tq=tq, tk=tk, causal=False, s_valid=s_valid, n_sel=n_sel)

    bg = jnp.stack([_pad_cols(b_igate[None], LANES)[0], _pad_cols(b_fgate[None], LANES)[0]])
    tri = (np.arange(L)[:, None] >= np.arange(L)[None, :]).astype(np.float32)
    mo, c_new, n_new, m_new, cv_new = _mlstm(
        u, vm, om, ig, fg, c0, n0, m0, cv0,
        conv_w, conv_b[None], wq_m.astype(BF16), wk_m.astype(BF16),
        jnp.swapaxes(wk_m, 1, 2).astype(BF16), bg, hnorm_g[None], jnp.asarray(tri),
        b=bsz, L=L, lb=lb)

    y = _ffn(x2, a.reshape(m, aw), mo, w_out.astype(BF16), norm2_g[None],
             w_up.astype(BF16), w_down.astype(BF16), tm=tm_ffn, tf=tf)
    new_state = (
        k.reshape(bsz, t, A_KV_HEADS, A_HEAD_DIM), v.reshape(bsz, t, A_KV_HEADS, A_HEAD_DIM),
        ki.reshape(bsz, t, IDX_DIM), c_new, n_new, m_new[:, 0, :n_m_heads],
        cv_new[:, SUBLANES - (CONV_W - 1):, :])
    return y.reshape(bsz, t, d), new_state


def kernel(x_prompt, x_sample, cache_k, cache_v, cache_kidx, state_mlstm_C, state_mlstm_n, state_mlstm_m,
           state_conv, norm1_g, w_in, q_norm_g, k_norm_g, conv_w, conv_b, wq_m, wk_m, b_igate, b_fgate,
           hnorm_g, w_out, norm2_g, w_up, w_down):
    depth = w_in.shape[0]
    d = x_prompt.shape[-1]
    mix = w_out.shape[1]
    aw = mix // 2
    mw = mix - aw
    kvw = A_KV_HEADS * A_HEAD_DIM
    n_m_heads = mw // M_HEAD_DIM
    splits = (aw, kvw, kvw, IDX_HEADS * IDX_DIM, IDX_DIM, IDX_HEADS, mw, mw, mw, n_m_heads, n_m_heads)
    t_p = x_prompt.shape[1]
    b_s, t_s = x_sample.shape[0], x_sample.shape[1]
    past = cache_k.shape[2]

    y_p, y_s = x_prompt, x_sample
    new_p, new_s = [], []
    for l in range(depth):
        params = (norm1_g[l], w_in[l], q_norm_g[l], k_norm_g[l], conv_w[l], conv_b[l], wq_m[l], wk_m[l],
                  b_igate[l], b_fgate[l], hnorm_g[l], w_out[l], norm2_g[l], w_up[l], w_down[l])
        y_p, st_p = _layer(
            y_p, jnp.arange(t_p), None, "prompt", None, params, splits,
            tm_proj=256, tm_ffn=512, tf=1024, tq=128, tk=512, L=128, lb=128)
        y_s, st_s = _layer(
            y_s, past + (jnp.arange(b_s * t_s) % t_s), None, "sample",
            (cache_k[l], cache_v[l], cache_kidx[l], state_mlstm_C[l], state_mlstm_n[l],
             state_mlstm_m[l], state_conv[l]), params, splits,
            tm_proj=b_s * t_s, tm_ffn=b_s * t_s, tf=512, tq=LANES, tk=384, L=128, lb=t_s)
        new_p.append(st_p)
        new_s.append(st_s)

    stack = lambda states, i: jnp.stack([s[i] for s in states])
    return (y_p, y_s,
            *[stack(new_p, i) for i in range(7)],
            *[stack(new_s, i) for i in range(7)])
```

```python
import functools

import jax
import jax.numpy as jnp
import numpy as np
from jax import lax
from jax.experimental import pallas as pl
from jax.experimental.pallas import tpu as pltpu

F32 = jnp.float32
BF16 = jnp.bfloat16
I32 = jnp.int32

EPS = 1e-6
ROPE_THETA = 500000.0
CHUNK = 64
CHUNK_LOG2 = 6
TOPK_MAX = 256
A_HEAD_DIM = 128
A_KV_HEADS = 2
IDX_HEADS = 8
IDX_DIM = 64
M_HEAD_DIM = 128
CONV_W = 4

LANES = 128
SUBLANES = 8
VMEM_LIMIT = 56 * 1024 * 1024
NEG_BIG = -1e30
LOG2_E = 1.4426950408889634
INT_MIN = -(2 ** 31)
BIG_J = 2 ** 30
UNSETTLED = 1e9
SEARCH_EARLY_BITS = 26


def _rms(x, g):
    r = lax.rsqrt(jnp.mean(x * x, axis=-1, keepdims=True) + EPS)
    return x * r * g


def _rope(t, c, sa, sb, half):
    return t * c + pltpu.roll(t, LANES - half, 1) * sa + pltpu.roll(t, half, 1) * sb


def _sigmoid(x):
    return 1.0 / (1.0 + jnp.exp(-x))


def _dot(a, b):
    return jnp.dot(a, b, preferred_element_type=F32)


def _dot_t(a, b):
    return lax.dot_general(a, b, (((1,), (1,)), ((), ())), preferred_element_type=F32)


def _proj_kernel(x_ref, g1_ref, wa_ref, wb_ref, qg_ref, kg_ref, ra_ref, ri_ref,
                 q_out, k_out, v_out, ki_out, kb_out, vb_out, kiab_out, qi_out,
                 wi_out, ig_out, fg_out, u_out, vm_out, om_out, *, n_q_heads):
    xn = _rms(x_ref[...], g1_ref[...]).astype(BF16)
    ra = ra_ref[...]
    ca, saa, sba = ra[:, :LANES], ra[:, LANES:2 * LANES], ra[:, 2 * LANES:]
    ri = ri_ref[...]
    ci, sai, sbi = ri[:, :LANES], ri[:, LANES:2 * LANES], ri[:, 2 * LANES:]
    qg = qg_ref[...]
    kg = kg_ref[...]
    aw = n_q_heads * A_HEAD_DIM
    kvw = A_KV_HEADS * A_HEAD_DIM
    q_scale = A_HEAD_DIM ** -0.5 * LOG2_E
    i_scale = IDX_DIM ** -0.5

    for c in range(0, aw, 512):
        z = _dot(xn, wa_ref[:, c:c + 512])
        for j in range(0, 512, LANES):
            qh = _rope(_rms(z[:, j:j + LANES], qg), ca, saa, sba, A_HEAD_DIM // 8) * q_scale
            q_out[:, c + j:c + j + LANES] = qh.astype(BF16)
    off = aw
    z = _dot(xn, wa_ref[:, off:off + 2 * kvw])
    for j in range(0, kvw, LANES):
        kh = _rope(_rms(z[:, j:j + LANES], kg), ca, saa, sba, A_HEAD_DIM // 8)
        k_out[:, j:j + LANES] = kh
        kb_out[:, j:j + LANES] = kh.astype(BF16)
    vv = z[:, kvw:]
    v_out[...] = vv
    vb_out[...] = vv.astype(BF16)
    off += 2 * kvw
    iw = IDX_HEADS * IDX_DIM
    z = _dot(xn, wa_ref[:, off:off + iw])
    for j in range(0, iw, LANES):
        qi = _rope(z[:, j:j + LANES], ci, sai, sbi, IDX_DIM // 8) * i_scale
        qi_out[:, j:j + LANES] = qi.astype(BF16)
    off += iw
    z = _dot(xn, wa_ref[:, off:off + 4 * LANES])
    ki2 = _rope(z[:, :LANES], ci, sai, sbi, IDX_DIM // 8)
    ki_out[...] = ki2[:, :IDX_DIM]
    lane = lax.broadcasted_iota(I32, (1, LANES), 1)
    lo = lane < IDX_DIM
    kiab_out[:, :LANES] = jnp.where(lo, ki2, 0.0).astype(BF16)
    kiab_out[:, LANES:] = jnp.where(lo, 0.0, ki2).astype(BF16)
    wi_out[...] = z[:, LANES:2 * LANES]
    ig_out[...] = z[:, 2 * LANES:3 * LANES]
    fg_out[...] = z[:, 3 * LANES:]
    mw = u_out.shape[1]
    for c in range(0, mw, 512):
        u_out[:, c:c + 512] = _dot(xn, wb_ref[:, c:c + 512])
        vm_out[:, c:c + 512] = _dot(xn, wb_ref[:, mw + c:mw + c + 512]).astype(BF16)
        om_out[:, c:c + 512] = _dot(xn, wb_ref[:, 2 * mw + c:2 * mw + c + 512])


def _proj(x, g1, wa, wb, qg, kg, ra, ri, *, tm, n_q_heads, m_width):
    m, d = x.shape
    n_tab = ra.shape[0] // tm
    aw = n_q_heads * A_HEAD_DIM
    kvw = A_KV_HEADS * A_HEAD_DIM
    iw = IDX_HEADS * IDX_DIM
    row = lambda w: pl.BlockSpec((tm, w), lambda i: (i, 0))
    const = lambda a: pl.BlockSpec(a.shape, lambda i: (0, 0), pipeline_mode=pl.Buffered(1))
    tab = pl.BlockSpec((tm, 3 * LANES), lambda i: (i % n_tab, 0))
    outs = [
        (aw, BF16), (kvw, F32), (kvw, F32), (IDX_DIM, F32), (kvw, BF16), (kvw, BF16),
        (2 * LANES, BF16), (iw, BF16), (LANES, F32), (LANES, F32), (LANES, F32),
        (m_width, F32), (m_width, BF16), (m_width, F32),
    ]
    return pl.pallas_call(
        functools.partial(_proj_kernel, n_q_heads=n_q_heads),
        grid=(m // tm,),
        in_specs=[row(d), const(g1), const(wa), const(wb), const(qg), const(kg), tab, tab],
        out_specs=[row(w) for w, _ in outs],
        out_shape=[jax.ShapeDtypeStruct((m, w), dt) for w, dt in outs],
        compiler_params=pltpu.CompilerParams(
            dimension_semantics=("arbitrary",), vmem_limit_bytes=VMEM_LIMIT),
        name="proj",
    )(x, g1, wa, wb, qg, kg, ra, ri)


def _pad_rows(x, n):
    if x.shape[0] == n:
        return x
    return jnp.concatenate([x, jnp.zeros((n - x.shape[0],) + x.shape[1:], x.dtype)], axis=0)


V_EXT_ROWS = A_HEAD_DIM + 16


def _attn_kernel(q_ref, qi_ref, wi_ref, k_ref, vt_ref, ki_ref, o_ref,
                 key_scr, tie_scr, qs_scr, qis_scr, m_scr, acc_scr, sa_scr, sb_scr, da_scr, db_scr,
                 *, tq, tqb, tk, causal, s_valid, n_sel, n_tiles, group):
    start = pl.program_id(1) * tqb
    nr = tk // SUBLANES
    n_acc = 8
    lane = lax.broadcasted_iota(I32, (1, tq), 1)
    rowi = lax.broadcasted_iota(I32, (tk, 1), 0)
    if causal:
        limit = (lax.shift_right_logical(start + lane, CHUNK_LOG2) + 1) * CHUNK
        nkt = jnp.minimum((start + tqb + tk - 1) // tk, n_tiles)
    else:
        limit = jnp.full((1, tq), s_valid, I32)
        nkt = n_tiles

    q = _pad_rows(q_ref[0], tq)
    for g in range(A_KV_HEADS):
        for r in range(group):
            h = g * group + r
            qs_scr[g, r * tq:(r + 1) * tq, :] = q[:, h * LANES:(h + 1) * LANES]
    qi = _pad_rows(qi_ref[0], tq)
    for j in range(IDX_HEADS // 2):
        qis_scr[j * tq:(j + 1) * tq, :] = qi[:, j * LANES:(j + 1) * LANES]
    w_t = jnp.transpose(_pad_rows(wi_ref[0], tq))[:IDX_HEADS] * (IDX_HEADS ** -0.5)

    d_bufs = (da_scr, db_scr)

    def stage_scores(tile, par):
        t_c = jnp.minimum(tile, nkt - 1)
        off = pl.multiple_of(t_c * tk, tk)
        kiab = ki_ref[0, pl.ds(off, tk), :]
        for half in range(IDX_HEADS // 4):
            qh = qis_scr[half * 2 * tq:(half + 1) * 2 * tq, :]
            for odd in range(2):
                d_bufs[par][2 * half + odd] = _dot_t(kiab[:, odd * LANES:(odd + 1) * LANES], qh)

    def finish_scores(tile, par):
        t_c = jnp.minimum(tile, nkt - 1)
        acc = jnp.zeros((tk, tq), F32)
        for half in range(IDX_HEADS // 4):
            for odd in range(2):
                d = d_bufs[par][2 * half + odd]
                for jj in range(2):
                    h = 2 * (2 * half + jj) + odd
                    acc = acc + w_t[h:h + 1, :] * jnp.maximum(d[:, jj * tq:(jj + 1) * tq], 0.0)
        bits = lax.bitcast_convert_type(acc, I32)
        key = bits ^ (lax.shift_right_arithmetic(bits, 31) & 0x7FFFFFFF)
        key = jnp.where(key == -1, 0, key)
        key_scr[t_c] = jnp.where(t_c * tk + rowi < limit, key, INT_MIN)

    stage_scores(0, 0)

    def score_pair(i, carry):
        stage_scores(2 * i + 1, 1)
        finish_scores(2 * i, 0)
        stage_scores(2 * i + 2, 0)
        finish_scores(2 * i + 1, 1)
        return carry

    lax.fori_loop(0, (nkt + 1) // 2, score_pair, 0)

    def count_ref(ref, pred):
        def body(kt, acc):
            hit = pred(ref[kt].reshape(nr, SUBLANES, tq))
            ones = jnp.where(hit, 1.0, 0.0).reshape(nr // n_acc, n_acc, SUBLANES, tq)
            return acc + jnp.sum(ones, axis=0)
        acc = lax.fori_loop(0, nkt, body, jnp.zeros((n_acc, SUBLANES, tq), F32))
        return jnp.sum(jnp.sum(acc, axis=0), axis=0, keepdims=True)

    def bit_step(i, carry):
        t_cur, c_cur = carry
        cand = t_cur + lax.shift_left(jnp.int32(1), 31 - i)
        cnt = count_ref(key_scr, lambda key: key >= cand[None])
        ok = cnt >= n_sel
        return jnp.where(ok, cand, t_cur), jnp.where(ok, cnt, c_cur)

    def any_lane(mask):
        return jnp.max(jnp.where(mask, 1.0, 0.0)) > 0.0

    valid = lane < tqb
    n_self = float(n_sel)

    def settle(t_cur, c_cur):
        pend = jnp.logical_and(valid, c_cur != n_self)
        return lax.cond(any_lane(pend), lambda: count_ref(key_scr, lambda key: key > t_cur[None]),
                        lambda: jnp.zeros((1, tq), F32))

    t_e, c_e = lax.fori_loop(
        0, SEARCH_EARLY_BITS, bit_step,
        (jnp.full((1, tq), INT_MIN, I32), jnp.where(limit < n_sel, n_self, UNSETTLED)))
    g_e = settle(t_e, c_e)
    unresolved = jnp.logical_and(jnp.logical_and(valid, c_e != n_self), g_e >= n_self)

    def finish_search():
        t_l, c_l = lax.fori_loop(SEARCH_EARLY_BITS, 32, bit_step, (t_e, c_e))
        return t_l, c_l, settle(t_l, c_l)

    t_fin, c_fin, c_gt = lax.cond(any_lane(unresolved), finish_search, lambda: (t_e, c_e, g_e))
    is_min = t_fin == INT_MIN
    needs = jnp.logical_and(jnp.logical_and(jnp.logical_not(is_min), c_fin > n_sel), valid)
    j_fast = jnp.where(is_min, 0, BIG_J)

    def tie_search():
        rem = n_self - c_gt

        def mark(kt, carry):
            tie_scr[kt] = jnp.where(key_scr[kt] == t_fin, kt * tk + rowi, BIG_J)
            return carry

        lax.fori_loop(0, nkt, mark, 0)
        j_bits = (n_tiles * tk).bit_length()

        def j_pending(carry):
            i, _, f_cur = carry
            return jnp.logical_and(i < j_bits, any_lane(jnp.logical_and(needs, f_cur != rem)))

        def j_step(carry):
            i, j_cur, f_cur = carry
            cand = j_cur + lax.shift_left(jnp.int32(1), j_bits - 1 - i)
            f = count_ref(tie_scr, lambda col: col < cand[None])
            ok = f <= rem
            return i + 1, jnp.where(ok, cand, j_cur), jnp.where(ok, f, f_cur)

        _, j_slow, _ = lax.while_loop(
            j_pending, j_step, (jnp.int32(0), jnp.zeros((1, tq), I32), jnp.zeros((1, tq), F32)))
        return jnp.where(needs, j_slow, j_fast)

    j_fin = lax.cond(any_lane(needs), tie_search, lambda: j_fast)

    m_scr[...] = jnp.full(m_scr.shape, NEG_BIG, F32)
    acc_scr[...] = jnp.zeros(acc_scr.shape, F32)

    s_bufs = (sa_scr, sb_scr)

    def stage_logits(tile, par):
        t_c = jnp.minimum(tile, nkt - 1)
        off = pl.multiple_of(t_c * tk, tk)
        for g in range(A_KV_HEADS):
            s_bufs[par][g] = _dot_t(k_ref[0, pl.ds(off, tk), g * LANES:(g + 1) * LANES], qs_scr[g])

    def attend_tile(tile, par):
        t_c = jnp.minimum(tile, nkt - 1)
        key = key_scr[t_c]
        sel = jnp.logical_or(key > t_fin, jnp.logical_and(key == t_fin, t_c * tk + rowi < j_fin))
        sel = jnp.logical_and(sel, tile < nkt)
        bias = jnp.where(sel, 0.0, -jnp.inf)
        bias = jnp.concatenate([bias] * group, axis=1)
        for g in range(A_KV_HEADS):
            s = s_bufs[par][g] + bias
            m_old = m_scr[g]
            m_new = jnp.maximum(m_old, jnp.max(s, axis=0, keepdims=True))
            alpha = jnp.exp2(m_old - m_new)
            p = jnp.exp2(s - m_new).astype(BF16)
            acc_scr[g] = alpha * acc_scr[g] + _dot(vt_ref[0, t_c, g], p)
            m_scr[g] = m_new

    stage_logits(0, 0)

    def attend_pair(i, carry):
        stage_logits(2 * i + 1, 1)
        attend_tile(2 * i, 0)
        stage_logits(2 * i + 2, 0)
        attend_tile(2 * i + 1, 1)
        return carry

    lax.fori_loop(0, (nkt + 1) // 2, attend_pair, 0)
    for g in range(A_KV_HEADS):
        acc = acc_scr[g]
        out_t = acc[:A_HEAD_DIM] / acc[A_HEAD_DIM:A_HEAD_DIM + 1]
        for r in range(group):
            h = g * group + r
            o = jnp.transpose(out_t[:, r * tq:(r + 1) * tq])
            o_ref[0, :, h * LANES:(h + 1) * LANES] = o[:tqb].astype(BF16)


def _attn(q, qi, wi, kb, vb, kiab, *, tq, tk, causal, s_valid, n_sel):
    b, t, aw = q.shape
    tqb = min(tq, t)
    s_pad = kb.shape[1]
    n_tiles = s_pad // tk
    n_heads = aw // A_HEAD_DIM
    group = n_heads // A_KV_HEADS
    vt = jnp.transpose(vb.reshape(b, s_pad, A_KV_HEADS, A_HEAD_DIM), (0, 2, 3, 1))
    vt = jnp.concatenate([vt, jnp.ones((b, A_KV_HEADS, V_EXT_ROWS - A_HEAD_DIM, s_pad), BF16)], axis=2)
    vt = jnp.transpose(vt.reshape(b, A_KV_HEADS, V_EXT_ROWS, n_tiles, tk), (0, 3, 1, 2, 4))
    qspec = lambda w: pl.BlockSpec((1, tqb, w), lambda bi, qi_: (bi, qi_, 0))
    kspec = lambda w: pl.BlockSpec((1, s_pad, w), lambda bi, qi_: (bi, 0, 0))
    kern = functools.partial(_attn_kernel, tq=tq, tqb=tqb, tk=tk, causal=causal, s_valid=s_valid,
                             n_sel=n_sel, n_tiles=n_tiles, group=group)
    return pl.pallas_call(
        kern,
        grid=(b, t // tqb),
        in_specs=[qspec(aw), qspec(qi.shape[2]), qspec(LANES),
                  kspec(kb.shape[2]),
                  pl.BlockSpec((1,) + vt.shape[1:], lambda bi, qi_: (bi, 0, 0, 0, 0)),
                  kspec(kiab.shape[2])],
        out_specs=qspec(aw),
        out_shape=jax.ShapeDtypeStruct((b, t, aw), BF16),
        scratch_shapes=[
            pltpu.VMEM((n_tiles, tk, tq), I32),
            pltpu.VMEM((n_tiles, tk, tq), I32),
            pltpu.VMEM((A_KV_HEADS, group * tq, LANES), BF16),
            pltpu.VMEM((IDX_HEADS // 2 * tq, LANES), BF16),
            pltpu.VMEM((A_KV_HEADS, 1, group * tq), F32),
            pltpu.VMEM((A_KV_HEADS, V_EXT_ROWS, group * tq), F32),
            pltpu.VMEM((A_KV_HEADS, tk, group * tq), F32),
            pltpu.VMEM((A_KV_HEADS, tk, group * tq), F32),
            pltpu.VMEM((IDX_HEADS // 2, tk, 2 * tq), F32),
            pltpu.VMEM((IDX_HEADS // 2, tk, 2 * tq), F32),
        ],
        compiler_params=pltpu.CompilerParams(
            dimension_semantics=("arbitrary", "arbitrary"), vmem_limit_bytes=VMEM_LIMIT),
        name="attn",
    )(q, qi, wi, kb, vt, kiab)


def _mlstm_kernel(u_ref, vm_ref, om_ref, ig_ref, fg_ref, c0_ref, n0_ref, m0_ref, cv0_ref,
                  cw_ref, cb_ref, wq_ref, wk_ref, wkt_ref, bg_ref, hg_ref, tri_ref,
                  mo_ref, c_out, n_out, m_out, cv_out,
                  c_scr, n_scr, m_scr, prev_scr, ubuf,
                  *, L, lb, n_heads):
    t = pl.program_id(1)
    nt = pl.num_programs(1)
    hd = M_HEAD_DIM
    pad = SUBLANES

    @pl.when(t == 0)
    def _():
        c_scr[...] = c0_ref[0]
        n_scr[...] = n0_ref[0]
        m_scr[...] = m0_ref[0]
        prev_scr[...] = cv0_ref[0]

    ubuf[0:pad, :] = prev_scr[...]
    ubuf[pad:pad + L, :] = _pad_rows(u_ref[...], L)
    uc = cb_ref[...]
    for i in range(CONV_W):
        uc = uc + ubuf[pad - (CONV_W - 1) + i:pad - (CONV_W - 1) + i + L, :] * cw_ref[i:i + 1, :]
    prev_scr[...] = ubuf[lb:lb + pad, :]
    uh = (uc * _sigmoid(uc)).astype(BF16)

    vm = _pad_rows(vm_ref[...], L)
    om = _pad_rows(om_ref[...], L)
    rowv = lax.broadcasted_iota(I32, (L, LANES), 0) < lb
    li = jnp.where(rowv, _pad_rows(ig_ref[...], L) + bg_ref[0:1, :], NEG_BIG)
    xf = _pad_rows(fg_ref[...], L) + bg_ref[1:2, :]
    lf = jnp.where(rowv, jnp.minimum(xf, 0.0) - jnp.log(1.0 + jnp.exp(-jnp.abs(xf))), 0.0)
    b = jnp.dot(tri_ref[...], lf, precision=lax.Precision.HIGHEST, preferred_element_type=F32)
    c = li - b
    c_t = c.T
    m_prev = m_scr[...]
    inter = b + m_prev
    b_last = b[L - 1:L, :]
    dec = b_last - b + li
    m_new = jnp.maximum(b_last + m_prev, jnp.max(dec, axis=0, keepdims=True))
    wts = jnp.exp(dec - m_new)
    wts_t = wts.T
    sc = jnp.exp(b_last + m_prev - m_new)
    causal = lax.broadcasted_iota(I32, (L, L), 0) >= lax.broadcasted_iota(I32, (L, L), 1)
    k_scale = hd ** -0.5

    for h in range(n_heads):
        sl = slice(h * hd, (h + 1) * hd)
        uh_h = uh[:, sl]
        q_h = _dot(uh_h, wq_ref[h])
        k_h = _dot(uh_h, wk_ref[h]) * k_scale
        kt_h = _dot_t(wkt_ref[h], uh_h) * k_scale
        v_h = vm[:, sl]
        dmat = jnp.where(causal, b[:, h:h + 1] + c_t[h:h + 1, :], -jnp.inf)
        inter_h = inter[:, h:h + 1]
        m_t = jnp.maximum(inter_h, jnp.max(dmat, axis=-1, keepdims=True))
        q_b = q_h.astype(BF16)
        sw = jnp.exp(dmat - m_t) * _dot_t(q_b, k_h.astype(BF16))
        sp = jnp.exp(inter_h - m_t)
        c_h = c_scr[h]
        n_h = n_scr[h:h + 1, :]
        num = sp * _dot(q_b, c_h.astype(BF16)) + _dot(sw.astype(BF16), v_h)
        den = sp * jnp.sum(q_h * n_h, axis=-1, keepdims=True) + jnp.sum(sw, axis=-1, keepdims=True)
        hh = num / jnp.maximum(jnp.abs(den), jnp.exp(-m_t))
        hn = _rms(hh, hg_ref[:, sl])
        out = _sigmoid(om[:, sl]) * hn
        mo_ref[:, sl] = out[:lb].astype(BF16)
        sc_h = sc[:, h:h + 1]
        c_scr[h] = sc_h * c_h + _dot((kt_h * wts_t[h:h + 1, :]).astype(BF16), v_h)
        n_scr[h:h + 1, :] = sc_h * n_h + jnp.sum(k_h * wts[:, h:h + 1], axis=0, keepdims=True)
    m_scr[...] = m_new

    @pl.when(t == nt - 1)
    def _():
        c_out[0] = c_scr[...]
        n_out[0] = n_scr[...]
        m_out[0] = m_scr[...]
        cv_out[0] = prev_scr[...]


def _mlstm(u, vm, om, ig, fg, c0, n0, m0, cv0, cw, cb, wq, wk, wkt, bg, hg, tri, *, b, L, lb):
    m, mw = u.shape
    nt = m // (b * lb)
    n_heads = mw // M_HEAD_DIM
    row = lambda w: pl.BlockSpec((lb, w), lambda bi, ti: (bi * nt + ti, 0))
    const = lambda a: pl.BlockSpec(a.shape, lambda bi, ti: (0,) * a.ndim)
    perb = lambda a: pl.BlockSpec((1,) + a.shape[1:], lambda bi, ti: (bi,) + (0,) * (a.ndim - 1))
    out_shapes = [
        jax.ShapeDtypeStruct((m, mw), BF16),
        jax.ShapeDtypeStruct(c0.shape, F32),
        jax.ShapeDtypeStruct(n0.shape, F32),
        jax.ShapeDtypeStruct(m0.shape, F32),
        jax.ShapeDtypeStruct(cv0.shape, F32),
    ]
    return pl.pallas_call(
        functools.partial(_mlstm_kernel, L=L, lb=lb, n_heads=n_heads),
        grid=(b, nt),
        in_specs=[row(mw), row(mw), row(mw), row(LANES), row(LANES),
                  perb(c0), perb(n0), perb(m0), perb(cv0),
                  const(cw), const(cb), const(wq), const(wk), const(wkt), const(bg), const(hg), const(tri)],
        out_specs=[row(mw), perb(c0), perb(n0), perb(m0), perb(cv0)],
        out_shape=out_shapes,
        scratch_shapes=[
            pltpu.VMEM(c0.shape[1:], F32),
            pltpu.VMEM(n0.shape[1:], F32),
            pltpu.VMEM(m0.shape[1:], F32),
            pltpu.VMEM((SUBLANES, mw), F32),
            pltpu.VMEM((L + SUBLANES, mw), F32),
        ],
        compiler_params=pltpu.CompilerParams(
            dimension_semantics=("arbitrary", "arbitrary"), vmem_limit_bytes=VMEM_LIMIT),
        name="mlstm",
    )(u, vm, om, ig, fg, c0, n0, m0, cv0, cw, cb, wq, wk, wkt, bg, hg, tri)


def _ffn_kernel(x_ref, a_ref, mo_ref, wo_ref, g2_ref, wu_ref, wd_ref, y_ref, hn_scr):
    j = pl.program_id(1)

    @pl.when(j == 0)
    def _():
        aw = a_ref.shape[1]
        h = x_ref[...] + _dot(a_ref[...], wo_ref[:aw, :]) + _dot(mo_ref[...], wo_ref[aw:, :])
        y_ref[...] = h
        hn_scr[...] = _rms(h, g2_ref[...]).astype(BF16)

    f = jnp.maximum(_dot(hn_scr[...], wu_ref[...]), 0.0)
    y_ref[...] += _dot((f * f).astype(BF16), wd_ref[...])


def _ffn(x, a, mo, wo, g2, wu, wd, *, tm, tf):
    m, d = x.shape
    dff = wu.shape[1]
    row = lambda w: pl.BlockSpec((tm, w), lambda i, j: (i, 0))
    const = lambda arr: pl.BlockSpec(arr.shape, lambda i, j: (0, 0), pipeline_mode=pl.Buffered(1))
    return pl.pallas_call(
        _ffn_kernel,
        grid=(m // tm, dff // tf),
        in_specs=[row(d), row(a.shape[1]), row(mo.shape[1]), const(wo), const(g2),
                  pl.BlockSpec((d, tf), lambda i, j: (0, j)),
                  pl.BlockSpec((tf, d), lambda i, j: (j, 0))],
        out_specs=row(d),
        out_shape=jax.ShapeDtypeStruct((m, d), F32),
        scratch_shapes=[pltpu.VMEM((tm, d), BF16)],
        compiler_params=pltpu.CompilerParams(
            dimension_semantics=("arbitrary", "arbitrary"), vmem_limit_bytes=VMEM_LIMIT),
        name="ffn",
    )(x, a, mo, wo, g2, wu, wd)


def _rope_tables(pos, rot, width):
    half = rot // 2
    inv_freq = ROPE_THETA ** (-jnp.arange(half, dtype=F32) / half)
    ang = pos.astype(F32)[:, None] * inv_freq[None, :]
    cos, sin = jnp.cos(ang), jnp.sin(ang)
    n = pos.shape[0]
    rest1 = jnp.ones((n, width - rot), F32)
    rest0 = jnp.zeros((n, width - rot), F32)
    z = jnp.zeros_like(sin)
    c = jnp.concatenate([cos, cos, rest1], axis=1)
    sa = jnp.concatenate([-sin, z, rest0], axis=1)
    sb = jnp.concatenate([z, sin, rest0], axis=1)
    rep = LANES // width
    return jnp.concatenate([jnp.tile(c, (1, rep)), jnp.tile(sa, (1, rep)), jnp.tile(sb, (1, rep))], axis=1)


def _pad_cols(w, n):
    return jnp.pad(w, ((0, 0), (0, n - w.shape[1])))


def _tile_plan(n_rows, t, prompt):
    if prompt:
        return dict(tm_proj=256, tm_ffn=512, tf=1024, tq=LANES, tk=512, L=128, lb=128)
    return dict(tm_proj=n_rows, tm_ffn=n_rows, tf=512, tq=LANES, tk=384, L=128, lb=t)


def _layer(x, pos_rows, mode, state, params, splits):
    (norm1_g, w_in, q_norm_g, k_norm_g, conv_w, conv_b, wq_m, wk_m, b_igate, b_fgate,
     hnorm_g, w_out, norm2_g, w_up, w_down) = params
    bsz, t, d = x.shape
    m = bsz * t
    plan = _tile_plan(m, t, mode == "prompt")
    tm_proj, tm_ffn, tf, tq, tk, L, lb = (plan[k] for k in ("tm_proj", "tm_ffn", "tf", "tq", "tk", "L", "lb"))
    x2 = x.reshape(m, d)
    offs = np.cumsum((0,) + tuple(splits))
    col = lambda i: w_in[:, int(offs[i]):int(offs[i + 1])]
    aw, mw = splits[0], splits[6]
    n_q_heads = aw // A_HEAD_DIM
    n_m_heads = mw // M_HEAD_DIM
    wa = jnp.concatenate(
        [col(0), col(1), col(2), col(3), col(4), col(4),
         _pad_cols(col(5), LANES), _pad_cols(col(9), LANES), _pad_cols(col(10), LANES)], axis=1).astype(BF16)
    wb = jnp.concatenate([col(6), col(7), col(8)], axis=1).astype(BF16)
    ra = _rope_tables(pos_rows, A_HEAD_DIM // 4, A_HEAD_DIM)
    ri = _rope_tables(pos_rows, IDX_DIM // 4, IDX_DIM)

    (q, k, v, ki, kb, vb, kiab, qi, wi, ig, fg, u, vm, om) = _proj(
        x2, norm1_g[None], wa, wb, q_norm_g[None], k_norm_g[None], ra, ri,
        tm=tm_proj, n_q_heads=n_q_heads, m_width=mw)

    kvw = A_KV_HEADS * A_HEAD_DIM
    r3 = lambda a_: a_.reshape(bsz, t, a_.shape[1])
    if mode == "prompt":
        c0 = jnp.zeros((bsz, n_m_heads, M_HEAD_DIM, M_HEAD_DIM), F32)
        n0 = jnp.zeros((bsz, n_m_heads, M_HEAD_DIM), F32)
        m0 = jnp.zeros((bsz, 1, LANES), F32)
        cv0 = jnp.zeros((bsz, SUBLANES, mw), F32)
        n_sel = min(TOPK_MAX, t // 4)
        a = _attn(r3(q), r3(qi), r3(wi), r3(kb), r3(vb), r3(kiab),
                  tq=tq, tk=tk, causal=True, s_valid=t, n_sel=n_sel)
    else:
        ck, cv, ckidx, s_c, s_n, s_m, s_conv = state
        p = ck.shape[1]
        s_valid = p + t
        s_pad = -(-s_valid // tk) * tk
        padk = lambda a_: jnp.pad(a_, ((0, 0), (0, s_pad - s_valid), (0, 0)))
        k_all = padk(jnp.concatenate([ck.reshape(bsz, p, kvw).astype(BF16), r3(kb)], axis=1))
        v_all = padk(jnp.concatenate([cv.reshape(bsz, p, kvw).astype(BF16), r3(vb)], axis=1))
        cki = ckidx.astype(BF16)
        zki = jnp.zeros_like(cki)
        ki_c = jnp.concatenate([cki, zki, zki, cki], axis=2)
        ki_all = padk(jnp.concatenate([ki_c, r3(kiab)], axis=1))
        c0, n0 = s_c.astype(F32), s_n.astype(F32)
        m0 = jnp.pad(s_m.astype(F32), ((0, 0), (0, LANES - n_m_heads)))[:, None, :]
        cv0 = jnp.pad(s_conv.astype(F32), ((0, 0), (SUBLANES - (CONV_W - 1), 0), (0, 0)))
        n_sel = min(TOPK_MAX, s_valid // 4)
        a = _attn(r3(q), r3(qi), r3(wi), k_all, v_all, ki_all,
                  tq=tq, tk=tk, causal=False, s_valid=s_valid, n_sel=n_sel)

    bg = jnp.stack([_pad_cols(b_igate[None], LANES)[0], _pad_cols(b_fgate[None], LANES)[0]])
    tri = (np.arange(L)[:, None] >= np.arange(L)[None, :]).astype(np.float32)
    mo, c_new, n_new, m_new, cv_new = _mlstm(
        u, vm, om, ig, fg, c0, n0, m0, cv0,
        conv_w, conv_b[None], wq_m.astype(BF16), wk_m.astype(BF16),
        jnp.swapaxes(wk_m, 1, 2).astype(BF16), bg, hnorm_g[None], jnp.asarray(tri),
        b=bsz, L=L, lb=lb)

    y = _ffn(x2, a.reshape(m, aw), mo, w_out.astype(BF16), norm2_g[None],
             w_up.astype(BF16), w_down.astype(BF16), tm=tm_ffn, tf=tf)
    new_state = (
        k.reshape(bsz, t, A_KV_HEADS, A_HEAD_DIM), v.reshape(bsz, t, A_KV_HEADS, A_HEAD_DIM),
        ki.reshape(bsz, t, IDX_DIM), c_new, n_new, m_new[:, 0, :n_m_heads],
        cv_new[:, SUBLANES - (CONV_W - 1):, :])
    return y.reshape(bsz, t, d), new_state


def kernel(x_prompt, x_sample, cache_k, cache_v, cache_kidx, state_mlstm_C, state_mlstm_n, state_mlstm_m,
           state_conv, norm1_g, w_in, q_norm_g, k_norm_g, conv_w, conv_b, wq_m, wk_m, b_igate, b_fgate,
           hnorm_g, w_out, norm2_g, w_up, w_down):
    depth = w_in.shape[0]
    d = x_prompt.shape[-1]
    mix = w_out.shape[1]
    aw = mix // 2
    mw = mix - aw
    kvw = A_KV_HEADS * A_HEAD_DIM
    n_m_heads = mw // M_HEAD_DIM
    splits = (aw, kvw, kvw, IDX_HEADS * IDX_DIM, IDX_DIM, IDX_HEADS, mw, mw, mw, n_m_heads, n_m_heads)
    t_p = x_prompt.shape[1]
    b_s, t_s = x_sample.shape[0], x_sample.shape[1]
    past = cache_k.shape[2]

    y_p, y_s = x_prompt, x_sample
    new_p, new_s = [], []
    for l in range(depth):
        params = (norm1_g[l], w_in[l], q_norm_g[l], k_norm_g[l], conv_w[l], conv_b[l], wq_m[l], wk_m[l],
                  b_igate[l], b_fgate[l], hnorm_g[l], w_out[l], norm2_g[l], w_up[l], w_down[l])
        y_p, st_p = _layer(
            y_p, jnp.arange(t_p), "prompt", None, params, splits)
        y_s, st_s = _layer(
            y_s, past + (jnp.arange(b_s * t_s) % t_s), "sample",
            (cache_k[l], cache_v[l], cache_kidx[l], state_mlstm_C[l], state_mlstm_n[l],
             state_mlstm_m[l], state_conv[l]), params, splits)
        new_p.append(st_p)
        new_s.append(st_s)

    stack = lambda states, i: jnp.stack([s[i] for s in states])
    return (y_p, y_s,
            *[stack(new_p, i) for i in range(7)],
            *[stack(new_s, i) for i in range(7)])
```

```python
import functools

import jax
import jax.numpy as jnp
import numpy as np
from jax import lax
from jax.experimental import pallas as pl
from jax.experimental.pallas import tpu as pltpu

F32 = jnp.float32
BF16 = jnp.bfloat16
I32 = jnp.int32

EPS = 1e-6
ROPE_THETA = 500000.0
CHUNK = 64
CHUNK_LOG2 = 6
TOPK_MAX = 256
A_HEAD_DIM = 128
A_KV_HEADS = 2
IDX_HEADS = 8
IDX_DIM = 64
M_HEAD_DIM = 128
CONV_W = 4

LANES = 128
SUBLANES = 8
VMEM_LIMIT = 56 * 1024 * 1024
NEG_BIG = -1e30
LOG2_E = 1.4426950408889634
INT_MIN = -(2 ** 31)
BIG_J = 2 ** 30
UNSETTLED = 1e9
SEARCH_EARLY_BITS = 26


def _rms(x, g):
    r = lax.rsqrt(jnp.mean(x * x, axis=-1, keepdims=True) + EPS)
    return x * r * g


def _rope(t, c, sa, sb, half):
    return t * c + pltpu.roll(t, LANES - half, 1) * sa + pltpu.roll(t, half, 1) * sb


def _sigmoid(x):
    return 1.0 / (1.0 + jnp.exp(-x))


def _dot(a, b):
    return jnp.dot(a, b, preferred_element_type=F32)


def _dot_t(a, b):
    return lax.dot_general(a, b, (((1,), (1,)), ((), ())), preferred_element_type=F32)


def _proj_kernel(x_ref, g1_ref, wa_ref, wb_ref, qg_ref, kg_ref, ra_ref, ri_ref,
                 q_out, k_out, v_out, ki_out, kb_out, vb_out, kiab_out, qi_out,
                 wi_out, ig_out, fg_out, u_out, vm_out, om_out, *, n_q_heads):
    xn = _rms(x_ref[...], g1_ref[...]).astype(BF16)
    ra = ra_ref[...]
    ca, saa, sba = ra[:, :LANES], ra[:, LANES:2 * LANES], ra[:, 2 * LANES:]
    ri = ri_ref[...]
    ci, sai, sbi = ri[:, :LANES], ri[:, LANES:2 * LANES], ri[:, 2 * LANES:]
    qg = qg_ref[...]
    kg = kg_ref[...]
    aw = n_q_heads * A_HEAD_DIM
    kvw = A_KV_HEADS * A_HEAD_DIM
    q_scale = A_HEAD_DIM ** -0.5 * LOG2_E
    i_scale = IDX_DIM ** -0.5

    for c in range(0, aw, 512):
        z = _dot(xn, wa_ref[:, c:c + 512])
        for j in range(0, 512, LANES):
            qh = _rope(_rms(z[:, j:j + LANES], qg), ca, saa, sba, A_HEAD_DIM // 8) * q_scale
            q_out[:, c + j:c + j + LANES] = qh.astype(BF16)
    off = aw
    z = _dot(xn, wa_ref[:, off:off + 2 * kvw])
    for j in range(0, kvw, LANES):
        kh = _rope(_rms(z[:, j:j + LANES], kg), ca, saa, sba, A_HEAD_DIM // 8)
        k_out[:, j:j + LANES] = kh
        kb_out[:, j:j + LANES] = kh.astype(BF16)
    vv = z[:, kvw:]
    v_out[...] = vv
    vb_out[...] = vv.astype(BF16)
    off += 2 * kvw
    iw = IDX_HEADS * IDX_DIM
    z = _dot(xn, wa_ref[:, off:off + iw])
    for j in range(0, iw, LANES):
        qi = _rope(z[:, j:j + LANES], ci, sai, sbi, IDX_DIM // 8) * i_scale
        qi_out[:, j:j + LANES] = qi.astype(BF16)
    off += iw
    z = _dot(xn, wa_ref[:, off:off + 4 * LANES])
    ki2 = _rope(z[:, :LANES], ci, sai, sbi, IDX_DIM // 8)
    ki_out[...] = ki2[:, :IDX_DIM]
    lane = lax.broadcasted_iota(I32, (1, LANES), 1)
    lo = lane < IDX_DIM
    kiab_out[:, :LANES] = jnp.where(lo, ki2, 0.0).astype(BF16)
    kiab_out[:, LANES:] = jnp.where(lo, 0.0, ki2).astype(BF16)
    wi_out[...] = z[:, LANES:2 * LANES]
    ig_out[...] = z[:, 2 * LANES:3 * LANES]
    fg_out[...] = z[:, 3 * LANES:]
    mw = u_out.shape[1]
    for c in range(0, mw, 512):
        u_out[:, c:c + 512] = _dot(xn, wb_ref[:, c:c + 512])
        vm_out[:, c:c + 512] = _dot(xn, wb_ref[:, mw + c:mw + c + 512]).astype(BF16)
        om_out[:, c:c + 512] = _dot(xn, wb_ref[:, 2 * mw + c:2 * mw + c + 512])


def _proj(x, g1, wa, wb, qg, kg, ra, ri, *, tm, n_q_heads, m_width):
    m, d = x.shape
    n_tab = ra.shape[0] // tm
    aw = n_q_heads * A_HEAD_DIM
    kvw = A_KV_HEADS * A_HEAD_DIM
    iw = IDX_HEADS * IDX_DIM
    row = lambda w: pl.BlockSpec((tm, w), lambda i: (i, 0))
    const = lambda a: pl.BlockSpec(a.shape, lambda i: (0, 0), pipeline_mode=pl.Buffered(1))
    tab = pl.BlockSpec((tm, 3 * LANES), lambda i: (i % n_tab, 0))
    outs = [
        (aw, BF16), (kvw, F32), (kvw, F32), (IDX_DIM, F32), (kvw, BF16), (kvw, BF16),
        (2 * LANES, BF16), (iw, BF16), (LANES, F32), (LANES, F32), (LANES, F32),
        (m_width, F32), (m_width, BF16), (m_width, F32),
    ]
    return pl.pallas_call(
        functools.partial(_proj_kernel, n_q_heads=n_q_heads),
        grid=(m // tm,),
        in_specs=[row(d), const(g1), const(wa), const(wb), const(qg), const(kg), tab, tab],
        out_specs=[row(w) for w, _ in outs],
        out_shape=[jax.ShapeDtypeStruct((m, w), dt) for w, dt in outs],
        compiler_params=pltpu.CompilerParams(
            dimension_semantics=("arbitrary",), vmem_limit_bytes=VMEM_LIMIT),
        name="proj",
    )(x, g1, wa, wb, qg, kg, ra, ri)


def _pad_rows(x, n):
    if x.shape[0] == n:
        return x
    return jnp.concatenate([x, jnp.zeros((n - x.shape[0],) + x.shape[1:], x.dtype)], axis=0)


V_EXT_ROWS = A_HEAD_DIM + 16


def _attn_kernel(q_ref, qi_ref, wi_ref, k_ref, vt_ref, ki_ref, o_ref,
                 key_scr, tie_scr, qs_scr, qis_scr, m_scr, acc_scr, sa_scr, sb_scr, cm_scr, da_scr, db_scr,
                 *, tq, tqb, tk, causal, s_valid, n_sel, n_tiles, group):
    start = pl.program_id(1) * tqb
    nr = tk // SUBLANES
    n_acc = 8
    lane = lax.broadcasted_iota(I32, (1, tq), 1)
    rowi = lax.broadcasted_iota(I32, (tk, 1), 0)
    if causal:
        limit = (lax.shift_right_logical(start + lane, CHUNK_LOG2) + 1) * CHUNK
        nkt = jnp.minimum((start + tqb + tk - 1) // tk, n_tiles)
    else:
        limit = jnp.full((1, tq), s_valid, I32)
        nkt = n_tiles

    q = _pad_rows(q_ref[0], tq)
    for g in range(A_KV_HEADS):
        for r in range(group):
            h = g * group + r
            qs_scr[g, r * tq:(r + 1) * tq, :] = q[:, h * LANES:(h + 1) * LANES]
    qi = _pad_rows(qi_ref[0], tq)
    for j in range(IDX_HEADS // 2):
        qis_scr[j * tq:(j + 1) * tq, :] = qi[:, j * LANES:(j + 1) * LANES]
    w_t = jnp.transpose(_pad_rows(wi_ref[0], tq))[:IDX_HEADS] * (IDX_HEADS ** -0.5)

    d_bufs = (da_scr, db_scr)

    def stage_scores(tile, par):
        t_c = jnp.minimum(tile, nkt - 1)
        off = pl.multiple_of(t_c * tk, tk)
        kiab = ki_ref[0, pl.ds(off, tk), :]
        for half in range(IDX_HEADS // 4):
            qh = qis_scr[half * 2 * tq:(half + 1) * 2 * tq, :]
            for odd in range(2):
                d_bufs[par][2 * half + odd] = _dot_t(kiab[:, odd * LANES:(odd + 1) * LANES], qh)

    def finish_scores(tile, par):
        t_c = jnp.minimum(tile, nkt - 1)
        acc = jnp.zeros((tk, tq), F32)
        for half in range(IDX_HEADS // 4):
            for odd in range(2):
                d = d_bufs[par][2 * half + odd]
                for jj in range(2):
                    h = 2 * (2 * half + jj) + odd
                    acc = acc + w_t[h:h + 1, :] * jnp.maximum(d[:, jj * tq:(jj + 1) * tq], 0.0)
        bits = lax.bitcast_convert_type(acc, I32)
        key = bits ^ (lax.shift_right_arithmetic(bits, 31) & 0x7FFFFFFF)
        key = jnp.where(key == -1, 0, key)
        key_scr[t_c] = jnp.where(t_c * tk + rowi < limit, key, INT_MIN)

    stage_scores(0, 0)

    def score_pair(i, carry):
        stage_scores(2 * i + 1, 1)
        finish_scores(2 * i, 0)
        stage_scores(2 * i + 2, 0)
        finish_scores(2 * i + 1, 1)
        return carry

    lax.fori_loop(0, (nkt + 1) // 2, score_pair, 0)

    def count_ref(ref, pred):
        def body(kt, acc):
            hit = pred(ref[kt].reshape(nr, SUBLANES, tq))
            ones = jnp.where(hit, 1.0, 0.0).reshape(nr // n_acc, n_acc, SUBLANES, tq)
            return acc + jnp.sum(ones, axis=0)
        acc = lax.fori_loop(0, nkt, body, jnp.zeros((n_acc, SUBLANES, tq), F32))
        return jnp.sum(jnp.sum(acc, axis=0), axis=0, keepdims=True)

    def bit_step(i, carry):
        t_cur, c_cur = carry
        cand = t_cur + lax.shift_left(jnp.int32(1), 31 - i)
        cnt = count_ref(key_scr, lambda key: key >= cand[None])
        ok = cnt >= n_sel
        return jnp.where(ok, cand, t_cur), jnp.where(ok, cnt, c_cur)

    def any_lane(mask):
        return jnp.max(jnp.where(mask, 1.0, 0.0)) > 0.0

    valid = lane < tqb
    n_self = float(n_sel)

    def settle(t_cur, c_cur):
        pend = jnp.logical_and(valid, c_cur != n_self)
        return lax.cond(any_lane(pend), lambda: count_ref(key_scr, lambda key: key > t_cur[None]),
                        lambda: jnp.zeros((1, tq), F32))

    t_e, c_e = lax.fori_loop(
        0, SEARCH_EARLY_BITS, bit_step,
        (jnp.full((1, tq), INT_MIN, I32), jnp.where(limit < n_sel, n_self, UNSETTLED)))
    g_e = settle(t_e, c_e)
    unresolved = jnp.logical_and(jnp.logical_and(valid, c_e != n_self), g_e >= n_self)

    def finish_search():
        t_l, c_l = lax.fori_loop(SEARCH_EARLY_BITS, 32, bit_step, (t_e, c_e))
        return t_l, c_l, settle(t_l, c_l)

    t_fin, c_fin, c_gt = lax.cond(any_lane(unresolved), finish_search, lambda: (t_e, c_e, g_e))
    is_min = t_fin == INT_MIN
    needs = jnp.logical_and(jnp.logical_and(jnp.logical_not(is_min), c_fin > n_sel), valid)
    j_fast = jnp.where(is_min, 0, BIG_J)

    def tie_search():
        rem = n_self - c_gt

        def mark(kt, carry):
            tie_scr[kt] = jnp.where(key_scr[kt] == t_fin, kt * tk + rowi, BIG_J)
            return carry

        lax.fori_loop(0, nkt, mark, 0)
        j_bits = (n_tiles * tk).bit_length()

        def j_pending(carry):
            i, _, f_cur = carry
            return jnp.logical_and(i < j_bits, any_lane(jnp.logical_and(needs, f_cur != rem)))

        def j_step(carry):
            i, j_cur, f_cur = carry
            cand = j_cur + lax.shift_left(jnp.int32(1), j_bits - 1 - i)
            f = count_ref(tie_scr, lambda col: col < cand[None])
            ok = f <= rem
            return i + 1, jnp.where(ok, cand, j_cur), jnp.where(ok, f, f_cur)

        _, j_slow, _ = lax.while_loop(
            j_pending, j_step, (jnp.int32(0), jnp.zeros((1, tq), I32), jnp.zeros((1, tq), F32)))
        return jnp.where(needs, j_slow, j_fast)

    j_fin = lax.cond(any_lane(needs), tie_search, lambda: j_fast)

    m_scr[...] = jnp.full(m_scr.shape, NEG_BIG, F32)
    acc_scr[...] = jnp.zeros(acc_scr.shape, F32)

    s_bufs = (sa_scr, sb_scr)

    def stage_logits(tile, par):
        t_c = jnp.minimum(tile, nkt - 1)
        off = pl.multiple_of(t_c * tk, tk)
        key = key_scr[t_c]
        sel = jnp.logical_or(key > t_fin, jnp.logical_and(key == t_fin, off + rowi < j_fin))
        sel = jnp.logical_and(sel, tile < nkt)
        bias = jnp.where(sel, 0.0, -jnp.inf)
        bias = jnp.concatenate([bias] * group, axis=1)
        for g in range(A_KV_HEADS):
            s = _dot_t(k_ref[0, pl.ds(off, tk), g * LANES:(g + 1) * LANES], qs_scr[g]) + bias
            s_bufs[par][g] = s
            cm_scr[par, g] = jnp.max(s, axis=0, keepdims=True)

    def attend_tile(tile, par):
        t_c = jnp.minimum(tile, nkt - 1)
        for g in range(A_KV_HEADS):
            m_old = m_scr[g]
            m_new = jnp.maximum(m_old, cm_scr[par, g])
            alpha = jnp.exp2(m_old - m_new)
            p = jnp.exp2(s_bufs[par][g] - m_new).astype(BF16)
            acc_scr[g] = alpha * acc_scr[g] + _dot(vt_ref[0, t_c, g], p)
            m_scr[g] = m_new

    stage_logits(0, 0)

    def attend_pair(i, carry):
        stage_logits(2 * i + 1, 1)
        attend_tile(2 * i, 0)
        stage_logits(2 * i + 2, 0)
        attend_tile(2 * i + 1, 1)
        return carry

    lax.fori_loop(0, (nkt + 1) // 2, attend_pair, 0)
    for g in range(A_KV_HEADS):
        acc = acc_scr[g]
        out_t = acc[:A_HEAD_DIM] / acc[A_HEAD_DIM:A_HEAD_DIM + 1]
        for r in range(group):
            h = g * group + r
            o = jnp.transpose(out_t[:, r * tq:(r + 1) * tq])
            o_ref[0, :, h * LANES:(h + 1) * LANES] = o[:tqb].astype(BF16)


def _attn(q, qi, wi, kb, vb, kiab, *, tq, tk, causal, s_valid, n_sel):
    b, t, aw = q.shape
    tqb = min(tq, t)
    s_pad = kb.shape[1]
    n_tiles = s_pad // tk
    n_heads = aw // A_HEAD_DIM
    group = n_heads // A_KV_HEADS
    vt = jnp.transpose(vb.reshape(b, s_pad, A_KV_HEADS, A_HEAD_DIM), (0, 2, 3, 1))
    vt = jnp.concatenate([vt, jnp.ones((b, A_KV_HEADS, V_EXT_ROWS - A_HEAD_DIM, s_pad), BF16)], axis=2)
    vt = jnp.transpose(vt.reshape(b, A_KV_HEADS, V_EXT_ROWS, n_tiles, tk), (0, 3, 1, 2, 4))
    qspec = lambda w: pl.BlockSpec((1, tqb, w), lambda bi, qi_: (bi, qi_, 0))
    kspec = lambda w: pl.BlockSpec((1, s_pad, w), lambda bi, qi_: (bi, 0, 0))
    kern = functools.partial(_attn_kernel, tq=tq, tqb=tqb, tk=tk, causal=causal, s_valid=s_valid,
                             n_sel=n_sel, n_tiles=n_tiles, group=group)
    return pl.pallas_call(
        kern,
        grid=(b, t // tqb),
        in_specs=[qspec(aw), qspec(qi.shape[2]), qspec(LANES),
                  kspec(kb.shape[2]),
                  pl.BlockSpec((1,) + vt.shape[1:], lambda bi, qi_: (bi, 0, 0, 0, 0)),
                  kspec(kiab.shape[2])],
        out_specs=qspec(aw),
        out_shape=jax.ShapeDtypeStruct((b, t, aw), BF16),
        scratch_shapes=[
            pltpu.VMEM((n_tiles, tk, tq), I32),
            pltpu.VMEM((n_tiles, tk, tq), I32),
            pltpu.VMEM((A_KV_HEADS, group * tq, LANES), BF16),
            pltpu.VMEM((IDX_HEADS // 2 * tq, LANES), BF16),
            pltpu.VMEM((A_KV_HEADS, 1, group * tq), F32),
            pltpu.VMEM((A_KV_HEADS, V_EXT_ROWS, group * tq), F32),
            pltpu.VMEM((A_KV_HEADS, tk, group * tq), F32),
            pltpu.VMEM((A_KV_HEADS, tk, group * tq), F32),
            pltpu.VMEM((2, A_KV_HEADS, 1, group * tq), F32),
            pltpu.VMEM((IDX_HEADS // 2, tk, 2 * tq), F32),
            pltpu.VMEM((IDX_HEADS // 2, tk, 2 * tq), F32),
        ],
        compiler_params=pltpu.CompilerParams(
            dimension_semantics=("arbitrary", "arbitrary"), vmem_limit_bytes=VMEM_LIMIT),
        name="attn",
    )(q, qi, wi, kb, vt, kiab)


def _mlstm_kernel(u_ref, vm_ref, om_ref, ig_ref, fg_ref, c0_ref, n0_ref, m0_ref, cv0_ref,
                  cw_ref, cb_ref, wq_ref, wk_ref, wkt_ref, bg_ref, hg_ref, tri_ref,
                  mo_ref, c_out, n_out, m_out, cv_out,
                  c_scr, n_scr, m_scr, prev_scr, ubuf,
                  *, L, lb, n_heads):
    t = pl.program_id(1)
    nt = pl.num_programs(1)
    hd = M_HEAD_DIM
    pad = SUBLANES

    @pl.when(t == 0)
    def _():
        c_scr[...] = c0_ref[0]
        n_scr[...] = n0_ref[0]
        m_scr[...] = m0_ref[0]
        prev_scr[...] = cv0_ref[0]

    ubuf[0:pad, :] = prev_scr[...]
    ubuf[pad:pad + L, :] = _pad_rows(u_ref[...], L)
    uc = cb_ref[...]
    for i in range(CONV_W):
        uc = uc + ubuf[pad - (CONV_W - 1) + i:pad - (CONV_W - 1) + i + L, :] * cw_ref[i:i + 1, :]
    prev_scr[...] = ubuf[lb:lb + pad, :]
    uh = (uc * _sigmoid(uc)).astype(BF16)

    vm = _pad_rows(vm_ref[...], L)
    om = _pad_rows(om_ref[...], L)
    rowv = lax.broadcasted_iota(I32, (L, LANES), 0) < lb
    li = jnp.where(rowv, _pad_rows(ig_ref[...], L) + bg_ref[0:1, :], NEG_BIG)
    xf = _pad_rows(fg_ref[...], L) + bg_ref[1:2, :]
    lf = jnp.where(rowv, jnp.minimum(xf, 0.0) - jnp.log(1.0 + jnp.exp(-jnp.abs(xf))), 0.0)
    b = jnp.dot(tri_ref[...], lf, precision=lax.Precision.HIGHEST, preferred_element_type=F32)
    c = li - b
    c_t = c.T
    m_prev = m_scr[...]
    inter = b + m_prev
    b_last = b[L - 1:L, :]
    dec = b_last - b + li
    m_new = jnp.maximum(b_last + m_prev, jnp.max(dec, axis=0, keepdims=True))
    wts = jnp.exp(dec - m_new)
    wts_t = wts.T
    sc = jnp.exp(b_last + m_prev - m_new)
    causal = lax.broadcasted_iota(I32, (L, L), 0) >= lax.broadcasted_iota(I32, (L, L), 1)
    k_scale = hd ** -0.5

    for h in range(n_heads):
        sl = slice(h * hd, (h + 1) * hd)
        uh_h = uh[:, sl]
        q_h = _dot(uh_h, wq_ref[h])
        k_h = _dot(uh_h, wk_ref[h]) * k_scale
        kt_h = _dot_t(wkt_ref[h], uh_h) * k_scale
        v_h = vm[:, sl]
        dmat = jnp.where(causal, b[:, h:h + 1] + c_t[h:h + 1, :], -jnp.inf)
        inter_h = inter[:, h:h + 1]
        m_t = jnp.maximum(inter_h, jnp.max(dmat, axis=-1, keepdims=True))
        q_b = q_h.astype(BF16)
        sw = jnp.exp(dmat - m_t) * _dot_t(q_b, k_h.astype(BF16))
        sp = jnp.exp(inter_h - m_t)
        c_h = c_scr[h]
        n_h = n_scr[h:h + 1, :]
        num = sp * _dot(q_b, c_h.astype(BF16)) + _dot(sw.astype(BF16), v_h)
        den = sp * jnp.sum(q_h * n_h, axis=-1, keepdims=True) + jnp.sum(sw, axis=-1, keepdims=True)
        hh = num / jnp.maximum(jnp.abs(den), jnp.exp(-m_t))
        hn = _rms(hh, hg_ref[:, sl])
        out = _sigmoid(om[:, sl]) * hn
        mo_ref[:, sl] = out[:lb].astype(BF16)
        sc_h = sc[:, h:h + 1]
        c_scr[h] = sc_h * c_h + _dot((kt_h * wts_t[h:h + 1, :]).astype(BF16), v_h)
        n_scr[h:h + 1, :] = sc_h * n_h + jnp.sum(k_h * wts[:, h:h + 1], axis=0, keepdims=True)
    m_scr[...] = m_new

    @pl.when(t == nt - 1)
    def _():
        c_out[0] = c_scr[...]
        n_out[0] = n_scr[...]
        m_out[0] = m_scr[...]
        cv_out[0] = prev_scr[...]


def _mlstm(u, vm, om, ig, fg, c0, n0, m0, cv0, cw, cb, wq, wk, wkt, bg, hg, tri, *, b, L, lb):
    m, mw = u.shape
    nt = m // (b * lb)
    n_heads = mw // M_HEAD_DIM
    row = lambda w: pl.BlockSpec((lb, w), lambda bi, ti: (bi * nt + ti, 0))
    const = lambda a: pl.BlockSpec(a.shape, lambda bi, ti: (0,) * a.ndim)
    perb = lambda a: pl.BlockSpec((1,) + a.shape[1:], lambda bi, ti: (bi,) + (0,) * (a.ndim - 1))
    out_shapes = [
        jax.ShapeDtypeStruct((m, mw), BF16),
        jax.ShapeDtypeStruct(c0.shape, F32),
        jax.ShapeDtypeStruct(n0.shape, F32),
        jax.ShapeDtypeStruct(m0.shape, F32),
        jax.ShapeDtypeStruct(cv0.shape, F32),
    ]
    return pl.pallas_call(
        functools.partial(_mlstm_kernel, L=L, lb=lb, n_heads=n_heads),
        grid=(b, nt),
        in_specs=[row(mw), row(mw), row(mw), row(LANES), row(LANES),
                  perb(c0), perb(n0), perb(m0), perb(cv0),
                  const(cw), const(cb), const(wq), const(wk), const(wkt), const(bg), const(hg), const(tri)],
        out_specs=[row(mw), perb(c0), perb(n0), perb(m0), perb(cv0)],
        out_shape=out_shapes,
        scratch_shapes=[
            pltpu.VMEM(c0.shape[1:], F32),
            pltpu.VMEM(n0.shape[1:], F32),
            pltpu.VMEM(m0.shape[1:], F32),
            pltpu.VMEM((SUBLANES, mw), F32),
            pltpu.VMEM((L + SUBLANES, mw), F32),
        ],
        compiler_params=pltpu.CompilerParams(
            dimension_semantics=("arbitrary", "arbitrary"), vmem_limit_bytes=VMEM_LIMIT),
        name="mlstm",
    )(u, vm, om, ig, fg, c0, n0, m0, cv0, cw, cb, wq, wk, wkt, bg, hg, tri)


def _ffn_kernel(x_ref, a_ref, mo_ref, wo_ref, g2_ref, wu_ref, wd_ref, y_ref, hn_scr):
    j = pl.program_id(1)

    @pl.when(j == 0)
    def _():
        aw = a_ref.shape[1]
        h = x_ref[...] + _dot(a_ref[...], wo_ref[:aw, :]) + _dot(mo_ref[...], wo_ref[aw:, :])
        y_ref[...] = h
        hn_scr[...] = _rms(h, g2_ref[...]).astype(BF16)

    f = jnp.maximum(_dot(hn_scr[...], wu_ref[...]), 0.0)
    y_ref[...] += _dot((f * f).astype(BF16), wd_ref[...])


def _ffn(x, a, mo, wo, g2, wu, wd, *, tm, tf):
    m, d = x.shape
    dff = wu.shape[1]
    row = lambda w: pl.BlockSpec((tm, w), lambda i, j: (i, 0))
    const = lambda arr: pl.BlockSpec(arr.shape, lambda i, j: (0, 0), pipeline_mode=pl.Buffered(1))
    return pl.pallas_call(
        _ffn_kernel,
        grid=(m // tm, dff // tf),
        in_specs=[row(d), row(a.shape[1]), row(mo.shape[1]), const(wo), const(g2),
                  pl.BlockSpec((d, tf), lambda i, j: (0, j)),
                  pl.BlockSpec((tf, d), lambda i, j: (j, 0))],
        out_specs=row(d),
        out_shape=jax.ShapeDtypeStruct((m, d), F32),
        scratch_shapes=[pltpu.VMEM((tm, d), BF16)],
        compiler_params=pltpu.CompilerParams(
            dimension_semantics=("arbitrary", "arbitrary"), vmem_limit_bytes=VMEM_LIMIT),
        name="ffn",
    )(x, a, mo, wo, g2, wu, wd)


def _rope_tables(pos, rot, width):
    half = rot // 2
    inv_freq = ROPE_THETA ** (-jnp.arange(half, dtype=F32) / half)
    ang = pos.astype(F32)[:, None] * inv_freq[None, :]
    cos, sin = jnp.cos(ang), jnp.sin(ang)
    n = pos.shape[0]
    rest1 = jnp.ones((n, width - rot), F32)
    rest0 = jnp.zeros((n, width - rot), F32)
    z = jnp.zeros_like(sin)
    c = jnp.concatenate([cos, cos, rest1], axis=1)
    sa = jnp.concatenate([-sin, z, rest0], axis=1)
    sb = jnp.concatenate([z, sin, rest0], axis=1)
    rep = LANES // width
    return jnp.concatenate([jnp.tile(c, (1, rep)), jnp.tile(sa, (1, rep)), jnp.tile(sb, (1, rep))], axis=1)


def _pad_cols(w, n):
    return jnp.pad(w, ((0, 0), (0, n - w.shape[1])))


def _tile_plan(n_rows, t, prompt):
    if prompt:
        return dict(tm_proj=256, tm_ffn=512, tf=1024, tq=LANES, tk=512, L=128, lb=128)
    return dict(tm_proj=n_rows, tm_ffn=n_rows, tf=512, tq=LANES, tk=384, L=128, lb=t)


def _layer(x, pos_rows, mode, state, params, splits):
    (norm1_g, w_in, q_norm_g, k_norm_g, conv_w, conv_b, wq_m, wk_m, b_igate, b_fgate,
     hnorm_g, w_out, norm2_g, w_up, w_down) = params
    bsz, t, d = x.shape
    m = bsz * t
    plan = _tile_plan(m, t, mode == "prompt")
    tm_proj, tm_ffn, tf, tq, tk, L, lb = (plan[k] for k in ("tm_proj", "tm_ffn", "tf", "tq", "tk", "L", "lb"))
    x2 = x.reshape(m, d)
    offs = np.cumsum((0,) + tuple(splits))
    col = lambda i: w_in[:, int(offs[i]):int(offs[i + 1])]
    aw, mw = splits[0], splits[6]
    n_q_heads = aw // A_HEAD_DIM
    n_m_heads = mw // M_HEAD_DIM
    wa = jnp.concatenate(
        [col(0), col(1), col(2), col(3), col(4), col(4),
         _pad_cols(col(5), LANES), _pad_cols(col(9), LANES), _pad_cols(col(10), LANES)], axis=1).astype(BF16)
    wb = jnp.concatenate([col(6), col(7), col(8)], axis=1).astype(BF16)
    ra = _rope_tables(pos_rows, A_HEAD_DIM // 4, A_HEAD_DIM)
    ri = _rope_tables(pos_rows, IDX_DIM // 4, IDX_DIM)

    (q, k, v, ki, kb, vb, kiab, qi, wi, ig, fg, u, vm, om) = _proj(
        x2, norm1_g[None], wa, wb, q_norm_g[None], k_norm_g[None], ra, ri,
        tm=tm_proj, n_q_heads=n_q_heads, m_width=mw)

    kvw = A_KV_HEADS * A_HEAD_DIM
    r3 = lambda a_: a_.reshape(bsz, t, a_.shape[1])
    if mode == "prompt":
        c0 = jnp.zeros((bsz, n_m_heads, M_HEAD_DIM, M_HEAD_DIM), F32)
        n0 = jnp.zeros((bsz, n_m_heads, M_HEAD_DIM), F32)
        m0 = jnp.zeros((bsz, 1, LANES), F32)
        cv0 = jnp.zeros((bsz, SUBLANES, mw), F32)
        n_sel = min(TOPK_MAX, t // 4)
        a = _attn(r3(q), r3(qi), r3(wi), r3(kb), r3(vb), r3(kiab),
                  tq=tq, tk=tk, causal=True, s_valid=t, n_sel=n_sel)
    else:
        ck, cv, ckidx, s_c, s_n, s_m, s_conv = state
        p = ck.shape[1]
        s_valid = p + t
        s_pad = -(-s_valid // tk) * tk
        padk = lambda a_: jnp.pad(a_, ((0, 0), (0, s_pad - s_valid), (0, 0)))
        k_all = padk(jnp.concatenate([ck.reshape(bsz, p, kvw).astype(BF16), r3(kb)], axis=1))
        v_all = padk(jnp.concatenate([cv.reshape(bsz, p, kvw).astype(BF16), r3(vb)], axis=1))
        cki = ckidx.astype(BF16)
        zki = jnp.zeros_like(cki)
        ki_c = jnp.concatenate([cki, zki, zki, cki], axis=2)
        ki_all = padk(jnp.concatenate([ki_c, r3(kiab)], axis=1))
        c0, n0 = s_c.astype(F32), s_n.astype(F32)
        m0 = jnp.pad(s_m.astype(F32), ((0, 0), (0, LANES - n_m_heads)))[:, None, :]
        cv0 = jnp.pad(s_conv.astype(F32), ((0, 0), (SUBLANES - (CONV_W - 1), 0), (0, 0)))
        n_sel = min(TOPK_MAX, s_valid // 4)
        a = _attn(r3(q), r3(qi), r3(wi), k_all, v_all, ki_all,
                  tq=tq, tk=tk, causal=False, s_valid=s_valid, n_sel=n_sel)

    bg = jnp.stack([_pad_cols(b_igate[None], LANES)[0], _pad_cols(b_fgate[None], LANES)[0]])
    tri = (np.arange(L)[:, None] >= np.arange(L)[None, :]).astype(np.float32)
    mo, c_new, n_new, m_new, cv_new = _mlstm(
        u, vm, om, ig, fg, c0, n0, m0, cv0,
        conv_w, conv_b[None], wq_m.astype(BF16), wk_m.astype(BF16),
        jnp.swapaxes(wk_m, 1, 2).astype(BF16), bg, hnorm_g[None], jnp.asarray(tri),
        b=bsz, L=L, lb=lb)

    y = _ffn(x2, a.reshape(m, aw), mo, w_out.astype(BF16), norm2_g[None],
             w_up.astype(BF16), w_down.astype(BF16), tm=tm_ffn, tf=tf)
    new_state = (
        k.reshape(bsz, t, A_KV_HEADS, A_HEAD_DIM), v.reshape(bsz, t, A_KV_HEADS, A_HEAD_DIM),
        ki.reshape(bsz, t, IDX_DIM), c_new, n_new, m_new[:, 0, :n_m_heads],
        cv_new[:, SUBLANES - (CONV_W - 1):, :])
    return y.reshape(bsz, t, d), new_state


def kernel(x_prompt, x_sample, cache_k, cache_v, cache_kidx, state_mlstm_C, state_mlstm_n, state_mlstm_m,
           state_conv, norm1_g, w_in, q_norm_g, k_norm_g, conv_w, conv_b, wq_m, wk_m, b_igate, b_fgate,
           hnorm_g, w_out, norm2_g, w_up, w_down):
    depth = w_in.shape[0]
    d = x_prompt.shape[-1]
    mix = w_out.shape[1]
    aw = mix // 2
    mw = mix - aw
    kvw = A_KV_HEADS * A_HEAD_DIM
    n_m_heads = mw // M_HEAD_DIM
    splits = (aw, kvw, kvw, IDX_HEADS * IDX_DIM, IDX_DIM, IDX_HEADS, mw, mw, mw, n_m_heads, n_m_heads)
    t_p = x_prompt.shape[1]
    b_s, t_s = x_sample.shape[0], x_sample.shape[1]
    past = cache_k.shape[2]

    y_p, y_s = x_prompt, x_sample
    new_p, new_s = [], []
    for l in range(depth):
        params = (norm1_g[l], w_in[l], q_norm_g[l], k_norm_g[l], conv_w[l], conv_b[l], wq_m[l], wk_m[l],
                  b_igate[l], b_fgate[l], hnorm_g[l], w_out[l], norm2_g[l], w_up[l], w_down[l])
        y_p, st_p = _layer(
            y_p, jnp.arange(t_p), "prompt", None, params, splits)
        y_s, st_s = _layer(
            y_s, past + (jnp.arange(b_s * t_s) % t_s), "sample",
            (cache_k[l], cache_v[l], cache_kidx[l], state_mlstm_C[l], state_mlstm_n[l],
             state_mlstm_m[l], state_conv[l]), params, splits)
        new_p.append(st_p)
        new_s.append(st_s)

    stack = lambda states, i: jnp.stack([s[i] for s in states])
    return (y_p, y_s,
            *[stack(new_p, i) for i in range(7)],
            *[stack(new_s, i) for i in range(7)])
```

```python
import functools

import jax
import jax.numpy as jnp
import numpy as np
from jax import lax
from jax.experimental import pallas as pl
from jax.experimental.pallas import tpu as pltpu

F32 = jnp.float32
BF16 = jnp.bfloat16
I32 = jnp.int32

EPS = 1e-6
ROPE_THETA = 500000.0
CHUNK = 64
CHUNK_LOG2 = 6
TOPK_MAX = 256
A_HEAD_DIM = 128
A_KV_HEADS = 2
IDX_HEADS = 8
IDX_DIM = 64
M_HEAD_DIM = 128
CONV_W = 4

LANES = 128
SUBLANES = 8
VMEM_LIMIT = 56 * 1024 * 1024
NEG_BIG = -1e30
LOG2_E = 1.4426950408889634
INT_MIN = -(2 ** 31)
BIG_J = 2 ** 30
UNSETTLED = 1e9
SEARCH_EARLY_BITS = 26


def _rms(x, g):
    r = lax.rsqrt(jnp.mean(x * x, axis=-1, keepdims=True) + EPS)
    return x * r * g


def _rope(t, c, sa, sb, half):
    return t * c + pltpu.roll(t, LANES - half, 1) * sa + pltpu.roll(t, half, 1) * sb


def _sigmoid(x):
    return 1.0 / (1.0 + jnp.exp(-x))


def _dot(a, b):
    return jnp.dot(a, b, preferred_element_type=F32)


def _dot_t(a, b):
    return lax.dot_general(a, b, (((1,), (1,)), ((), ())), preferred_element_type=F32)


def _proj_kernel(x_ref, g1_ref, wa_ref, wb_ref, qg_ref, kg_ref, ra_ref, ri_ref,
                 q_out, k_out, v_out, ki_out, kb_out, vb_out, kiab_out, qi_out,
                 wi_out, ig_out, fg_out, u_out, vm_out, om_out, *, n_q_heads):
    xn = _rms(x_ref[...], g1_ref[...]).astype(BF16)
    ra = ra_ref[...]
    ca, saa, sba = ra[:, :LANES], ra[:, LANES:2 * LANES], ra[:, 2 * LANES:]
    ri = ri_ref[...]
    ci, sai, sbi = ri[:, :LANES], ri[:, LANES:2 * LANES], ri[:, 2 * LANES:]
    qg = qg_ref[...]
    kg = kg_ref[...]
    aw = n_q_heads * A_HEAD_DIM
    kvw = A_KV_HEADS * A_HEAD_DIM
    q_scale = A_HEAD_DIM ** -0.5 * LOG2_E
    i_scale = IDX_DIM ** -0.5

    for c in range(0, aw, 512):
        z = _dot(xn, wa_ref[:, c:c + 512])
        for j in range(0, 512, LANES):
            qh = _rope(_rms(z[:, j:j + LANES], qg), ca, saa, sba, A_HEAD_DIM // 8) * q_scale
            q_out[:, c + j:c + j + LANES] = qh.astype(BF16)
    off = aw
    z = _dot(xn, wa_ref[:, off:off + 2 * kvw])
    for j in range(0, kvw, LANES):
        kh = _rope(_rms(z[:, j:j + LANES], kg), ca, saa, sba, A_HEAD_DIM // 8)
        k_out[:, j:j + LANES] = kh
        kb_out[:, j:j + LANES] = kh.astype(BF16)
    vv = z[:, kvw:]
    v_out[...] = vv
    vb_out[...] = vv.astype(BF16)
    off += 2 * kvw
    iw = IDX_HEADS * IDX_DIM
    z = _dot(xn, wa_ref[:, off:off + iw])
    for j in range(0, iw, LANES):
        qi = _rope(z[:, j:j + LANES], ci, sai, sbi, IDX_DIM // 8) * i_scale
        qi_out[:, j:j + LANES] = qi.astype(BF16)
    off += iw
    z = _dot(xn, wa_ref[:, off:off + 4 * LANES])
    ki2 = _rope(z[:, :LANES], ci, sai, sbi, IDX_DIM // 8)
    ki_out[...] = ki2[:, :IDX_DIM]
    lane = lax.broadcasted_iota(I32, (1, LANES), 1)
    lo = lane < IDX_DIM
    kiab_out[:, :LANES] = jnp.where(lo, ki2, 0.0).astype(BF16)
    kiab_out[:, LANES:] = jnp.where(lo, 0.0, ki2).astype(BF16)
    wi_out[...] = z[:, LANES:2 * LANES]
    ig_out[...] = z[:, 2 * LANES:3 * LANES]
    fg_out[...] = z[:, 3 * LANES:]
    mw = u_out.shape[1]
    for c in range(0, mw, 512):
        u_out[:, c:c + 512] = _dot(xn, wb_ref[:, c:c + 512])
        vm_out[:, c:c + 512] = _dot(xn, wb_ref[:, mw + c:mw + c + 512]).astype(BF16)
        om_out[:, c:c + 512] = _dot(xn, wb_ref[:, 2 * mw + c:2 * mw + c + 512])


def _proj(x, g1, wa, wb, qg, kg, ra, ri, *, tm, n_q_heads, m_width):
    m, d = x.shape
    n_tab = ra.shape[0] // tm
    aw = n_q_heads * A_HEAD_DIM
    kvw = A_KV_HEADS * A_HEAD_DIM
    iw = IDX_HEADS * IDX_DIM
    row = lambda w: pl.BlockSpec((tm, w), lambda i: (i, 0))
    const = lambda a: pl.BlockSpec(a.shape, lambda i: (0, 0), pipeline_mode=pl.Buffered(1))
    tab = pl.BlockSpec((tm, 3 * LANES), lambda i: (i % n_tab, 0))
    outs = [
        (aw, BF16), (kvw, F32), (kvw, F32), (IDX_DIM, F32), (kvw, BF16), (kvw, BF16),
        (2 * LANES, BF16), (iw, BF16), (LANES, F32), (LANES, F32), (LANES, F32),
        (m_width, F32), (m_width, BF16), (m_width, F32),
    ]
    return pl.pallas_call(
        functools.partial(_proj_kernel, n_q_heads=n_q_heads),
        grid=(m // tm,),
        in_specs=[row(d), const(g1), const(wa), const(wb), const(qg), const(kg), tab, tab],
        out_specs=[row(w) for w, _ in outs],
        out_shape=[jax.ShapeDtypeStruct((m, w), dt) for w, dt in outs],
        compiler_params=pltpu.CompilerParams(
            dimension_semantics=("arbitrary",), vmem_limit_bytes=VMEM_LIMIT),
        name="proj",
    )(x, g1, wa, wb, qg, kg, ra, ri)


def _pad_rows(x, n):
    if x.shape[0] == n:
        return x
    return jnp.concatenate([x, jnp.zeros((n - x.shape[0],) + x.shape[1:], x.dtype)], axis=0)


V_EXT_ROWS = A_HEAD_DIM + 16


def _attn_kernel(q_ref, qi_ref, wi_ref, k_ref, vt_ref, ki_ref, o_ref,
                 key_scr, tie_scr, qs_scr, qis_scr, m_scr, acc_scr, sa_scr, sb_scr, cm_scr, da_scr, db_scr,
                 *, tq, tqb, tk, causal, s_valid, n_sel, n_tiles, group):
    start = pl.program_id(1) * tqb
    nr = tk // SUBLANES
    n_acc = 8
    lane = lax.broadcasted_iota(I32, (1, tq), 1)
    rowi = lax.broadcasted_iota(I32, (tk, 1), 0)
    if causal:
        limit = (lax.shift_right_logical(start + lane, CHUNK_LOG2) + 1) * CHUNK
        nkt = jnp.minimum((start + tqb + tk - 1) // tk, n_tiles)
    else:
        limit = jnp.full((1, tq), s_valid, I32)
        nkt = n_tiles

    q = _pad_rows(q_ref[0], tq)
    for g in range(A_KV_HEADS):
        for r in range(group):
            h = g * group + r
            qs_scr[g, r * tq:(r + 1) * tq, :] = q[:, h * LANES:(h + 1) * LANES]
    qi = _pad_rows(qi_ref[0], tq)
    for j in range(IDX_HEADS // 2):
        qis_scr[j * tq:(j + 1) * tq, :] = qi[:, j * LANES:(j + 1) * LANES]
    w_t = jnp.transpose(_pad_rows(wi_ref[0], tq))[:IDX_HEADS] * (IDX_HEADS ** -0.5)

    d_bufs = (da_scr, db_scr)

    def stage_scores(tile, par):
        t_c = jnp.minimum(tile, nkt - 1)
        off = pl.multiple_of(t_c * tk, tk)
        kiab = ki_ref[0, pl.ds(off, tk), :]
        for half in range(IDX_HEADS // 4):
            qh = qis_scr[half * 2 * tq:(half + 1) * 2 * tq, :]
            for odd in range(2):
                d_bufs[par][2 * half + odd] = _dot_t(kiab[:, odd * LANES:(odd + 1) * LANES], qh)

    def finish_scores(tile, par):
        t_c = jnp.minimum(tile, nkt - 1)
        acc = jnp.zeros((tk, tq), F32)
        for half in range(IDX_HEADS // 4):
            for odd in range(2):
                d = d_bufs[par][2 * half + odd]
                for jj in range(2):
                    h = 2 * (2 * half + jj) + odd
                    acc = acc + w_t[h:h + 1, :] * jnp.maximum(d[:, jj * tq:(jj + 1) * tq], 0.0)
        bits = lax.bitcast_convert_type(acc, I32)
        key = bits ^ (lax.shift_right_arithmetic(bits, 31) & 0x7FFFFFFF)
        key = jnp.where(key == -1, 0, key)
        key_scr[t_c] = jnp.where(t_c * tk + rowi < limit, key, INT_MIN)

    stage_scores(0, 0)

    def score_pair(i, carry):
        stage_scores(2 * i + 1, 1)
        finish_scores(2 * i, 0)
        stage_scores(2 * i + 2, 0)
        finish_scores(2 * i + 1, 1)
        return carry

    lax.fori_loop(0, (nkt + 1) // 2, score_pair, 0)

    def count_ref(ref, pred):
        def body(kt, acc):
            hit = pred(ref[kt].reshape(nr, SUBLANES, tq))
            ones = jnp.where(hit, 1.0, 0.0).reshape(nr // n_acc, n_acc, SUBLANES, tq)
            return acc + jnp.sum(ones, axis=0)
        acc = lax.fori_loop(0, nkt, body, jnp.zeros((n_acc, SUBLANES, tq), F32))
        return jnp.sum(jnp.sum(acc, axis=0), axis=0, keepdims=True)

    def bit_step(i, carry):
        t_cur, c_cur = carry
        cand = t_cur + lax.shift_left(jnp.int32(1), 31 - i)
        cnt = count_ref(key_scr, lambda key: key >= cand[None])
        ok = cnt >= n_sel
        return jnp.where(ok, cand, t_cur), jnp.where(ok, cnt, c_cur)

    def any_lane(mask):
        return jnp.max(jnp.where(mask, 1.0, 0.0)) > 0.0

    valid = lane < tqb
    n_self = float(n_sel)

    def settle(t_cur, c_cur):
        pend = jnp.logical_and(valid, c_cur != n_self)
        return lax.cond(any_lane(pend), lambda: count_ref(key_scr, lambda key: key > t_cur[None]),
                        lambda: jnp.zeros((1, tq), F32))

    t_e, c_e = lax.fori_loop(
        0, SEARCH_EARLY_BITS, bit_step,
        (jnp.full((1, tq), INT_MIN, I32), jnp.where(limit < n_sel, n_self, UNSETTLED)))
    g_e = settle(t_e, c_e)
    unresolved = jnp.logical_and(jnp.logical_and(valid, c_e != n_self), g_e >= n_self)

    def finish_search():
        t_l, c_l = lax.fori_loop(SEARCH_EARLY_BITS, 32, bit_step, (t_e, c_e))
        return t_l, c_l, settle(t_l, c_l)

    t_fin, c_fin, c_gt = lax.cond(any_lane(unresolved), finish_search, lambda: (t_e, c_e, g_e))
    is_min = t_fin == INT_MIN
    needs = jnp.logical_and(jnp.logical_and(jnp.logical_not(is_min), c_fin > n_sel), valid)
    j_fast = jnp.where(is_min, 0, BIG_J)

    def tie_search():
        rem = n_self - c_gt

        def mark(kt, carry):
            tie_scr[kt] = jnp.where(key_scr[kt] == t_fin, kt * tk + rowi, BIG_J)
            return carry

        lax.fori_loop(0, nkt, mark, 0)
        j_bits = (n_tiles * tk).bit_length()

        def j_pending(carry):
            i, _, f_cur = carry
            return jnp.logical_and(i < j_bits, any_lane(jnp.logical_and(needs, f_cur != rem)))

        def j_step(carry):
            i, j_cur, f_cur = carry
            cand = j_cur + lax.shift_left(jnp.int32(1), j_bits - 1 - i)
            f = count_ref(tie_scr, lambda col: col < cand[None])
            ok = f <= rem
            return i + 1, jnp.where(ok, cand, j_cur), jnp.where(ok, f, f_cur)

        _, j_slow, _ = lax.while_loop(
            j_pending, j_step, (jnp.int32(0), jnp.zeros((1, tq), I32), jnp.zeros((1, tq), F32)))
        return jnp.where(needs, j_slow, j_fast)

    j_fin = lax.cond(any_lane(needs), tie_search, lambda: j_fast)

    m_scr[...] = jnp.full(m_scr.shape, NEG_BIG, F32)
    acc_scr[...] = jnp.zeros(acc_scr.shape, F32)

    s_bufs = (sa_scr, sb_scr)

    def stage_logits(tile, par):
        t_c = jnp.minimum(tile, nkt - 1)
        off = pl.multiple_of(t_c * tk, tk)
        key = key_scr[t_c]
        sel = jnp.logical_or(key > t_fin, jnp.logical_and(key == t_fin, off + rowi < j_fin))
        sel = jnp.logical_and(sel, tile < nkt)
        bias = jnp.where(sel, 0.0, -jnp.inf)
        bias = jnp.concatenate([bias] * group, axis=1)
        for g in range(A_KV_HEADS):
            s = _dot_t(k_ref[0, pl.ds(off, tk), g * LANES:(g + 1) * LANES], qs_scr[g]) + bias
            s_bufs[par][g] = s
            cm_scr[par, g] = jnp.max(s, axis=0, keepdims=True)

    def attend_tile(tile, par):
        t_c = jnp.minimum(tile, nkt - 1)
        for g in range(A_KV_HEADS):
            m_old = m_scr[g]
            m_new = jnp.maximum(m_old, cm_scr[par, g])
            alpha = jnp.exp2(m_old - m_new)
            p = jnp.exp2(s_bufs[par][g] - m_new).astype(BF16)
            acc_scr[g] = alpha * acc_scr[g] + _dot(vt_ref[0, t_c, g], p)
            m_scr[g] = m_new

    stage_logits(0, 0)

    def attend_pair(i, carry):
        stage_logits(2 * i + 1, 1)
        attend_tile(2 * i, 0)
        stage_logits(2 * i + 2, 0)
        attend_tile(2 * i + 1, 1)
        return carry

    lax.fori_loop(0, (nkt + 1) // 2, attend_pair, 0)
    for g in range(A_KV_HEADS):
        acc = acc_scr[g]
        out_t = acc[:A_HEAD_DIM] / acc[A_HEAD_DIM:A_HEAD_DIM + 1]
        for r in range(group):
            h = g * group + r
            o = jnp.transpose(out_t[:, r * tq:(r + 1) * tq])
            o_ref[0, :, h * LANES:(h + 1) * LANES] = o[:tqb].astype(BF16)


def _attn(q, qi, wi, kb, vb, kiab, *, tq, tk, causal, s_valid, n_sel):
    b, t, aw = q.shape
    tqb = min(tq, t)
    s_pad = kb.shape[1]
    n_tiles = s_pad // tk
    n_heads = aw // A_HEAD_DIM
    group = n_heads // A_KV_HEADS
    vt = jnp.transpose(vb.reshape(b, s_pad, A_KV_HEADS, A_HEAD_DIM), (0, 2, 3, 1))
    vt = jnp.concatenate([vt, jnp.ones((b, A_KV_HEADS, V_EXT_ROWS - A_HEAD_DIM, s_pad), BF16)], axis=2)
    vt = jnp.transpose(vt.reshape(b, A_KV_HEADS, V_EXT_ROWS, n_tiles, tk), (0, 3, 1, 2, 4))
    qspec = lambda w: pl.BlockSpec((1, tqb, w), lambda bi, qi_: (bi, qi_, 0))
    kspec = lambda w: pl.BlockSpec((1, s_pad, w), lambda bi, qi_: (bi, 0, 0))
    kern = functools.partial(_attn_kernel, tq=tq, tqb=tqb, tk=tk, causal=causal, s_valid=s_valid,
                             n_sel=n_sel, n_tiles=n_tiles, group=group)
    return pl.pallas_call(
        kern,
        grid=(b, t // tqb),
        in_specs=[qspec(aw), qspec(qi.shape[2]), qspec(LANES),
                  kspec(kb.shape[2]),
                  pl.BlockSpec((1,) + vt.shape[1:], lambda bi, qi_: (bi, 0, 0, 0, 0)),
                  kspec(kiab.shape[2])],
        out_specs=qspec(aw),
        out_shape=jax.ShapeDtypeStruct((b, t, aw), BF16),
        scratch_shapes=[
            pltpu.VMEM((n_tiles, tk, tq), I32),
            pltpu.VMEM((n_tiles, tk, tq), I32),
            pltpu.VMEM((A_KV_HEADS, group * tq, LANES), BF16),
            pltpu.VMEM((IDX_HEADS // 2 * tq, LANES), BF16),
            pltpu.VMEM((A_KV_HEADS, 1, group * tq), F32),
            pltpu.VMEM((A_KV_HEADS, V_EXT_ROWS, group * tq), F32),
            pltpu.VMEM((A_KV_HEADS, tk, group * tq), F32),
            pltpu.VMEM((A_KV_HEADS, tk, group * tq), F32),
            pltpu.VMEM((2, A_KV_HEADS, 1, group * tq), F32),
            pltpu.VMEM((IDX_HEADS // 2, tk, 2 * tq), F32),
            pltpu.VMEM((IDX_HEADS // 2, tk, 2 * tq), F32),
        ],
        compiler_params=pltpu.CompilerParams(
            dimension_semantics=("arbitrary", "arbitrary"), vmem_limit_bytes=VMEM_LIMIT),
        name="attn",
    )(q, qi, wi, kb, vt, kiab)


def _mlstm_kernel(u_ref, vm_ref, om_ref, ig_ref, fg_ref, c0_ref, n0_ref, m0_ref, cv0_ref,
                  cw_ref, cb_ref, wq_ref, wk_ref, wkt_ref, bg_ref, hg_ref, tri_ref,
                  mo_ref, c_out, n_out, m_out, cv_out,
                  c_scr, n_scr, m_scr, prev_scr, ubuf,
                  *, L, lb, n_heads):
    t = pl.program_id(1)
    nt = pl.num_programs(1)
    hd = M_HEAD_DIM
    pad = SUBLANES

    @pl.when(t == 0)
    def _():
        c_scr[...] = c0_ref[0]
        n_scr[...] = n0_ref[0]
        m_scr[...] = m0_ref[0]
        prev_scr[...] = cv0_ref[0]

    ubuf[0:pad, :] = prev_scr[...]
    ubuf[pad:pad + L, :] = _pad_rows(u_ref[...], L)
    uc = cb_ref[...]
    for i in range(CONV_W):
        uc = uc + ubuf[pad - (CONV_W - 1) + i:pad - (CONV_W - 1) + i + L, :] * cw_ref[i:i + 1, :]
    prev_scr[...] = ubuf[lb:lb + pad, :]
    uh = (uc * _sigmoid(uc)).astype(BF16)

    vm = _pad_rows(vm_ref[...], L)
    om = _pad_rows(om_ref[...], L)
    rowv = lax.broadcasted_iota(I32, (L, LANES), 0) < lb
    li = jnp.where(rowv, _pad_rows(ig_ref[...], L) + bg_ref[0:1, :], NEG_BIG)
    xf = _pad_rows(fg_ref[...], L) + bg_ref[1:2, :]
    lf = jnp.where(rowv, jnp.minimum(xf, 0.0) - jnp.log(1.0 + jnp.exp(-jnp.abs(xf))), 0.0)
    b = jnp.dot(tri_ref[...], lf, precision=lax.Precision.HIGHEST, preferred_element_type=F32)
    c = li - b
    c_t = c.T
    m_prev = m_scr[...]
    inter = b + m_prev
    b_last = b[L - 1:L, :]
    dec = b_last - b + li
    m_new = jnp.maximum(b_last + m_prev, jnp.max(dec, axis=0, keepdims=True))
    wts = jnp.exp(dec - m_new)
    wts_t = wts.T
    sc = jnp.exp(b_last + m_prev - m_new)
    causal = lax.broadcasted_iota(I32, (L, L), 0) >= lax.broadcasted_iota(I32, (L, L), 1)
    k_scale = hd ** -0.5

    heads = range(n_heads)
    sls = [slice(h * hd, (h + 1) * hd) for h in heads]
    q_l = [_dot(uh[:, sls[h]], wq_ref[h]) for h in heads]
    k_l = [_dot(uh[:, sls[h]], wk_ref[h]) * k_scale for h in heads]
    kt_l = [_dot_t(wkt_ref[h], uh[:, sls[h]]) * k_scale for h in heads]
    qb_l = [q.astype(BF16) for q in q_l]
    s_l = [_dot_t(qb_l[h], k_l[h].astype(BF16)) for h in heads]
    qc_l = [_dot(qb_l[h], c_scr[h].astype(BF16)) for h in heads]
    d_l = [jnp.where(causal, b[:, h:h + 1] + c_t[h:h + 1, :], -jnp.inf) for h in heads]
    mt_l = [jnp.maximum(inter[:, h:h + 1], jnp.max(d_l[h], axis=-1, keepdims=True)) for h in heads]
    sw_l = [jnp.exp(d_l[h] - mt_l[h]) * s_l[h] for h in heads]
    sp_l = [jnp.exp(inter[:, h:h + 1] - mt_l[h]) for h in heads]
    num_l = [sp_l[h] * qc_l[h] + _dot(sw_l[h].astype(BF16), vm[:, sls[h]]) for h in heads]
    den_l = [sp_l[h] * jnp.sum(q_l[h] * n_scr[h:h + 1, :], axis=-1, keepdims=True)
             + jnp.sum(sw_l[h], axis=-1, keepdims=True) for h in heads]
    kv_l = [_dot((kt_l[h] * wts_t[h:h + 1, :]).astype(BF16), vm[:, sls[h]]) for h in heads]
    hh_l = [num_l[h] / jnp.maximum(jnp.abs(den_l[h]), jnp.exp(-mt_l[h])) for h in heads]
    r_l = [lax.rsqrt(jnp.mean(hh * hh, axis=-1, keepdims=True) + EPS) for hh in hh_l]
    og_l = [_sigmoid(om[:, sls[h]]) for h in heads]
    for h in heads:
        mo_ref[:, sls[h]] = (og_l[h] * (hh_l[h] * r_l[h] * hg_ref[:, sls[h]]))[:lb].astype(BF16)
    for h in heads:
        sc_h = sc[:, h:h + 1]
        c_scr[h] = sc_h * c_scr[h] + kv_l[h]
        n_scr[h:h + 1, :] = (sc_h * n_scr[h:h + 1, :]
                             + jnp.sum(k_l[h] * wts[:, h:h + 1], axis=0, keepdims=True))
    m_scr[...] = m_new

    @pl.when(t == nt - 1)
    def _():
        c_out[0] = c_scr[...]
        n_out[0] = n_scr[...]
        m_out[0] = m_scr[...]
        cv_out[0] = prev_scr[...]


def _mlstm(u, vm, om, ig, fg, c0, n0, m0, cv0, cw, cb, wq, wk, wkt, bg, hg, tri, *, b, L, lb):
    m, mw = u.shape
    nt = m // (b * lb)
    n_heads = mw // M_HEAD_DIM
    row = lambda w: pl.BlockSpec((lb, w), lambda bi, ti: (bi * nt + ti, 0))
    const = lambda a: pl.BlockSpec(a.shape, lambda bi, ti: (0,) * a.ndim)
    perb = lambda a: pl.BlockSpec((1,) + a.shape[1:], lambda bi, ti: (bi,) + (0,) * (a.ndim - 1))
    out_shapes = [
        jax.ShapeDtypeStruct((m, mw), BF16),
        jax.ShapeDtypeStruct(c0.shape, F32),
        jax.ShapeDtypeStruct(n0.shape, F32),
        jax.ShapeDtypeStruct(m0.shape, F32),
        jax.ShapeDtypeStruct(cv0.shape, F32),
    ]
    return pl.pallas_call(
        functools.partial(_mlstm_kernel, L=L, lb=lb, n_heads=n_heads),
        grid=(b, nt),
        in_specs=[row(mw), row(mw), row(mw), row(LANES), row(LANES),
                  perb(c0), perb(n0), perb(m0), perb(cv0),
                  const(cw), const(cb), const(wq), const(wk), const(wkt), const(bg), const(hg), const(tri)],
        out_specs=[row(mw), perb(c0), perb(n0), perb(m0), perb(cv0)],
        out_shape=out_shapes,
        scratch_shapes=[
            pltpu.VMEM(c0.shape[1:], F32),
            pltpu.VMEM(n0.shape[1:], F32),
            pltpu.VMEM(m0.shape[1:], F32),
            pltpu.VMEM((SUBLANES, mw), F32),
            pltpu.VMEM((L + SUBLANES, mw), F32),
        ],
        compiler_params=pltpu.CompilerParams(
            dimension_semantics=("arbitrary", "arbitrary"), vmem_limit_bytes=VMEM_LIMIT),
        name="mlstm",
    )(u, vm, om, ig, fg, c0, n0, m0, cv0, cw, cb, wq, wk, wkt, bg, hg, tri)


def _ffn_kernel(x_ref, a_ref, mo_ref, wo_ref, g2_ref, wu_ref, wd_ref, y_ref, hn_scr):
    j = pl.program_id(1)

    @pl.when(j == 0)
    def _():
        aw = a_ref.shape[1]
        h = x_ref[...] + _dot(a_ref[...], wo_ref[:aw, :]) + _dot(mo_ref[...], wo_ref[aw:, :])
        y_ref[...] = h
        hn_scr[...] = _rms(h, g2_ref[...]).astype(BF16)

    f = jnp.maximum(_dot(hn_scr[...], wu_ref[...]), 0.0)
    y_ref[...] += _dot((f * f).astype(BF16), wd_ref[...])


def _ffn(x, a, mo, wo, g2, wu, wd, *, tm, tf):
    m, d = x.shape
    dff = wu.shape[1]
    row = lambda w: pl.BlockSpec((tm, w), lambda i, j: (i, 0))
    const = lambda arr: pl.BlockSpec(arr.shape, lambda i, j: (0, 0), pipeline_mode=pl.Buffered(1))
    return pl.pallas_call(
        _ffn_kernel,
        grid=(m // tm, dff // tf),
        in_specs=[row(d), row(a.shape[1]), row(mo.shape[1]), const(wo), const(g2),
                  pl.BlockSpec((d, tf), lambda i, j: (0, j)),
                  pl.BlockSpec((tf, d), lambda i, j: (j, 0))],
        out_specs=row(d),
        out_shape=jax.ShapeDtypeStruct((m, d), F32),
        scratch_shapes=[pltpu.VMEM((tm, d), BF16)],
        compiler_params=pltpu.CompilerParams(
            dimension_semantics=("arbitrary", "arbitrary"), vmem_limit_bytes=VMEM_LIMIT),
        name="ffn",
    )(x, a, mo, wo, g2, wu, wd)


def _rope_tables(pos, rot, width):
    half = rot // 2
    inv_freq = ROPE_THETA ** (-jnp.arange(half, dtype=F32) / half)
    ang = pos.astype(F32)[:, None] * inv_freq[None, :]
    cos, sin = jnp.cos(ang), jnp.sin(ang)
    n = pos.shape[0]
    rest1 = jnp.ones((n, width - rot), F32)
    rest0 = jnp.zeros((n, width - rot), F32)
    z = jnp.zeros_like(sin)
    c = jnp.concatenate([cos, cos, rest1], axis=1)
    sa = jnp.concatenate([-sin, z, rest0], axis=1)
    sb = jnp.concatenate([z, sin, rest0], axis=1)
    rep = LANES // width
    return jnp.concatenate([jnp.tile(c, (1, rep)), jnp.tile(sa, (1, rep)), jnp.tile(sb, (1, rep))], axis=1)


def _pad_cols(w, n):
    return jnp.pad(w, ((0, 0), (0, n - w.shape[1])))


def _tile_plan(n_rows, t, prompt):
    if prompt:
        return dict(tm_proj=256, tm_ffn=512, tf=1024, tq=LANES, tk=512, L=128, lb=128)
    return dict(tm_proj=n_rows, tm_ffn=n_rows, tf=512, tq=LANES, tk=384, L=128, lb=t)


def _layer(x, pos_rows, mode, state, params, splits):
    (norm1_g, w_in, q_norm_g, k_norm_g, conv_w, conv_b, wq_m, wk_m, b_igate, b_fgate,
     hnorm_g, w_out, norm2_g, w_up, w_down) = params
    bsz, t, d = x.shape
    m = bsz * t
    plan = _tile_plan(m, t, mode == "prompt")
    tm_proj, tm_ffn, tf, tq, tk, L, lb = (plan[k] for k in ("tm_proj", "tm_ffn", "tf", "tq", "tk", "L", "lb"))
    x2 = x.reshape(m, d)
    offs = np.cumsum((0,) + tuple(splits))
    col = lambda i: w_in[:, int(offs[i]):int(offs[i + 1])]
    aw, mw = splits[0], splits[6]
    n_q_heads = aw // A_HEAD_DIM
    n_m_heads = mw // M_HEAD_DIM
    wa = jnp.concatenate(
        [col(0), col(1), col(2), col(3), col(4), col(4),
         _pad_cols(col(5), LANES), _pad_cols(col(9), LANES), _pad_cols(col(10), LANES)], axis=1).astype(BF16)
    wb = jnp.concatenate([col(6), col(7), col(8)], axis=1).astype(BF16)
    ra = _rope_tables(pos_rows, A_HEAD_DIM // 4, A_HEAD_DIM)
    ri = _rope_tables(pos_rows, IDX_DIM // 4, IDX_DIM)

    (q, k, v, ki, kb, vb, kiab, qi, wi, ig, fg, u, vm, om) = _proj(
        x2, norm1_g[None], wa, wb, q_norm_g[None], k_norm_g[None], ra, ri,
        tm=tm_proj, n_q_heads=n_q_heads, m_width=mw)

    kvw = A_KV_HEADS * A_HEAD_DIM
    r3 = lambda a_: a_.reshape(bsz, t, a_.shape[1])
    if mode == "prompt":
        c0 = jnp.zeros((bsz, n_m_heads, M_HEAD_DIM, M_HEAD_DIM), F32)
        n0 = jnp.zeros((bsz, n_m_heads, M_HEAD_DIM), F32)
        m0 = jnp.zeros((bsz, 1, LANES), F32)
        cv0 = jnp.zeros((bsz, SUBLANES, mw), F32)
        n_sel = min(TOPK_MAX, t // 4)
        a = _attn(r3(q), r3(qi), r3(wi), r3(kb), r3(vb), r3(kiab),
                  tq=tq, tk=tk, causal=True, s_valid=t, n_sel=n_sel)
    else:
        ck, cv, ckidx, s_c, s_n, s_m, s_conv = state
        p = ck.shape[1]
        s_valid = p + t
        s_pad = -(-s_valid // tk) * tk
        padk = lambda a_: jnp.pad(a_, ((0, 0), (0, s_pad - s_valid), (0, 0)))
        k_all = padk(jnp.concatenate([ck.reshape(bsz, p, kvw).astype(BF16), r3(kb)], axis=1))
        v_all = padk(jnp.concatenate([cv.reshape(bsz, p, kvw).astype(BF16), r3(vb)], axis=1))
        cki = ckidx.astype(BF16)
        zki = jnp.zeros_like(cki)
        ki_c = jnp.concatenate([cki, zki, zki, cki], axis=2)
        ki_all = padk(jnp.concatenate([ki_c, r3(kiab)], axis=1))
        c0, n0 = s_c.astype(F32), s_n.astype(F32)
        m0 = jnp.pad(s_m.astype(F32), ((0, 0), (0, LANES - n_m_heads)))[:, None, :]
        cv0 = jnp.pad(s_conv.astype(F32), ((0, 0), (SUBLANES - (CONV_W - 1), 0), (0, 0)))
        n_sel = min(TOPK_MAX, s_valid // 4)
        a = _attn(r3(q), r3(qi), r3(wi), k_all, v_all, ki_all,
                  tq=tq, tk=tk, causal=False, s_valid=s_valid, n_sel=n_sel)

    bg = jnp.stack([_pad_cols(b_igate[None], LANES)[0], _pad_cols(b_fgate[None], LANES)[0]])
    tri = (np.arange(L)[:, None] >= np.arange(L)[None, :]).astype(np.float32)
    mo, c_new, n_new, m_new, cv_new = _mlstm(
        u, vm, om, ig, fg, c0, n0, m0, cv0,
        conv_w, conv_b[None], wq_m.astype(BF16), wk_m.astype(BF16),
        jnp.swapaxes(wk_m, 1, 2).astype(BF16), bg, hnorm_g[None], jnp.asarray(tri),
        b=bsz, L=L, lb=lb)

    y = _ffn(x2, a.reshape(m, aw), mo, w_out.astype(BF16), norm2_g[None],
             w_up.astype(BF16), w_down.astype(BF16), tm=tm_ffn, tf=tf)
    new_state = (
        k.reshape(bsz, t, A_KV_HEADS, A_HEAD_DIM), v.reshape(bsz, t, A_KV_HEADS, A_HEAD_DIM),
        ki.reshape(bsz, t, IDX_DIM), c_new, n_new, m_new[:, 0, :n_m_heads],
        cv_new[:, SUBLANES - (CONV_W - 1):, :])
    return y.reshape(bsz, t, d), new_state


def kernel(x_prompt, x_sample, cache_k, cache_v, cache_kidx, state_mlstm_C, state_mlstm_n, state_mlstm_m,
           state_conv, norm1_g, w_in, q_norm_g, k_norm_g, conv_w, conv_b, wq_m, wk_m, b_igate, b_fgate,
           hnorm_g, w_out, norm2_g, w_up, w_down):
    depth = w_in.shape[0]
    d = x_prompt.shape[-1]
    mix = w_out.shape[1]
    aw = mix // 2
    mw = mix - aw
    kvw = A_KV_HEADS * A_HEAD_DIM
    n_m_heads = mw // M_HEAD_DIM
    splits = (aw, kvw, kvw, IDX_HEADS * IDX_DIM, IDX_DIM, IDX_HEADS, mw, mw, mw, n_m_heads, n_m_heads)
    t_p = x_prompt.shape[1]
    b_s, t_s = x_sample.shape[0], x_sample.shape[1]
    past = cache_k.shape[2]

    y_p, y_s = x_prompt, x_sample
    new_p, new_s = [], []
    for l in range(depth):
        params = (norm1_g[l], w_in[l], q_norm_g[l], k_norm_g[l], conv_w[l], conv_b[l], wq_m[l], wk_m[l],
                  b_igate[l], b_fgate[l], hnorm_g[l], w_out[l], norm2_g[l], w_up[l], w_down[l])
        y_p, st_p = _layer(
            y_p, jnp.arange(t_p), "prompt", None, params, splits)
        y_s, st_s = _layer(
            y_s, past + (jnp.arange(b_s * t_s) % t_s), "sample",
            (cache_k[l], cache_v[l], cache_kidx[l], state_mlstm_C[l], state_mlstm_n[l],
             state_mlstm_m[l], state_conv[l]), params, splits)
        new_p.append(st_p)
        new_s.append(st_s)

    stack = lambda states, i: jnp.stack([s[i] for s in states])
    return (y_p, y_s,
            *[stack(new_p, i) for i in range(7)],
            *[stack(new_s, i) for i in range(7)])
```

```python
import functools

import jax
import jax.numpy as jnp
import numpy as np
from jax import lax
from jax.experimental import pallas as pl
from jax.experimental.pallas import tpu as pltpu

F32 = jnp.float32
BF16 = jnp.bfloat16
I32 = jnp.int32

EPS = 1e-6
ROPE_THETA = 500000.0
CHUNK = 64
CHUNK_LOG2 = 6
TOPK_MAX = 256
A_HEAD_DIM = 128
A_KV_HEADS = 2
IDX_HEADS = 8
IDX_DIM = 64
M_HEAD_DIM = 128
CONV_W = 4

LANES = 128
SUBLANES = 8
VMEM_LIMIT = 56 * 1024 * 1024
NEG_BIG = -1e30
LOG2_E = 1.4426950408889634
INT_MIN = -(2 ** 31)
BIG_J = 2 ** 30
UNSETTLED = 1e9
SEARCH_EARLY_BITS = 26


def _rms(x, g):
    r = lax.rsqrt(jnp.mean(x * x, axis=-1, keepdims=True) + EPS)
    return x * r * g


def _rope(t, c, sa, sb, half):
    return t * c + pltpu.roll(t, LANES - half, 1) * sa + pltpu.roll(t, half, 1) * sb


def _sigmoid(x):
    return 1.0 / (1.0 + jnp.exp(-x))


def _dot(a, b):
    return jnp.dot(a, b, preferred_element_type=F32)


def _dot_t(a, b):
    return lax.dot_general(a, b, (((1,), (1,)), ((), ())), preferred_element_type=F32)


def _proj_kernel(x_ref, g1_ref, wa_ref, wb_ref, qg_ref, kg_ref, ra_ref, ri_ref,
                 q_out, k_out, v_out, ki_out, kb_out, vb_out, kiab_out, qi_out,
                 wi_out, ig_out, fg_out, u_out, vm_out, om_out, *, n_q_heads):
    xn = _rms(x_ref[...], g1_ref[...]).astype(BF16)
    ra = ra_ref[...]
    ca, saa, sba = ra[:, :LANES], ra[:, LANES:2 * LANES], ra[:, 2 * LANES:]
    ri = ri_ref[...]
    ci, sai, sbi = ri[:, :LANES], ri[:, LANES:2 * LANES], ri[:, 2 * LANES:]
    qg = qg_ref[...]
    kg = kg_ref[...]
    aw = n_q_heads * A_HEAD_DIM
    kvw = A_KV_HEADS * A_HEAD_DIM
    q_scale = A_HEAD_DIM ** -0.5 * LOG2_E
    i_scale = IDX_DIM ** -0.5

    for c in range(0, aw, 512):
        z = _dot(xn, wa_ref[:, c:c + 512])
        for j in range(0, 512, LANES):
            qh = _rope(_rms(z[:, j:j + LANES], qg), ca, saa, sba, A_HEAD_DIM // 8) * q_scale
            q_out[:, c + j:c + j + LANES] = qh.astype(BF16)
    off = aw
    z = _dot(xn, wa_ref[:, off:off + 2 * kvw])
    for j in range(0, kvw, LANES):
        kh = _rope(_rms(z[:, j:j + LANES], kg), ca, saa, sba, A_HEAD_DIM // 8)
        k_out[:, j:j + LANES] = kh
        kb_out[:, j:j + LANES] = kh.astype(BF16)
    vv = z[:, kvw:]
    v_out[...] = vv
    vb_out[...] = vv.astype(BF16)
    off += 2 * kvw
    iw = IDX_HEADS * IDX_DIM
    z = _dot(xn, wa_ref[:, off:off + iw])
    for j in range(0, iw, LANES):
        qi = _rope(z[:, j:j + LANES], ci, sai, sbi, IDX_DIM // 8) * i_scale
        qi_out[:, j:j + LANES] = qi.astype(BF16)
    off += iw
    z = _dot(xn, wa_ref[:, off:off + 4 * LANES])
    ki2 = _rope(z[:, :LANES], ci, sai, sbi, IDX_DIM // 8)
    ki_out[...] = ki2[:, :IDX_DIM]
    lane = lax.broadcasted_iota(I32, (1, LANES), 1)
    lo = lane < IDX_DIM
    kiab_out[:, :LANES] = jnp.where(lo, ki2, 0.0).astype(BF16)
    kiab_out[:, LANES:] = jnp.where(lo, 0.0, ki2).astype(BF16)
    wi_out[...] = z[:, LANES:2 * LANES]
    ig_out[...] = z[:, 2 * LANES:3 * LANES]
    fg_out[...] = z[:, 3 * LANES:]
    mw = u_out.shape[1]
    for c in range(0, mw, 512):
        u_out[:, c:c + 512] = _dot(xn, wb_ref[:, c:c + 512])
        vm_out[:, c:c + 512] = _dot(xn, wb_ref[:, mw + c:mw + c + 512]).astype(BF16)
        om_out[:, c:c + 512] = _dot(xn, wb_ref[:, 2 * mw + c:2 * mw + c + 512])


def _proj(x, g1, wa, wb, qg, kg, ra, ri, *, tm, n_q_heads, m_width):
    m, d = x.shape
    n_tab = ra.shape[0] // tm
    aw = n_q_heads * A_HEAD_DIM
    kvw = A_KV_HEADS * A_HEAD_DIM
    iw = IDX_HEADS * IDX_DIM
    row = lambda w: pl.BlockSpec((tm, w), lambda i: (i, 0))
    const = lambda a: pl.BlockSpec(a.shape, lambda i: (0, 0), pipeline_mode=pl.Buffered(1))
    tab = pl.BlockSpec((tm, 3 * LANES), lambda i: (i % n_tab, 0))
    outs = [
        (aw, BF16), (kvw, F32), (kvw, F32), (IDX_DIM, F32), (kvw, BF16), (kvw, BF16),
        (2 * LANES, BF16), (iw, BF16), (LANES, F32), (LANES, F32), (LANES, F32),
        (m_width, F32), (m_width, BF16), (m_width, F32),
    ]
    return pl.pallas_call(
        functools.partial(_proj_kernel, n_q_heads=n_q_heads),
        grid=(m // tm,),
        in_specs=[row(d), const(g1), const(wa), const(wb), const(qg), const(kg), tab, tab],
        out_specs=[row(w) for w, _ in outs],
        out_shape=[jax.ShapeDtypeStruct((m, w), dt) for w, dt in outs],
        compiler_params=pltpu.CompilerParams(
            dimension_semantics=("arbitrary",), vmem_limit_bytes=VMEM_LIMIT),
        name="proj",
    )(x, g1, wa, wb, qg, kg, ra, ri)


def _pad_rows(x, n):
    if x.shape[0] == n:
        return x
    return jnp.concatenate([x, jnp.zeros((n - x.shape[0],) + x.shape[1:], x.dtype)], axis=0)


V_EXT_ROWS = A_HEAD_DIM + 16


def _attn_kernel(q_ref, qi_ref, wi_ref, k_ref, vt_ref, ki_ref, o_ref,
                 key_scr, tie_scr, qs_scr, qis_scr, m_scr, acc_scr, sa_scr, sb_scr, cm_scr, da_scr, db_scr,
                 *, tq, tqb, tk, causal, s_valid, n_sel, n_tiles, group):
    start = pl.program_id(1) * tqb
    nr = tk // SUBLANES
    n_acc = 8
    lane = lax.broadcasted_iota(I32, (1, tq), 1)
    rowi = lax.broadcasted_iota(I32, (tk, 1), 0)
    if causal:
        limit = (lax.shift_right_logical(start + lane, CHUNK_LOG2) + 1) * CHUNK
        nkt = jnp.minimum((start + tqb + tk - 1) // tk, n_tiles)
    else:
        limit = jnp.full((1, tq), s_valid, I32)
        nkt = n_tiles

    q = _pad_rows(q_ref[0], tq)
    for g in range(A_KV_HEADS):
        for r in range(group):
            h = g * group + r
            qs_scr[g, r * tq:(r + 1) * tq, :] = q[:, h * LANES:(h + 1) * LANES]
    qi = _pad_rows(qi_ref[0], tq)
    for j in range(IDX_HEADS // 2):
        qis_scr[j * tq:(j + 1) * tq, :] = qi[:, j * LANES:(j + 1) * LANES]
    w_t = jnp.transpose(_pad_rows(wi_ref[0], tq))[:IDX_HEADS] * (IDX_HEADS ** -0.5)

    d_bufs = (da_scr, db_scr)

    def stage_scores(tile, par):
        t_c = jnp.minimum(tile, nkt - 1)
        off = pl.multiple_of(t_c * tk, tk)
        kiab = ki_ref[0, pl.ds(off, tk), :]
        for half in range(IDX_HEADS // 4):
            qh = qis_scr[half * 2 * tq:(half + 1) * 2 * tq, :]
            for odd in range(2):
                d_bufs[par][2 * half + odd] = _dot_t(kiab[:, odd * LANES:(odd + 1) * LANES], qh)

    def finish_scores(t_c, par):
        acc = jnp.zeros((tk, tq), F32)
        for half in range(IDX_HEADS // 4):
            for odd in range(2):
                d = d_bufs[par][2 * half + odd]
                for jj in range(2):
                    h = 2 * (2 * half + jj) + odd
                    acc = acc + w_t[h:h + 1, :] * jnp.maximum(d[:, jj * tq:(jj + 1) * tq], 0.0)
        bits = lax.bitcast_convert_type(acc, I32)
        key = bits ^ (lax.shift_right_arithmetic(bits, 31) & 0x7FFFFFFF)
        key = jnp.where(key == -1, 0, key)
        key_scr[t_c] = jnp.where(t_c * tk + rowi < limit, key, INT_MIN)

    stage_scores(0, 0)

    def score_pair(i, carry):
        stage_scores(2 * i + 1, 1)
        finish_scores(2 * i, 0)
        stage_scores(2 * i + 2, 0)
        finish_scores(2 * i + 1, 1)
        return carry

    lax.fori_loop(0, nkt // 2, score_pair, 0)

    @pl.when(nkt % 2 == 1)
    def _():
        finish_scores(nkt - 1, 0)

    def count_ref(ref, pred):
        def body(kt, acc):
            hit = pred(ref[kt].reshape(nr, SUBLANES, tq))
            ones = jnp.where(hit, 1.0, 0.0).reshape(nr // n_acc, n_acc, SUBLANES, tq)
            return acc + jnp.sum(ones, axis=0)
        acc = lax.fori_loop(0, nkt, body, jnp.zeros((n_acc, SUBLANES, tq), F32))
        return jnp.sum(jnp.sum(acc, axis=0), axis=0, keepdims=True)

    def bit_step(i, carry):
        t_cur, c_cur = carry
        cand = t_cur + lax.shift_left(jnp.int32(1), 31 - i)
        cnt = count_ref(key_scr, lambda key: key >= cand[None])
        ok = cnt >= n_sel
        return jnp.where(ok, cand, t_cur), jnp.where(ok, cnt, c_cur)

    def any_lane(mask):
        return jnp.max(jnp.where(mask, 1.0, 0.0)) > 0.0

    valid = lane < tqb
    n_self = float(n_sel)

    def settle(t_cur, c_cur):
        pend = jnp.logical_and(valid, c_cur != n_self)
        return lax.cond(any_lane(pend), lambda: count_ref(key_scr, lambda key: key > t_cur[None]),
                        lambda: jnp.zeros((1, tq), F32))

    t_e, c_e = lax.fori_loop(
        0, SEARCH_EARLY_BITS, bit_step,
        (jnp.full((1, tq), INT_MIN, I32), jnp.where(limit < n_sel, n_self, UNSETTLED)))
    g_e = settle(t_e, c_e)
    unresolved = jnp.logical_and(jnp.logical_and(valid, c_e != n_self), g_e >= n_self)

    def finish_search():
        t_l, c_l = lax.fori_loop(SEARCH_EARLY_BITS, 32, bit_step, (t_e, c_e))
        return t_l, c_l, settle(t_l, c_l)

    t_fin, c_fin, c_gt = lax.cond(any_lane(unresolved), finish_search, lambda: (t_e, c_e, g_e))
    is_min = t_fin == INT_MIN
    needs = jnp.logical_and(jnp.logical_and(jnp.logical_not(is_min), c_fin > n_sel), valid)
    j_fast = jnp.where(is_min, 0, BIG_J)

    def tie_search():
        rem = n_self - c_gt

        def mark(kt, carry):
            tie_scr[kt] = jnp.where(key_scr[kt] == t_fin, kt * tk + rowi, BIG_J)
            return carry

        lax.fori_loop(0, nkt, mark, 0)
        j_bits = (n_tiles * tk).bit_length()

        def j_pending(carry):
            i, _, f_cur = carry
            return jnp.logical_and(i < j_bits, any_lane(jnp.logical_and(needs, f_cur != rem)))

        def j_step(carry):
            i, j_cur, f_cur = carry
            cand = j_cur + lax.shift_left(jnp.int32(1), j_bits - 1 - i)
            f = count_ref(tie_scr, lambda col: col < cand[None])
            ok = f <= rem
            return i + 1, jnp.where(ok, cand, j_cur), jnp.where(ok, f, f_cur)

        _, j_slow, _ = lax.while_loop(
            j_pending, j_step, (jnp.int32(0), jnp.zeros((1, tq), I32), jnp.zeros((1, tq), F32)))
        return jnp.where(needs, j_slow, j_fast)

    j_fin = lax.cond(any_lane(needs), tie_search, lambda: j_fast)

    m_scr[...] = jnp.full(m_scr.shape, NEG_BIG, F32)
    acc_scr[...] = jnp.zeros(acc_scr.shape, F32)

    s_bufs = (sa_scr, sb_scr)

    def stage_logits(tile, par):
        t_c = jnp.minimum(tile, nkt - 1)
        off = pl.multiple_of(t_c * tk, tk)
        key = key_scr[t_c]
        sel = jnp.logical_or(key > t_fin, jnp.logical_and(key == t_fin, off + rowi < j_fin))
        bias = jnp.where(sel, 0.0, -jnp.inf)
        bias = jnp.concatenate([bias] * group, axis=1)
        for g in range(A_KV_HEADS):
            s = _dot_t(k_ref[0, pl.ds(off, tk), g * LANES:(g + 1) * LANES], qs_scr[g]) + bias
            s_bufs[par][g] = s
            cm_scr[par, g] = jnp.max(s, axis=0, keepdims=True)

    def attend_tile(t_c, par):
        for g in range(A_KV_HEADS):
            m_old = m_scr[g]
            m_new = jnp.maximum(m_old, cm_scr[par, g])
            alpha = jnp.exp2(m_old - m_new)
            p = jnp.exp2(s_bufs[par][g] - m_new).astype(BF16)
            acc_scr[g] = alpha * acc_scr[g] + _dot(vt_ref[0, t_c, g], p)
            m_scr[g] = m_new

    stage_logits(0, 0)

    def attend_pair(i, carry):
        stage_logits(2 * i + 1, 1)
        attend_tile(2 * i, 0)
        stage_logits(2 * i + 2, 0)
        attend_tile(2 * i + 1, 1)
        return carry

    lax.fori_loop(0, nkt // 2, attend_pair, 0)

    @pl.when(nkt % 2 == 1)
    def _():
        attend_tile(nkt - 1, 0)
    for g in range(A_KV_HEADS):
        acc = acc_scr[g]
        out_t = acc[:A_HEAD_DIM] / acc[A_HEAD_DIM:A_HEAD_DIM + 1]
        for r in range(group):
            h = g * group + r
            o = jnp.transpose(out_t[:, r * tq:(r + 1) * tq])
            o_ref[0, :, h * LANES:(h + 1) * LANES] = o[:tqb].astype(BF16)


def _attn(q, qi, wi, kb, vb, kiab, *, tq, tk, causal, s_valid, n_sel):
    b, t, aw = q.shape
    tqb = min(tq, t)
    s_pad = kb.shape[1]
    n_tiles = s_pad // tk
    n_heads = aw // A_HEAD_DIM
    group = n_heads // A_KV_HEADS
    vt = jnp.transpose(vb.reshape(b, s_pad, A_KV_HEADS, A_HEAD_DIM), (0, 2, 3, 1))
    vt = jnp.concatenate([vt, jnp.ones((b, A_KV_HEADS, V_EXT_ROWS - A_HEAD_DIM, s_pad), BF16)], axis=2)
    vt = jnp.transpose(vt.reshape(b, A_KV_HEADS, V_EXT_ROWS, n_tiles, tk), (0, 3, 1, 2, 4))
    qspec = lambda w: pl.BlockSpec((1, tqb, w), lambda bi, qi_: (bi, qi_, 0))
    kspec = lambda w: pl.BlockSpec((1, s_pad, w), lambda bi, qi_: (bi, 0, 0))
    kern = functools.partial(_attn_kernel, tq=tq, tqb=tqb, tk=tk, causal=causal, s_valid=s_valid,
                             n_sel=n_sel, n_tiles=n_tiles, group=group)
    return pl.pallas_call(
        kern,
        grid=(b, t // tqb),
        in_specs=[qspec(aw), qspec(qi.shape[2]), qspec(LANES),
                  kspec(kb.shape[2]),
                  pl.BlockSpec((1,) + vt.shape[1:], lambda bi, qi_: (bi, 0, 0, 0, 0)),
                  kspec(kiab.shape[2])],
        out_specs=qspec(aw),
        out_shape=jax.ShapeDtypeStruct((b, t, aw), BF16),
        scratch_shapes=[
            pltpu.VMEM((n_tiles, tk, tq), I32),
            pltpu.VMEM((n_tiles, tk, tq), I32),
            pltpu.VMEM((A_KV_HEADS, group * tq, LANES), BF16),
            pltpu.VMEM((IDX_HEADS // 2 * tq, LANES), BF16),
            pltpu.VMEM((A_KV_HEADS, 1, group * tq), F32),
            pltpu.VMEM((A_KV_HEADS, V_EXT_ROWS, group * tq), F32),
            pltpu.VMEM((A_KV_HEADS, tk, group * tq), F32),
            pltpu.VMEM((A_KV_HEADS, tk, group * tq), F32),
            pltpu.VMEM((2, A_KV_HEADS, 1, group * tq), F32),
            pltpu.VMEM((IDX_HEADS // 2, tk, 2 * tq), F32),
            pltpu.VMEM((IDX_HEADS // 2, tk, 2 * tq), F32),
        ],
        compiler_params=pltpu.CompilerParams(
            dimension_semantics=("arbitrary", "arbitrary"), vmem_limit_bytes=VMEM_LIMIT),
        name="attn",
    )(q, qi, wi, kb, vt, kiab)


def _mlstm_kernel(u_ref, vm_ref, om_ref, ig_ref, fg_ref, c0_ref, n0_ref, m0_ref, cv0_ref,
                  cw_ref, cb_ref, wq_ref, wk_ref, wkt_ref, bg_ref, hg_ref, tri_ref,
                  mo_ref, c_out, n_out, m_out, cv_out,
                  c_scr, n_scr, m_scr, prev_scr, ubuf,
                  *, L, lb, n_heads):
    t = pl.program_id(1)
    nt = pl.num_programs(1)
    hd = M_HEAD_DIM
    pad = SUBLANES

    @pl.when(t == 0)
    def _():
        c_scr[...] = c0_ref[0]
        n_scr[...] = n0_ref[0]
        m_scr[...] = m0_ref[0]
        prev_scr[...] = cv0_ref[0]

    ubuf[0:pad, :] = prev_scr[...]
    ubuf[pad:pad + L, :] = _pad_rows(u_ref[...], L)
    uc = cb_ref[...]
    for i in range(CONV_W):
        uc = uc + ubuf[pad - (CONV_W - 1) + i:pad - (CONV_W - 1) + i + L, :] * cw_ref[i:i + 1, :]
    prev_scr[...] = ubuf[lb:lb + pad, :]
    uh = (uc * _sigmoid(uc)).astype(BF16)

    vm = _pad_rows(vm_ref[...], L)
    om = _pad_rows(om_ref[...], L)
    rowv = lax.broadcasted_iota(I32, (L, LANES), 0) < lb
    li = jnp.where(rowv, _pad_rows(ig_ref[...], L) + bg_ref[0:1, :], NEG_BIG)
    xf = _pad_rows(fg_ref[...], L) + bg_ref[1:2, :]
    lf = jnp.where(rowv, jnp.minimum(xf, 0.0) - jnp.log(1.0 + jnp.exp(-jnp.abs(xf))), 0.0)
    b = jnp.dot(tri_ref[...], lf, precision=lax.Precision.HIGHEST, preferred_element_type=F32)
    c = li - b
    c_t = c.T
    m_prev = m_scr[...]
    inter = b + m_prev
    b_last = b[L - 1:L, :]
    dec = b_last - b + li
    m_new = jnp.maximum(b_last + m_prev, jnp.max(dec, axis=0, keepdims=True))
    wts = jnp.exp(dec - m_new)
    wts_t = wts.T
    sc = jnp.exp(b_last + m_prev - m_new)
    causal = lax.broadcasted_iota(I32, (L, L), 0) >= lax.broadcasted_iota(I32, (L, L), 1)
    k_scale = hd ** -0.5

    heads = range(n_heads)
    sls = [slice(h * hd, (h + 1) * hd) for h in heads]
    q_l = [_dot(uh[:, sls[h]], wq_ref[h]) for h in heads]
    k_l = [_dot(uh[:, sls[h]], wk_ref[h]) * k_scale for h in heads]
    kt_l = [_dot_t(wkt_ref[h], uh[:, sls[h]]) * k_scale for h in heads]
    qb_l = [q.astype(BF16) for q in q_l]
    s_l = [_dot_t(qb_l[h], k_l[h].astype(BF16)) for h in heads]
    qc_l = [_dot(qb_l[h], c_scr[h].astype(BF16)) for h in heads]
    d_l = [jnp.where(causal, b[:, h:h + 1] + c_t[h:h + 1, :], -jnp.inf) for h in heads]
    mt_l = [jnp.maximum(inter[:, h:h + 1], jnp.max(d_l[h], axis=-1, keepdims=True)) for h in heads]
    sw_l = [jnp.exp(d_l[h] - mt_l[h]) * s_l[h] for h in heads]
    sp_l = [jnp.exp(inter[:, h:h + 1] - mt_l[h]) for h in heads]
    num_l = [sp_l[h] * qc_l[h] + _dot(sw_l[h].astype(BF16), vm[:, sls[h]]) for h in heads]
    den_l = [sp_l[h] * jnp.sum(q_l[h] * n_scr[h:h + 1, :], axis=-1, keepdims=True)
             + jnp.sum(sw_l[h], axis=-1, keepdims=True) for h in heads]
    kv_l = [_dot((kt_l[h] * wts_t[h:h + 1, :]).astype(BF16), vm[:, sls[h]]) for h in heads]
    hh_l = [num_l[h] / jnp.maximum(jnp.abs(den_l[h]), jnp.exp(-mt_l[h])) for h in heads]
    r_l = [lax.rsqrt(jnp.mean(hh * hh, axis=-1, keepdims=True) + EPS) for hh in hh_l]
    og_l = [_sigmoid(om[:, sls[h]]) for h in heads]
    for h in heads:
        mo_ref[:, sls[h]] = (og_l[h] * (hh_l[h] * r_l[h] * hg_ref[:, sls[h]]))[:lb].astype(BF16)
    for h in heads:
        sc_h = sc[:, h:h + 1]
        c_scr[h] = sc_h * c_scr[h] + kv_l[h]
        n_scr[h:h + 1, :] = (sc_h * n_scr[h:h + 1, :]
                             + jnp.sum(k_l[h] * wts[:, h:h + 1], axis=0, keepdims=True))
    m_scr[...] = m_new

    @pl.when(t == nt - 1)
    def _():
        c_out[0] = c_scr[...]
        n_out[0] = n_scr[...]
        m_out[0] = m_scr[...]
        cv_out[0] = prev_scr[...]


def _mlstm(u, vm, om, ig, fg, c0, n0, m0, cv0, cw, cb, wq, wk, wkt, bg, hg, tri, *, b, L, lb):
    m, mw = u.shape
    nt = m // (b * lb)
    n_heads = mw // M_HEAD_DIM
    row = lambda w: pl.BlockSpec((lb, w), lambda bi, ti: (bi * nt + ti, 0))
    const = lambda a: pl.BlockSpec(a.shape, lambda bi, ti: (0,) * a.ndim)
    perb = lambda a: pl.BlockSpec((1,) + a.shape[1:], lambda bi, ti: (bi,) + (0,) * (a.ndim - 1))
    out_shapes = [
        jax.ShapeDtypeStruct((m, mw), BF16),
        jax.ShapeDtypeStruct(c0.shape, F32),
        jax.ShapeDtypeStruct(n0.shape, F32),
        jax.ShapeDtypeStruct(m0.shape, F32),
        jax.ShapeDtypeStruct(cv0.shape, F32),
    ]
    return pl.pallas_call(
        functools.partial(_mlstm_kernel, L=L, lb=lb, n_heads=n_heads),
        grid=(b, nt),
        in_specs=[row(mw), row(mw), row(mw), row(LANES), row(LANES),
                  perb(c0), perb(n0), perb(m0), perb(cv0),
                  const(cw), const(cb), const(wq), const(wk), const(wkt), const(bg), const(hg), const(tri)],
        out_specs=[row(mw), perb(c0), perb(n0), perb(m0), perb(cv0)],
        out_shape=out_shapes,
        scratch_shapes=[
            pltpu.VMEM(c0.shape[1:], F32),
            pltpu.VMEM(n0.shape[1:], F32),
            pltpu.VMEM(m0.shape[1:], F32),
            pltpu.VMEM((SUBLANES, mw), F32),
            pltpu.VMEM((L + SUBLANES, mw), F32),
        ],
        compiler_params=pltpu.CompilerParams(
            dimension_semantics=("arbitrary", "arbitrary"), vmem_limit_bytes=VMEM_LIMIT),
        name="mlstm",
    )(u, vm, om, ig, fg, c0, n0, m0, cv0, cw, cb, wq, wk, wkt, bg, hg, tri)


def _ffn_kernel(x_ref, a_ref, mo_ref, wo_ref, g2_ref, wu_ref, wd_ref, y_ref, hn_scr):
    j = pl.program_id(1)

    @pl.when(j == 0)
    def _():
        aw = a_ref.shape[1]
        h = x_ref[...] + _dot(a_ref[...], wo_ref[:aw, :]) + _dot(mo_ref[...], wo_ref[aw:, :])
        y_ref[...] = h
        hn_scr[...] = _rms(h, g2_ref[...]).astype(BF16)

    f = jnp.maximum(_dot(hn_scr[...], wu_ref[...]), 0.0)
    y_ref[...] += _dot((f * f).astype(BF16), wd_ref[...])


def _ffn(x, a, mo, wo, g2, wu, wd, *, tm, tf):
    m, d = x.shape
    dff = wu.shape[1]
    row = lambda w: pl.BlockSpec((tm, w), lambda i, j: (i, 0))
    const = lambda arr: pl.BlockSpec(arr.shape, lambda i, j: (0, 0), pipeline_mode=pl.Buffered(1))
    return pl.pallas_call(
        _ffn_kernel,
        grid=(m // tm, dff // tf),
        in_specs=[row(d), row(a.shape[1]), row(mo.shape[1]), const(wo), const(g2),
                  pl.BlockSpec((d, tf), lambda i, j: (0, j)),
                  pl.BlockSpec((tf, d), lambda i, j: (j, 0))],
        out_specs=row(d),
        out_shape=jax.ShapeDtypeStruct((m, d), F32),
        scratch_shapes=[pltpu.VMEM((tm, d), BF16)],
        compiler_params=pltpu.CompilerParams(
            dimension_semantics=("arbitrary", "arbitrary"), vmem_limit_bytes=VMEM_LIMIT),
        name="ffn",
    )(x, a, mo, wo, g2, wu, wd)


def _rope_tables(pos, rot, width):
    half = rot // 2
    inv_freq = ROPE_THETA ** (-jnp.arange(half, dtype=F32) / half)
    ang = pos.astype(F32)[:, None] * inv_freq[None, :]
    cos, sin = jnp.cos(ang), jnp.sin(ang)
    n = pos.shape[0]
    rest1 = jnp.ones((n, width - rot), F32)
    rest0 = jnp.zeros((n, width - rot), F32)
    z = jnp.zeros_like(sin)
    c = jnp.concatenate([cos, cos, rest1], axis=1)
    sa = jnp.concatenate([-sin, z, rest0], axis=1)
    sb = jnp.concatenate([z, sin, rest0], axis=1)
    rep = LANES // width
    return jnp.concatenate([jnp.tile(c, (1, rep)), jnp.tile(sa, (1, rep)), jnp.tile(sb, (1, rep))], axis=1)


def _pad_cols(w, n):
    return jnp.pad(w, ((0, 0), (0, n - w.shape[1])))


def _tile_plan(n_rows, t, prompt):
    if prompt:
        return dict(tm_proj=256, tm_ffn=512, tf=1024, tq=LANES, tk=512, L=128, lb=128)
    return dict(tm_proj=n_rows, tm_ffn=n_rows, tf=512, tq=LANES, tk=384, L=128, lb=t)


def _layer(x, pos_rows, mode, state, params, splits):
    (norm1_g, w_in, q_norm_g, k_norm_g, conv_w, conv_b, wq_m, wk_m, b_igate, b_fgate,
     hnorm_g, w_out, norm2_g, w_up, w_down) = params
    bsz, t, d = x.shape
    m = bsz * t
    plan = _tile_plan(m, t, mode == "prompt")
    tm_proj, tm_ffn, tf, tq, tk, L, lb = (plan[k] for k in ("tm_proj", "tm_ffn", "tf", "tq", "tk", "L", "lb"))
    x2 = x.reshape(m, d)
    offs = np.cumsum((0,) + tuple(splits))
    col = lambda i: w_in[:, int(offs[i]):int(offs[i + 1])]
    aw, mw = splits[0], splits[6]
    n_q_heads = aw // A_HEAD_DIM
    n_m_heads = mw // M_HEAD_DIM
    wa = jnp.concatenate(
        [col(0), col(1), col(2), col(3), col(4), col(4),
         _pad_cols(col(5), LANES), _pad_cols(col(9), LANES), _pad_cols(col(10), LANES)], axis=1).astype(BF16)
    wb = jnp.concatenate([col(6), col(7), col(8)], axis=1).astype(BF16)
    ra = _rope_tables(pos_rows, A_HEAD_DIM // 4, A_HEAD_DIM)
    ri = _rope_tables(pos_rows, IDX_DIM // 4, IDX_DIM)

    (q, k, v, ki, kb, vb, kiab, qi, wi, ig, fg, u, vm, om) = _proj(
        x2, norm1_g[None], wa, wb, q_norm_g[None], k_norm_g[None], ra, ri,
        tm=tm_proj, n_q_heads=n_q_heads, m_width=mw)

    kvw = A_KV_HEADS * A_HEAD_DIM
    r3 = lambda a_: a_.reshape(bsz, t, a_.shape[1])
    if mode == "prompt":
        c0 = jnp.zeros((bsz, n_m_heads, M_HEAD_DIM, M_HEAD_DIM), F32)
        n0 = jnp.zeros((bsz, n_m_heads, M_HEAD_DIM), F32)
        m0 = jnp.zeros((bsz, 1, LANES), F32)
        cv0 = jnp.zeros((bsz, SUBLANES, mw), F32)
        n_sel = min(TOPK_MAX, t // 4)
        a = _attn(r3(q), r3(qi), r3(wi), r3(kb), r3(vb), r3(kiab),
                  tq=tq, tk=tk, causal=True, s_valid=t, n_sel=n_sel)
    else:
        ck, cv, ckidx, s_c, s_n, s_m, s_conv = state
        p = ck.shape[1]
        s_valid = p + t
        s_pad = -(-s_valid // tk) * tk
        padk = lambda a_: jnp.pad(a_, ((0, 0), (0, s_pad - s_valid), (0, 0)))
        k_all = padk(jnp.concatenate([ck.reshape(bsz, p, kvw).astype(BF16), r3(kb)], axis=1))
        v_all = padk(jnp.concatenate([cv.reshape(bsz, p, kvw).astype(BF16), r3(vb)], axis=1))
        cki = ckidx.astype(BF16)
        zki = jnp.zeros_like(cki)
        ki_c = jnp.concatenate([cki, zki, zki, cki], axis=2)
        ki_all = padk(jnp.concatenate([ki_c, r3(kiab)], axis=1))
        c0, n0 = s_c.astype(F32), s_n.astype(F32)
        m0 = jnp.pad(s_m.astype(F32), ((0, 0), (0, LANES - n_m_heads)))[:, None, :]
        cv0 = jnp.pad(s_conv.astype(F32), ((0, 0), (SUBLANES - (CONV_W - 1), 0), (0, 0)))
        n_sel = min(TOPK_MAX, s_valid // 4)
        a = _attn(r3(q), r3(qi), r3(wi), k_all, v_all, ki_all,
                  tq=tq, tk=tk, causal=False, s_valid=s_valid, n_sel=n_sel)

    bg = jnp.stack([_pad_cols(b_igate[None], LANES)[0], _pad_cols(b_fgate[None], LANES)[0]])
    tri = (np.arange(L)[:, None] >= np.arange(L)[None, :]).astype(np.float32)
    mo, c_new, n_new, m_new, cv_new = _mlstm(
        u, vm, om, ig, fg, c0, n0, m0, cv0,
        conv_w, conv_b[None], wq_m.astype(BF16), wk_m.astype(BF16),
        jnp.swapaxes(wk_m, 1, 2).astype(BF16), bg, hnorm_g[None], jnp.asarray(tri),
        b=bsz, L=L, lb=lb)

    y = _ffn(x2, a.reshape(m, aw), mo, w_out.astype(BF16), norm2_g[None],
             w_up.astype(BF16), w_down.astype(BF16), tm=tm_ffn, tf=tf)
    new_state = (
        k.reshape(bsz, t, A_KV_HEADS, A_HEAD_DIM), v.reshape(bsz, t, A_KV_HEADS, A_HEAD_DIM),
        ki.reshape(bsz, t, IDX_DIM), c_new, n_new, m_new[:, 0, :n_m_heads],
        cv_new[:, SUBLANES - (CONV_W - 1):, :])
    return y.reshape(bsz, t, d), new_state


def kernel(x_prompt, x_sample, cache_k, cache_v, cache_kidx, state_mlstm_C, state_mlstm_n, state_mlstm_m,
           state_conv, norm1_g, w_in, q_norm_g, k_norm_g, conv_w, conv_b, wq_m, wk_m, b_igate, b_fgate,
           hnorm_g, w_out, norm2_g, w_up, w_down):
    depth = w_in.shape[0]
    d = x_prompt.shape[-1]
    mix = w_out.shape[1]
    aw = mix // 2
    mw = mix - aw
    kvw = A_KV_HEADS * A_HEAD_DIM
    n_m_heads = mw // M_HEAD_DIM
    splits = (aw, kvw, kvw, IDX_HEADS * IDX_DIM, IDX_DIM, IDX_HEADS, mw, mw, mw, n_m_heads, n_m_heads)
    t_p = x_prompt.shape[1]
    b_s, t_s = x_sample.shape[0], x_sample.shape[1]
    past = cache_k.shape[2]

    y_p, y_s = x_prompt, x_sample
    new_p, new_s = [], []
    for l in range(depth):
        params = (norm1_g[l], w_in[l], q_norm_g[l], k_norm_g[l], conv_w[l], conv_b[l], wq_m[l], wk_m[l],
                  b_igate[l], b_fgate[l], hnorm_g[l], w_out[l], norm2_g[l], w_up[l], w_down[l])
        y_p, st_p = _layer(
            y_p, jnp.arange(t_p), "prompt", None, params, splits)
        y_s, st_s = _layer(
            y_s, past + (jnp.arange(b_s * t_s) % t_s), "sample",
            (cache_k[l], cache_v[l], cache_kidx[l], state_mlstm_C[l], state_mlstm_n[l],
             state_mlstm_m[l], state_conv[l]), params, splits)
        new_p.append(st_p)
        new_s.append(st_s)

    stack = lambda states, i: jnp.stack([s[i] for s in states])
    return (y_p, y_s,
            *[stack(new_p, i) for i in range(7)],
            *[stack(new_s, i) for i in range(7)])
```

```python
import functools

import jax
import jax.numpy as jnp
import numpy as np
from jax import lax
from jax.experimental import pallas as pl
from jax.experimental.pallas import tpu as pltpu

F32 = jnp.float32
BF16 = jnp.bfloat16
I32 = jnp.int32

EPS = 1e-6
ROPE_THETA = 500000.0
CHUNK = 64
CHUNK_LOG2 = 6
TOPK_MAX = 256
A_HEAD_DIM = 128
A_KV_HEADS = 2
IDX_HEADS = 8
IDX_DIM = 64
M_HEAD_DIM = 128
CONV_W = 4

LANES = 128
SUBLANES = 8
VMEM_LIMIT = 56 * 1024 * 1024
NEG_BIG = -1e30
LOG2_E = 1.4426950408889634
INT_MIN = -(2 ** 31)
BIG_J = 2 ** 30
UNSETTLED = 1e9
SEARCH_EARLY_BITS = 26


def _rms(x, g):
    r = lax.rsqrt(jnp.mean(x * x, axis=-1, keepdims=True) + EPS)
    return x * r * g


def _rope(t, c, sa, sb, half):
    return t * c + pltpu.roll(t, LANES - half, 1) * sa + pltpu.roll(t, half, 1) * sb


def _sigmoid(x):
    return 1.0 / (1.0 + jnp.exp(-x))


def _dot(a, b):
    return jnp.dot(a, b, preferred_element_type=F32)


def _dot_t(a, b):
    return lax.dot_general(a, b, (((1,), (1,)), ((), ())), preferred_element_type=F32)


def _proj_kernel(x_ref, g1_ref, wa_ref, wb_ref, qg_ref, kg_ref, ra_ref, ri_ref,
                 q_out, k_out, v_out, ki_out, kb_out, vb_out, kiab_out, qi_out,
                 wi_out, ig_out, fg_out, u_out, vm_out, om_out, *, n_q_heads):
    xn = _rms(x_ref[...], g1_ref[...]).astype(BF16)
    ra = ra_ref[...]
    ca, saa, sba = ra[:, :LANES], ra[:, LANES:2 * LANES], ra[:, 2 * LANES:]
    ri = ri_ref[...]
    ci, sai, sbi = ri[:, :LANES], ri[:, LANES:2 * LANES], ri[:, 2 * LANES:]
    qg = qg_ref[...]
    kg = kg_ref[...]
    aw = n_q_heads * A_HEAD_DIM
    kvw = A_KV_HEADS * A_HEAD_DIM
    q_scale = A_HEAD_DIM ** -0.5 * LOG2_E
    i_scale = IDX_DIM ** -0.5

    for c in range(0, aw, 512):
        z = _dot(xn, wa_ref[:, c:c + 512])
        for j in range(0, 512, LANES):
            qh = _rope(_rms(z[:, j:j + LANES], qg), ca, saa, sba, A_HEAD_DIM // 8) * q_scale
            q_out[:, c + j:c + j + LANES] = qh.astype(BF16)
    off = aw
    z = _dot(xn, wa_ref[:, off:off + 2 * kvw])
    for j in range(0, kvw, LANES):
        kh = _rope(_rms(z[:, j:j + LANES], kg), ca, saa, sba, A_HEAD_DIM // 8)
        k_out[:, j:j + LANES] = kh
        kb_out[:, j:j + LANES] = kh.astype(BF16)
    vv = z[:, kvw:]
    v_out[...] = vv
    vb_out[...] = vv.astype(BF16)
    off += 2 * kvw
    iw = IDX_HEADS * IDX_DIM
    z = _dot(xn, wa_ref[:, off:off + iw])
    for j in range(0, iw, LANES):
        qi = _rope(z[:, j:j + LANES], ci, sai, sbi, IDX_DIM // 8) * i_scale
        qi_out[:, j:j + LANES] = qi.astype(BF16)
    off += iw
    z = _dot(xn, wa_ref[:, off:off + 4 * LANES])
    ki2 = _rope(z[:, :LANES], ci, sai, sbi, IDX_DIM // 8)
    ki_out[...] = ki2[:, :IDX_DIM]
    lane = lax.broadcasted_iota(I32, (1, LANES), 1)
    lo = lane < IDX_DIM
    kiab_out[:, :LANES] = jnp.where(lo, ki2, 0.0).astype(BF16)
    kiab_out[:, LANES:] = jnp.where(lo, 0.0, ki2).astype(BF16)
    wi_out[...] = z[:, LANES:2 * LANES]
    ig_out[...] = z[:, 2 * LANES:3 * LANES]
    fg_out[...] = z[:, 3 * LANES:]
    mw = u_out.shape[1]
    for c in range(0, mw, 512):
        u_out[:, c:c + 512] = _dot(xn, wb_ref[:, c:c + 512])
        vm_out[:, c:c + 512] = _dot(xn, wb_ref[:, mw + c:mw + c + 512]).astype(BF16)
        om_out[:, c:c + 512] = _dot(xn, wb_ref[:, 2 * mw + c:2 * mw + c + 512])


def _proj(x, g1, wa, wb, qg, kg, ra, ri, *, tm, n_q_heads, m_width):
    m, d = x.shape
    n_tab = ra.shape[0] // tm
    aw = n_q_heads * A_HEAD_DIM
    kvw = A_KV_HEADS * A_HEAD_DIM
    iw = IDX_HEADS * IDX_DIM
    row = lambda w: pl.BlockSpec((tm, w), lambda i: (i, 0))
    const = lambda a: pl.BlockSpec(a.shape, lambda i: (0, 0), pipeline_mode=pl.Buffered(1))
    tab = pl.BlockSpec((tm, 3 * LANES), lambda i: (i % n_tab, 0))
    outs = [
        (aw, BF16), (kvw, F32), (kvw, F32), (IDX_DIM, F32), (kvw, BF16), (kvw, BF16),
        (2 * LANES, BF16), (iw, BF16), (LANES, F32), (LANES, F32), (LANES, F32),
        (m_width, F32), (m_width, BF16), (m_width, F32),
    ]
    return pl.pallas_call(
        functools.partial(_proj_kernel, n_q_heads=n_q_heads),
        grid=(m // tm,),
        in_specs=[row(d), const(g1), const(wa), const(wb), const(qg), const(kg), tab, tab],
        out_specs=[row(w) for w, _ in outs],
        out_shape=[jax.ShapeDtypeStruct((m, w), dt) for w, dt in outs],
        compiler_params=pltpu.CompilerParams(
            dimension_semantics=("arbitrary",), vmem_limit_bytes=VMEM_LIMIT),
        name="proj",
    )(x, g1, wa, wb, qg, kg, ra, ri)


def _pad_rows(x, n):
    if x.shape[0] == n:
        return x
    return jnp.concatenate([x, jnp.zeros((n - x.shape[0],) + x.shape[1:], x.dtype)], axis=0)


V_EXT_ROWS = A_HEAD_DIM + 16


def _attn_kernel(q_ref, qi_ref, wi_ref, k_ref, vt_ref, ki_ref, o_ref,
                 key_scr, tie_scr, qs_scr, qis_scr, m_scr, acc_scr, sa_scr, sb_scr, cm_scr, da_scr, db_scr,
                 *, tq, tqb, tk, causal, s_valid, n_sel, n_tiles, group):
    start = pl.program_id(1) * tqb
    nr = tk // SUBLANES
    n_acc = 8
    lane = lax.broadcasted_iota(I32, (1, tq), 1)
    rowi = lax.broadcasted_iota(I32, (tk, 1), 0)
    if causal:
        limit = (lax.shift_right_logical(start + lane, CHUNK_LOG2) + 1) * CHUNK
        nkt = jnp.minimum((start + tqb + tk - 1) // tk, n_tiles)
    else:
        limit = jnp.full((1, tq), s_valid, I32)
        nkt = n_tiles

    q = _pad_rows(q_ref[0], tq)
    for g in range(A_KV_HEADS):
        for r in range(group):
            h = g * group + r
            qs_scr[g, r * tq:(r + 1) * tq, :] = q[:, h * LANES:(h + 1) * LANES]
    qi = _pad_rows(qi_ref[0], tq)
    for j in range(IDX_HEADS // 2):
        qis_scr[j * tq:(j + 1) * tq, :] = qi[:, j * LANES:(j + 1) * LANES]
    w_t = jnp.transpose(_pad_rows(wi_ref[0], tq))[:IDX_HEADS] * (IDX_HEADS ** -0.5)

    d_bufs = (da_scr, db_scr)

    def stage_scores(tile, par):
        t_c = jnp.minimum(tile, nkt - 1)
        off = pl.multiple_of(t_c * tk, tk)
        kiab = ki_ref[0, pl.ds(off, tk), :]
        for half in range(IDX_HEADS // 4):
            qh = qis_scr[half * 2 * tq:(half + 1) * 2 * tq, :]
            for odd in range(2):
                d_bufs[par][2 * half + odd] = _dot_t(kiab[:, odd * LANES:(odd + 1) * LANES], qh)

    def finish_scores(t_c, par):
        acc = jnp.zeros((tk, tq), F32)
        for half in range(IDX_HEADS // 4):
            for odd in range(2):
                d = d_bufs[par][2 * half + odd]
                for jj in range(2):
                    h = 2 * (2 * half + jj) + odd
                    acc = acc + w_t[h:h + 1, :] * jnp.maximum(d[:, jj * tq:(jj + 1) * tq], 0.0)
        bits = lax.bitcast_convert_type(acc, I32)
        key = bits ^ (lax.shift_right_arithmetic(bits, 31) & 0x7FFFFFFF)
        key = jnp.where(key == -1, 0, key)
        key_scr[t_c] = jnp.where(t_c * tk + rowi < limit, key, INT_MIN)

    stage_scores(0, 0)

    def score_pair(i, carry):
        stage_scores(2 * i + 1, 1)
        finish_scores(2 * i, 0)
        stage_scores(2 * i + 2, 0)
        finish_scores(2 * i + 1, 1)
        return carry

    lax.fori_loop(0, nkt // 2, score_pair, 0)

    @pl.when(nkt % 2 == 1)
    def _():
        finish_scores(nkt - 1, 0)

    def count_ref(ref, pred):
        def body(kt, acc):
            hit = pred(ref[kt].reshape(nr, SUBLANES, tq))
            ones = jnp.where(hit, 1.0, 0.0).reshape(nr // n_acc, n_acc, SUBLANES, tq)
            return acc + jnp.sum(ones, axis=0)
        acc = lax.fori_loop(0, nkt, body, jnp.zeros((n_acc, SUBLANES, tq), F32))
        return jnp.sum(jnp.sum(acc, axis=0), axis=0, keepdims=True)

    def bit_step(i, carry):
        t_cur, c_cur = carry
        cand = t_cur + lax.shift_left(jnp.int32(1), 31 - i)
        cnt = count_ref(key_scr, lambda key: key >= cand[None])
        ok = cnt >= n_sel
        return jnp.where(ok, cand, t_cur), jnp.where(ok, cnt, c_cur)

    def any_lane(mask):
        return jnp.max(jnp.where(mask, 1.0, 0.0)) > 0.0

    valid = lane < tqb
    n_self = float(n_sel)

    def settle(t_cur, c_cur):
        pend = jnp.logical_and(valid, c_cur != n_self)
        return lax.cond(any_lane(pend), lambda: count_ref(key_scr, lambda key: key > t_cur[None]),
                        lambda: jnp.zeros((1, tq), F32))

    t_e, c_e = lax.fori_loop(
        0, SEARCH_EARLY_BITS, bit_step,
        (jnp.full((1, tq), INT_MIN, I32), jnp.where(limit < n_sel, n_self, UNSETTLED)))
    g_e = settle(t_e, c_e)
    unresolved = jnp.logical_and(jnp.logical_and(valid, c_e != n_self), g_e >= n_self)

    def finish_search():
        t_l, c_l = lax.fori_loop(SEARCH_EARLY_BITS, 32, bit_step, (t_e, c_e))
        return t_l, c_l, settle(t_l, c_l)

    t_fin, c_fin, c_gt = lax.cond(any_lane(unresolved), finish_search, lambda: (t_e, c_e, g_e))
    is_min = t_fin == INT_MIN
    needs = jnp.logical_and(jnp.logical_and(jnp.logical_not(is_min), c_fin > n_sel), valid)
    j_fast = jnp.where(is_min, 0, BIG_J)

    def tie_search():
        rem = n_self - c_gt

        def mark(kt, carry):
            tie_scr[kt] = jnp.where(key_scr[kt] == t_fin, kt * tk + rowi, BIG_J)
            return carry

        lax.fori_loop(0, nkt, mark, 0)
        j_bits = (n_tiles * tk).bit_length()

        def j_pending(carry):
            i, _, f_cur = carry
            return jnp.logical_and(i < j_bits, any_lane(jnp.logical_and(needs, f_cur != rem)))

        def j_step(carry):
            i, j_cur, f_cur = carry
            cand = j_cur + lax.shift_left(jnp.int32(1), j_bits - 1 - i)
            f = count_ref(tie_scr, lambda col: col < cand[None])
            ok = f <= rem
            return i + 1, jnp.where(ok, cand, j_cur), jnp.where(ok, f, f_cur)

        _, j_slow, _ = lax.while_loop(
            j_pending, j_step, (jnp.int32(0), jnp.zeros((1, tq), I32), jnp.zeros((1, tq), F32)))
        return jnp.where(needs, j_slow, j_fast)

    j_fin = lax.cond(any_lane(needs), tie_search, lambda: j_fast)

    m_scr[...] = jnp.full(m_scr.shape, NEG_BIG, F32)
    acc_scr[...] = jnp.zeros(acc_scr.shape, F32)

    s_bufs = (sa_scr, sb_scr)

    def stage_logits(tile, par):
        t_c = jnp.minimum(tile, nkt - 1)
        off = pl.multiple_of(t_c * tk, tk)
        key = key_scr[t_c]
        sel = jnp.logical_or(key > t_fin, jnp.logical_and(key == t_fin, off + rowi < j_fin))
        bias = jnp.where(sel, 0.0, -jnp.inf)
        bias = jnp.concatenate([bias] * group, axis=1)
        for g in range(A_KV_HEADS):
            s = _dot_t(k_ref[0, pl.ds(off, tk), g * LANES:(g + 1) * LANES], qs_scr[g]) + bias
            s_bufs[par][g] = s
            cm_scr[par, g] = jnp.max(s, axis=0, keepdims=True)

    def attend_tile(t_c, par):
        for g in range(A_KV_HEADS):
            m_old = m_scr[g]
            m_new = jnp.maximum(m_old, cm_scr[par, g])
            alpha = jnp.exp2(m_old - m_new)
            p = jnp.exp2(s_bufs[par][g] - m_new).astype(BF16)
            acc_scr[g] = alpha * acc_scr[g] + _dot(vt_ref[0, t_c, g], p)
            m_scr[g] = m_new

    stage_logits(0, 0)

    def attend_pair(i, carry):
        stage_logits(2 * i + 1, 1)
        attend_tile(2 * i, 0)
        stage_logits(2 * i + 2, 0)
        attend_tile(2 * i + 1, 1)
        return carry

    lax.fori_loop(0, nkt // 2, attend_pair, 0)

    @pl.when(nkt % 2 == 1)
    def _():
        attend_tile(nkt - 1, 0)
    for g in range(A_KV_HEADS):
        acc = acc_scr[g]
        out_t = acc[:A_HEAD_DIM] / acc[A_HEAD_DIM:A_HEAD_DIM + 1]
        for r in range(group):
            h = g * group + r
            o = jnp.transpose(out_t[:, r * tq:(r + 1) * tq])
            o_ref[0, :, h * LANES:(h + 1) * LANES] = o[:tqb].astype(BF16)


def _attn(q, qi, wi, kb, vb, kiab, *, tq, tk, causal, s_valid, n_sel):
    b, t, aw = q.shape
    tqb = min(tq, t)
    s_pad = kb.shape[1]
    n_tiles = s_pad // tk
    n_heads = aw // A_HEAD_DIM
    group = n_heads // A_KV_HEADS
    vt = jnp.transpose(vb.reshape(b, s_pad, A_KV_HEADS, A_HEAD_DIM), (0, 2, 3, 1))
    vt = jnp.concatenate([vt, jnp.ones((b, A_KV_HEADS, V_EXT_ROWS - A_HEAD_DIM, s_pad), BF16)], axis=2)
    vt = jnp.transpose(vt.reshape(b, A_KV_HEADS, V_EXT_ROWS, n_tiles, tk), (0, 3, 1, 2, 4))
    qspec = lambda w: pl.BlockSpec((1, tqb, w), lambda bi, qi_: (bi, qi_, 0))
    kspec = lambda w: pl.BlockSpec((1, s_pad, w), lambda bi, qi_: (bi, 0, 0))
    kern = functools.partial(_attn_kernel, tq=tq, tqb=tqb, tk=tk, causal=causal, s_valid=s_valid,
                             n_sel=n_sel, n_tiles=n_tiles, group=group)
    return pl.pallas_call(
        kern,
        grid=(b, t // tqb),
        in_specs=[qspec(aw), qspec(qi.shape[2]), qspec(LANES),
                  kspec(kb.shape[2]),
                  pl.BlockSpec((1,) + vt.shape[1:], lambda bi, qi_: (bi, 0, 0, 0, 0)),
                  kspec(kiab.shape[2])],
        out_specs=qspec(aw),
        out_shape=jax.ShapeDtypeStruct((b, t, aw), BF16),
        scratch_shapes=[
            pltpu.VMEM((n_tiles, tk, tq), I32),
            pltpu.VMEM((n_tiles, tk, tq), I32),
            pltpu.VMEM((A_KV_HEADS, group * tq, LANES), BF16),
            pltpu.VMEM((IDX_HEADS // 2 * tq, LANES), BF16),
            pltpu.VMEM((A_KV_HEADS, 1, group * tq), F32),
            pltpu.VMEM((A_KV_HEADS, V_EXT_ROWS, group * tq), F32),
            pltpu.VMEM((A_KV_HEADS, tk, group * tq), F32),
            pltpu.VMEM((A_KV_HEADS, tk, group * tq), F32),
            pltpu.VMEM((2, A_KV_HEADS, 1, group * tq), F32),
            pltpu.VMEM((IDX_HEADS // 2, tk, 2 * tq), F32),
            pltpu.VMEM((IDX_HEADS // 2, tk, 2 * tq), F32),
        ],
        compiler_params=pltpu.CompilerParams(
            dimension_semantics=("arbitrary", "arbitrary"), vmem_limit_bytes=VMEM_LIMIT),
        name="attn",
    )(q, qi, wi, kb, vt, kiab)


def _mlstm_kernel(u_ref, vm_ref, om_ref, ig_ref, fg_ref, c0_ref, n0_ref, m0_ref, cv0_ref,
                  cw_ref, cb_ref, wq_ref, wk_ref, wkt_ref, bg_ref, hg_ref, tri_ref,
                  mo_ref, c_out, n_out, m_out, cv_out,
                  c_scr, n_scr, m_scr, prev_scr, ubuf,
                  *, L, lb, n_heads):
    t = pl.program_id(1)
    nt = pl.num_programs(1)
    hd = M_HEAD_DIM
    pad = SUBLANES

    @pl.when(t == 0)
    def _():
        c_scr[...] = c0_ref[0]
        n_scr[...] = n0_ref[0]
        m_scr[...] = m0_ref[0]
        prev_scr[...] = cv0_ref[0]

    ubuf[0:pad, :] = prev_scr[...]
    ubuf[pad:pad + L, :] = _pad_rows(u_ref[...], L)
    uc = cb_ref[...]
    for i in range(CONV_W):
        uc = uc + ubuf[pad - (CONV_W - 1) + i:pad - (CONV_W - 1) + i + L, :] * cw_ref[i:i + 1, :]
    prev_scr[...] = ubuf[lb:lb + pad, :]
    uh = (uc * _sigmoid(uc)).astype(BF16)

    vm = _pad_rows(vm_ref[...], L)
    om = _pad_rows(om_ref[...], L)
    rowv = lax.broadcasted_iota(I32, (L, LANES), 0) < lb
    li = jnp.where(rowv, _pad_rows(ig_ref[...], L) + bg_ref[0:1, :], NEG_BIG)
    xf = _pad_rows(fg_ref[...], L) + bg_ref[1:2, :]
    lf = jnp.where(rowv, jnp.minimum(xf, 0.0) - jnp.log(1.0 + jnp.exp(-jnp.abs(xf))), 0.0)
    b = jnp.dot(tri_ref[...], lf, precision=lax.Precision.HIGHEST, preferred_element_type=F32)
    c = li - b
    c_t = c.T
    m_prev = m_scr[...]
    inter = b + m_prev
    b_last = b[L - 1:L, :]
    dec = b_last - b + li
    m_new = jnp.maximum(b_last + m_prev, jnp.max(dec, axis=0, keepdims=True))
    wts = jnp.exp(dec - m_new)
    wts_t = wts.T
    sc = jnp.exp(b_last + m_prev - m_new)
    causal = lax.broadcasted_iota(I32, (L, L), 0) >= lax.broadcasted_iota(I32, (L, L), 1)
    k_scale = hd ** -0.5

    heads = range(n_heads)
    sls = [slice(h * hd, (h + 1) * hd) for h in heads]
    q_l = [_dot(uh[:, sls[h]], wq_ref[h]) for h in heads]
    k_l = [_dot(uh[:, sls[h]], wk_ref[h]) * k_scale for h in heads]
    kt_l = [_dot_t(wkt_ref[h], uh[:, sls[h]]) * k_scale for h in heads]
    qb_l = [q.astype(BF16) for q in q_l]
    s_l = [_dot_t(qb_l[h], k_l[h].astype(BF16)) for h in heads]
    qc_l = [_dot(qb_l[h], c_scr[h].astype(BF16)) for h in heads]
    d_l = [jnp.where(causal, b[:, h:h + 1] + c_t[h:h + 1, :], -jnp.inf) for h in heads]
    mt_l = [jnp.maximum(inter[:, h:h + 1], jnp.max(d_l[h], axis=-1, keepdims=True)) for h in heads]
    sw_l = [jnp.exp(d_l[h] - mt_l[h]) * s_l[h] for h in heads]
    sp_l = [jnp.exp(inter[:, h:h + 1] - mt_l[h]) for h in heads]
    num_l = [sp_l[h] * qc_l[h] + _dot(sw_l[h].astype(BF16), vm[:, sls[h]]) for h in heads]
    den_l = [sp_l[h] * jnp.sum(q_l[h] * n_scr[h:h + 1, :], axis=-1, keepdims=True)
             + jnp.sum(sw_l[h], axis=-1, keepdims=True) for h in heads]
    kv_l = [_dot((kt_l[h] * wts_t[h:h + 1, :]).astype(BF16), vm[:, sls[h]]) for h in heads]
    hh_l = [num_l[h] / jnp.maximum(jnp.abs(den_l[h]), jnp.exp(-mt_l[h])) for h in heads]
    r_l = [lax.rsqrt(jnp.mean(hh * hh, axis=-1, keepdims=True) + EPS) for hh in hh_l]
    og_l = [_sigmoid(om[:, sls[h]]) for h in heads]
    for h in heads:
        mo_ref[:, sls[h]] = (og_l[h] * (hh_l[h] * r_l[h] * hg_ref[:, sls[h]]))[:lb].astype(BF16)
    for h in heads:
        sc_h = sc[:, h:h + 1]
        c_scr[h] = sc_h * c_scr[h] + kv_l[h]
        n_scr[h:h + 1, :] = (sc_h * n_scr[h:h + 1, :]
                             + jnp.sum(k_l[h] * wts[:, h:h + 1], axis=0, keepdims=True))
    m_scr[...] = m_new

    @pl.when(t == nt - 1)
    def _():
        c_out[0] = c_scr[...]
        n_out[0] = n_scr[...]
        m_out[0] = m_scr[...]
        cv_out[0] = prev_scr[...]


def _mlstm(u, vm, om, ig, fg, c0, n0, m0, cv0, cw, cb, wq, wk, wkt, bg, hg, tri, *, b, L, lb):
    m, mw = u.shape
    nt = m // (b * lb)
    n_heads = mw // M_HEAD_DIM
    row = lambda w: pl.BlockSpec((lb, w), lambda bi, ti: (bi * nt + ti, 0))
    const = lambda a: pl.BlockSpec(a.shape, lambda bi, ti: (0,) * a.ndim)
    perb = lambda a: pl.BlockSpec((1,) + a.shape[1:], lambda bi, ti: (bi,) + (0,) * (a.ndim - 1))
    out_shapes = [
        jax.ShapeDtypeStruct((m, mw), BF16),
        jax.ShapeDtypeStruct(c0.shape, F32),
        jax.ShapeDtypeStruct(n0.shape, F32),
        jax.ShapeDtypeStruct(m0.shape, F32),
        jax.ShapeDtypeStruct(cv0.shape, F32),
    ]
    return pl.pallas_call(
        functools.partial(_mlstm_kernel, L=L, lb=lb, n_heads=n_heads),
        grid=(b, nt),
        in_specs=[row(mw), row(mw), row(mw), row(LANES), row(LANES),
                  perb(c0), perb(n0), perb(m0), perb(cv0),
                  const(cw), const(cb), const(wq), const(wk), const(wkt), const(bg), const(hg), const(tri)],
        out_specs=[row(mw), perb(c0), perb(n0), perb(m0), perb(cv0)],
        out_shape=out_shapes,
        scratch_shapes=[
            pltpu.VMEM(c0.shape[1:], F32),
            pltpu.VMEM(n0.shape[1:], F32),
            pltpu.VMEM(m0.shape[1:], F32),
            pltpu.VMEM((SUBLANES, mw), F32),
            pltpu.VMEM((L + SUBLANES, mw), F32),
        ],
        compiler_params=pltpu.CompilerParams(
            dimension_semantics=("arbitrary", "arbitrary"), vmem_limit_bytes=VMEM_LIMIT),
        name="mlstm",
    )(u, vm, om, ig, fg, c0, n0, m0, cv0, cw, cb, wq, wk, wkt, bg, hg, tri)


def _ffn_kernel(x_ref, a_ref, mo_ref, wo_ref, g2_ref, wu_ref, wd_ref, y_ref, hn_scr):
    j = pl.program_id(1)

    @pl.when(j == 0)
    def _():
        aw = a_ref.shape[1]
        h = x_ref[...] + _dot(a_ref[...], wo_ref[:aw, :]) + _dot(mo_ref[...], wo_ref[aw:, :])
        y_ref[...] = h
        hn_scr[...] = _rms(h, g2_ref[...]).astype(BF16)

    f = jnp.maximum(_dot(hn_scr[...], wu_ref[...]), 0.0)
    y_ref[...] += _dot((f * f).astype(BF16), wd_ref[...])


def _ffn(x, a, mo, wo, g2, wu, wd, *, tm, tf):
    m, d = x.shape
    dff = wu.shape[1]
    row = lambda w: pl.BlockSpec((tm, w), lambda i, j: (i, 0))
    const = lambda arr: pl.BlockSpec(arr.shape, lambda i, j: (0, 0), pipeline_mode=pl.Buffered(1))
    return pl.pallas_call(
        _ffn_kernel,
        grid=(m // tm, dff // tf),
        in_specs=[row(d), row(a.shape[1]), row(mo.shape[1]), const(wo), const(g2),
                  pl.BlockSpec((d, tf), lambda i, j: (0, j)),
                  pl.BlockSpec((tf, d), lambda i, j: (j, 0))],
        out_specs=row(d),
        out_shape=jax.ShapeDtypeStruct((m, d), F32),
        scratch_shapes=[pltpu.VMEM((tm, d), BF16)],
        compiler_params=pltpu.CompilerParams(
            dimension_semantics=("arbitrary", "arbitrary"), vmem_limit_bytes=VMEM_LIMIT),
        name="ffn",
    )(x, a, mo, wo, g2, wu, wd)


def _rope_tables(pos, rot, width):
    half = rot // 2
    inv_freq = ROPE_THETA ** (-jnp.arange(half, dtype=F32) / half)
    ang = pos.astype(F32)[:, None] * inv_freq[None, :]
    cos, sin = jnp.cos(ang), jnp.sin(ang)
    n = pos.shape[0]
    rest1 = jnp.ones((n, width - rot), F32)
    rest0 = jnp.zeros((n, width - rot), F32)
    z = jnp.zeros_like(sin)
    c = jnp.concatenate([cos, cos, rest1], axis=1)
    sa = jnp.concatenate([-sin, z, rest0], axis=1)
    sb = jnp.concatenate([z, sin, rest0], axis=1)
    rep = LANES // width
    return jnp.concatenate([jnp.tile(c, (1, rep)), jnp.tile(sa, (1, rep)), jnp.tile(sb, (1, rep))], axis=1)


def _pad_cols(w, n):
    return jnp.pad(w, ((0, 0), (0, n - w.shape[1])))


def _tile_plan(n_rows, t, n_keys, prompt):
    if prompt:
        return dict(tm_proj=256, tm_ffn=512, tf=1024, tq=LANES, tk=512, L=128, lb=128)
    return dict(tm_proj=n_rows, tm_ffn=n_rows, tf=512, tq=LANES, tk=-(-n_keys // LANES) * LANES, L=128, lb=t)


def _layer(x, pos_rows, mode, state, params, splits):
    (norm1_g, w_in, q_norm_g, k_norm_g, conv_w, conv_b, wq_m, wk_m, b_igate, b_fgate,
     hnorm_g, w_out, norm2_g, w_up, w_down) = params
    bsz, t, d = x.shape
    m = bsz * t
    n_keys = t if mode == "prompt" else state[0].shape[1] + t
    plan = _tile_plan(m, t, n_keys, mode == "prompt")
    tm_proj, tm_ffn, tf, tq, tk, L, lb = (plan[k] for k in ("tm_proj", "tm_ffn", "tf", "tq", "tk", "L", "lb"))
    x2 = x.reshape(m, d)
    offs = np.cumsum((0,) + tuple(splits))
    col = lambda i: w_in[:, int(offs[i]):int(offs[i + 1])]
    aw, mw = splits[0], splits[6]
    n_q_heads = aw // A_HEAD_DIM
    n_m_heads = mw // M_HEAD_DIM
    wa = jnp.concatenate(
        [col(0), col(1), col(2), col(3), col(4), col(4),
         _pad_cols(col(5), LANES), _pad_cols(col(9), LANES), _pad_cols(col(10), LANES)], axis=1).astype(BF16)
    wb = jnp.concatenate([col(6), col(7), col(8)], axis=1).astype(BF16)
    ra = _rope_tables(pos_rows, A_HEAD_DIM // 4, A_HEAD_DIM)
    ri = _rope_tables(pos_rows, IDX_DIM // 4, IDX_DIM)

    (q, k, v, ki, kb, vb, kiab, qi, wi, ig, fg, u, vm, om) = _proj(
        x2, norm1_g[None], wa, wb, q_norm_g[None], k_norm_g[None], ra, ri,
        tm=tm_proj, n_q_heads=n_q_heads, m_width=mw)

    kvw = A_KV_HEADS * A_HEAD_DIM
    r3 = lambda a_: a_.reshape(bsz, t, a_.shape[1])
    if mode == "prompt":
        c0 = jnp.zeros((bsz, n_m_heads, M_HEAD_DIM, M_HEAD_DIM), F32)
        n0 = jnp.zeros((bsz, n_m_heads, M_HEAD_DIM), F32)
        m0 = jnp.zeros((bsz, 1, LANES), F32)
        cv0 = jnp.zeros((bsz, SUBLANES, mw), F32)
        n_sel = min(TOPK_MAX, t // 4)
        a = _attn(r3(q), r3(qi), r3(wi), r3(kb), r3(vb), r3(kiab),
                  tq=tq, tk=tk, causal=True, s_valid=t, n_sel=n_sel)
    else:
        ck, cv, ckidx, s_c, s_n, s_m, s_conv = state
        p = ck.shape[1]
        s_valid = p + t
        s_pad = -(-s_valid // tk) * tk
        padk = lambda a_: jnp.pad(a_, ((0, 0), (0, s_pad - s_valid), (0, 0)))
        k_all = padk(jnp.concatenate([ck.reshape(bsz, p, kvw).astype(BF16), r3(kb)], axis=1))
        v_all = padk(jnp.concatenate([cv.reshape(bsz, p, kvw).astype(BF16), r3(vb)], axis=1))
        cki = ckidx.astype(BF16)
        zki = jnp.zeros_like(cki)
        ki_c = jnp.concatenate([cki, zki, zki, cki], axis=2)
        ki_all = padk(jnp.concatenate([ki_c, r3(kiab)], axis=1))
        c0, n0 = s_c.astype(F32), s_n.astype(F32)
        m0 = jnp.pad(s_m.astype(F32), ((0, 0), (0, LANES - n_m_heads)))[:, None, :]
        cv0 = jnp.pad(s_conv.astype(F32), ((0, 0), (SUBLANES - (CONV_W - 1), 0), (0, 0)))
        n_sel = min(TOPK_MAX, s_valid // 4)
        a = _attn(r3(q), r3(qi), r3(wi), k_all, v_all, ki_all,
                  tq=tq, tk=tk, causal=False, s_valid=s_valid, n_sel=n_sel)

    bg = jnp.stack([_pad_cols(b_igate[None], LANES)[0], _pad_cols(b_fgate[None], LANES)[0]])
    tri = (np.arange(L)[:, None] >= np.arange(L)[None, :]).astype(np.float32)
    mo, c_new, n_new, m_new, cv_new = _mlstm(
        u, vm, om, ig, fg, c0, n0, m0, cv0,
        conv_w, conv_b[None], wq_m.astype(BF16), wk_m.astype(BF16),
        jnp.swapaxes(wk_m, 1, 2).astype(BF16), bg, hnorm_g[None], jnp.asarray(tri),
        b=bsz, L=L, lb=lb)

    y = _ffn(x2, a.reshape(m, aw), mo, w_out.astype(BF16), norm2_g[None],
             w_up.astype(BF16), w_down.astype(BF16), tm=tm_ffn, tf=tf)
    new_state = (
        k.reshape(bsz, t, A_KV_HEADS, A_HEAD_DIM), v.reshape(bsz, t, A_KV_HEADS, A_HEAD_DIM),
        ki.reshape(bsz, t, IDX_DIM), c_new, n_new, m_new[:, 0, :n_m_heads],
        cv_new[:, SUBLANES - (CONV_W - 1):, :])
    return y.reshape(bsz, t, d), new_state


def kernel(x_prompt, x_sample, cache_k, cache_v, cache_kidx, state_mlstm_C, state_mlstm_n, state_mlstm_m,
           state_conv, norm1_g, w_in, q_norm_g, k_norm_g, conv_w, conv_b, wq_m, wk_m, b_igate, b_fgate,
           hnorm_g, w_out, norm2_g, w_up, w_down):
    depth = w_in.shape[0]
    d = x_prompt.shape[-1]
    mix = w_out.shape[1]
    aw = mix // 2
    mw = mix - aw
    kvw = A_KV_HEADS * A_HEAD_DIM
    n_m_heads = mw // M_HEAD_DIM
    splits = (aw, kvw, kvw, IDX_HEADS * IDX_DIM, IDX_DIM, IDX_HEADS, mw, mw, mw, n_m_heads, n_m_heads)
    t_p = x_prompt.shape[1]
    b_s, t_s = x_sample.shape[0], x_sample.shape[1]
    past = cache_k.shape[2]

    y_p, y_s = x_prompt, x_sample
    new_p, new_s = [], []
    for l in range(depth):
        params = (norm1_g[l], w_in[l], q_norm_g[l], k_norm_g[l], conv_w[l], conv_b[l], wq_m[l], wk_m[l],
                  b_igate[l], b_fgate[l], hnorm_g[l], w_out[l], norm2_g[l], w_up[l], w_down[l])
        y_p, st_p = _layer(
            y_p, jnp.arange(t_p), "prompt", None, params, splits)
        y_s, st_s = _layer(
            y_s, past + (jnp.arange(b_s * t_s) % t_s), "sample",
            (cache_k[l], cache_v[l], cache_kidx[l], state_mlstm_C[l], state_mlstm_n[l],
             state_mlstm_m[l], state_conv[l]), params, splits)
        new_p.append(st_p)
        new_s.append(st_s)

    stack = lambda states, i: jnp.stack([s[i] for s in states])
    return (y_p, y_s,
            *[stack(new_p, i) for i in range(7)],
            *[stack(new_s, i) for i in range(7)])
```

```python
import functools

import jax
import jax.numpy as jnp
import numpy as np
from jax import lax
from jax.experimental import pallas as pl
from jax.experimental.pallas import tpu as pltpu

F32 = jnp.float32
BF16 = jnp.bfloat16
I32 = jnp.int32

EPS = 1e-6
ROPE_THETA = 500000.0
CHUNK = 64
CHUNK_LOG2 = 6
TOPK_MAX = 256
A_HEAD_DIM = 128
A_KV_HEADS = 2
IDX_HEADS = 8
IDX_DIM = 64
M_HEAD_DIM = 128
CONV_W = 4

LANES = 128
SUBLANES = 8
VMEM_LIMIT = 56 * 1024 * 1024
NEG_BIG = -1e30
LOG2_E = 1.4426950408889634
INT_MIN = -(2 ** 31)
BIG_J = 2 ** 30
UNSETTLED = 1e9
SEARCH_EARLY_BITS = 26


def _rms(x, g):
    r = lax.rsqrt(jnp.mean(x * x, axis=-1, keepdims=True) + EPS)
    return x * r * g


def _rope(t, c, sa, sb, half):
    return t * c + pltpu.roll(t, LANES - half, 1) * sa + pltpu.roll(t, half, 1) * sb


def _sigmoid(x):
    return 1.0 / (1.0 + jnp.exp(-x))


def _dot(a, b):
    return jnp.dot(a, b, preferred_element_type=F32)


def _dot_t(a, b):
    return lax.dot_general(a, b, (((1,), (1,)), ((), ())), preferred_element_type=F32)


def _proj_kernel(x_ref, g1_ref, wa_ref, wb_ref, qg_ref, kg_ref, ra_ref, ri_ref,
                 q_out, k_out, v_out, ki_out, kb_out, vb_out, kiab_out, qi_out,
                 wi_out, ig_out, fg_out, u_out, vm_out, om_out, *, n_q_heads):
    xn = _rms(x_ref[...], g1_ref[...]).astype(BF16)
    ra = ra_ref[...]
    ca, saa, sba = ra[:, :LANES], ra[:, LANES:2 * LANES], ra[:, 2 * LANES:]
    ri = ri_ref[...]
    ci, sai, sbi = ri[:, :LANES], ri[:, LANES:2 * LANES], ri[:, 2 * LANES:]
    qg = qg_ref[...]
    kg = kg_ref[...]
    aw = n_q_heads * A_HEAD_DIM
    kvw = A_KV_HEADS * A_HEAD_DIM
    q_scale = A_HEAD_DIM ** -0.5 * LOG2_E
    i_scale = IDX_DIM ** -0.5

    for c in range(0, aw, 512):
        z = _dot(xn, wa_ref[:, c:c + 512])
        for j in range(0, 512, LANES):
            qh = _rope(_rms(z[:, j:j + LANES], qg), ca, saa, sba, A_HEAD_DIM // 8) * q_scale
            q_out[:, c + j:c + j + LANES] = qh.astype(BF16)
    off = aw
    z = _dot(xn, wa_ref[:, off:off + 2 * kvw])
    for j in range(0, kvw, LANES):
        kh = _rope(_rms(z[:, j:j + LANES], kg), ca, saa, sba, A_HEAD_DIM // 8)
        k_out[:, j:j + LANES] = kh
        kb_out[:, j:j + LANES] = kh.astype(BF16)
    vv = z[:, kvw:]
    v_out[...] = vv
    vb_out[...] = vv.astype(BF16)
    off += 2 * kvw
    iw = IDX_HEADS * IDX_DIM
    z = _dot(xn, wa_ref[:, off:off + iw])
    for j in range(0, iw, LANES):
        qi = _rope(z[:, j:j + LANES], ci, sai, sbi, IDX_DIM // 8) * i_scale
        qi_out[:, j:j + LANES] = qi.astype(BF16)
    off += iw
    z = _dot(xn, wa_ref[:, off:off + 4 * LANES])
    ki2 = _rope(z[:, :LANES], ci, sai, sbi, IDX_DIM // 8)
    ki_out[...] = ki2[:, :IDX_DIM]
    lane = lax.broadcasted_iota(I32, (1, LANES), 1)
    lo = lane < IDX_DIM
    kiab_out[:, :LANES] = jnp.where(lo, ki2, 0.0).astype(BF16)
    kiab_out[:, LANES:] = jnp.where(lo, 0.0, ki2).astype(BF16)
    wi_out[...] = z[:, LANES:2 * LANES]
    ig_out[...] = z[:, 2 * LANES:3 * LANES]
    fg_out[...] = z[:, 3 * LANES:]
    mw = u_out.shape[1]
    for c in range(0, mw, 512):
        u_out[:, c:c + 512] = _dot(xn, wb_ref[:, c:c + 512])
        vm_out[:, c:c + 512] = _dot(xn, wb_ref[:, mw + c:mw + c + 512]).astype(BF16)
        om_out[:, c:c + 512] = _dot(xn, wb_ref[:, 2 * mw + c:2 * mw + c + 512])


def _proj(x, g1, wa, wb, qg, kg, ra, ri, *, tm, n_q_heads, m_width):
    m, d = x.shape
    n_tab = ra.shape[0] // tm
    aw = n_q_heads * A_HEAD_DIM
    kvw = A_KV_HEADS * A_HEAD_DIM
    iw = IDX_HEADS * IDX_DIM
    row = lambda w: pl.BlockSpec((tm, w), lambda i: (i, 0))
    const = lambda a: pl.BlockSpec(a.shape, lambda i: (0, 0), pipeline_mode=pl.Buffered(1))
    tab = pl.BlockSpec((tm, 3 * LANES), lambda i: (i % n_tab, 0))
    outs = [
        (aw, BF16), (kvw, F32), (kvw, F32), (IDX_DIM, F32), (kvw, BF16), (kvw, BF16),
        (2 * LANES, BF16), (iw, BF16), (LANES, F32), (LANES, F32), (LANES, F32),
        (m_width, F32), (m_width, BF16), (m_width, F32),
    ]
    return pl.pallas_call(
        functools.partial(_proj_kernel, n_q_heads=n_q_heads),
        grid=(m // tm,),
        in_specs=[row(d), const(g1), const(wa), const(wb), const(qg), const(kg), tab, tab],
        out_specs=[row(w) for w, _ in outs],
        out_shape=[jax.ShapeDtypeStruct((m, w), dt) for w, dt in outs],
        compiler_params=pltpu.CompilerParams(
            dimension_semantics=("arbitrary",), vmem_limit_bytes=VMEM_LIMIT),
        name="proj",
    )(x, g1, wa, wb, qg, kg, ra, ri)


def _pad_rows(x, n):
    if x.shape[0] == n:
        return x
    return jnp.concatenate([x, jnp.zeros((n - x.shape[0],) + x.shape[1:], x.dtype)], axis=0)


def _transpose_bf16(x):
    return jnp.transpose(x.astype(F32)).astype(BF16)


V_EXT_ROWS = A_HEAD_DIM + 16


def _attn_kernel(q_ref, qi_ref, wi_ref, k_ref, vt_ref, ki_ref, o_ref,
                 key_scr, tie_scr, qs_scr, qis_scr, m_scr, acc_scr, sa_scr, sb_scr, cm_scr, da_scr, db_scr,
                 *, tq, tqb, tk, causal, s_valid, n_sel, n_tiles, group):
    start = pl.program_id(1) * tqb
    nr = tk // SUBLANES
    n_acc = 8
    lane = lax.broadcasted_iota(I32, (1, tq), 1)
    rowi = lax.broadcasted_iota(I32, (tk, 1), 0)
    if causal:
        limit = (lax.shift_right_logical(start + lane, CHUNK_LOG2) + 1) * CHUNK
        nkt = jnp.minimum((start + tqb + tk - 1) // tk, n_tiles)
    else:
        limit = jnp.full((1, tq), s_valid, I32)
        nkt = n_tiles

    q = _pad_rows(q_ref[0], tq)
    for g in range(A_KV_HEADS):
        for r in range(group):
            h = g * group + r
            qs_scr[g, :, r * tq:(r + 1) * tq] = _transpose_bf16(q[:, h * LANES:(h + 1) * LANES])
    qi = _pad_rows(qi_ref[0], tq)
    for j in range(IDX_HEADS // 2):
        qis_scr[:, j * tq:(j + 1) * tq] = _transpose_bf16(qi[:, j * LANES:(j + 1) * LANES])
    w_t = jnp.transpose(_pad_rows(wi_ref[0], tq))[:IDX_HEADS] * (IDX_HEADS ** -0.5)

    d_bufs = (da_scr, db_scr)

    def stage_scores(tile, par):
        t_c = jnp.minimum(tile, nkt - 1)
        off = pl.multiple_of(t_c * tk, tk)
        kiab = ki_ref[0, pl.ds(off, tk), :]
        for half in range(IDX_HEADS // 4):
            qh = qis_scr[:, half * 2 * tq:(half + 1) * 2 * tq]
            for odd in range(2):
                d_bufs[par][2 * half + odd] = _dot(kiab[:, odd * LANES:(odd + 1) * LANES], qh)

    def finish_scores(t_c, par):
        acc = jnp.zeros((tk, tq), F32)
        for half in range(IDX_HEADS // 4):
            for odd in range(2):
                d = d_bufs[par][2 * half + odd]
                for jj in range(2):
                    h = 2 * (2 * half + jj) + odd
                    acc = acc + w_t[h:h + 1, :] * jnp.maximum(d[:, jj * tq:(jj + 1) * tq], 0.0)
        bits = lax.bitcast_convert_type(acc, I32)
        key = bits ^ (lax.shift_right_arithmetic(bits, 31) & 0x7FFFFFFF)
        key = jnp.where(key == -1, 0, key)
        key_scr[t_c] = jnp.where(t_c * tk + rowi < limit, key, INT_MIN)

    stage_scores(0, 0)

    def score_pair(i, carry):
        stage_scores(2 * i + 1, 1)
        finish_scores(2 * i, 0)
        stage_scores(2 * i + 2, 0)
        finish_scores(2 * i + 1, 1)
        return carry

    lax.fori_loop(0, nkt // 2, score_pair, 0)

    @pl.when(nkt % 2 == 1)
    def _():
        finish_scores(nkt - 1, 0)

    def count_ref(ref, pred):
        def body(kt, acc):
            hit = pred(ref[kt].reshape(nr, SUBLANES, tq))
            ones = jnp.where(hit, 1.0, 0.0).reshape(nr // n_acc, n_acc, SUBLANES, tq)
            return acc + jnp.sum(ones, axis=0)
        acc = lax.fori_loop(0, nkt, body, jnp.zeros((n_acc, SUBLANES, tq), F32))
        return jnp.sum(jnp.sum(acc, axis=0), axis=0, keepdims=True)

    def bit_step(i, carry):
        t_cur, c_cur = carry
        cand = t_cur + lax.shift_left(jnp.int32(1), 31 - i)
        cnt = count_ref(key_scr, lambda key: key >= cand[None])
        ok = cnt >= n_sel
        return jnp.where(ok, cand, t_cur), jnp.where(ok, cnt, c_cur)

    def any_lane(mask):
        return jnp.max(jnp.where(mask, 1.0, 0.0)) > 0.0

    valid = lane < tqb
    n_self = float(n_sel)

    def settle(t_cur, c_cur):
        pend = jnp.logical_and(valid, c_cur != n_self)
        return lax.cond(any_lane(pend), lambda: count_ref(key_scr, lambda key: key > t_cur[None]),
                        lambda: jnp.zeros((1, tq), F32))

    t_e, c_e = lax.fori_loop(
        0, SEARCH_EARLY_BITS, bit_step,
        (jnp.full((1, tq), INT_MIN, I32), jnp.where(limit < n_sel, n_self, UNSETTLED)))
    g_e = settle(t_e, c_e)
    unresolved = jnp.logical_and(jnp.logical_and(valid, c_e != n_self), g_e >= n_self)

    def finish_search():
        t_l, c_l = lax.fori_loop(SEARCH_EARLY_BITS, 32, bit_step, (t_e, c_e))
        return t_l, c_l, settle(t_l, c_l)

    t_fin, c_fin, c_gt = lax.cond(any_lane(unresolved), finish_search, lambda: (t_e, c_e, g_e))
    is_min = t_fin == INT_MIN
    needs = jnp.logical_and(jnp.logical_and(jnp.logical_not(is_min), c_fin > n_sel), valid)
    j_fast = jnp.where(is_min, 0, BIG_J)

    def tie_search():
        rem = n_self - c_gt

        def mark(kt, carry):
            tie_scr[kt] = jnp.where(key_scr[kt] == t_fin, kt * tk + rowi, BIG_J)
            return carry

        lax.fori_loop(0, nkt, mark, 0)
        j_bits = (n_tiles * tk).bit_length()

        def j_pending(carry):
            i, _, f_cur = carry
            return jnp.logical_and(i < j_bits, any_lane(jnp.logical_and(needs, f_cur != rem)))

        def j_step(carry):
            i, j_cur, f_cur = carry
            cand = j_cur + lax.shift_left(jnp.int32(1), j_bits - 1 - i)
            f = count_ref(tie_scr, lambda col: col < cand[None])
            ok = f <= rem
            return i + 1, jnp.where(ok, cand, j_cur), jnp.where(ok, f, f_cur)

        _, j_slow, _ = lax.while_loop(
            j_pending, j_step, (jnp.int32(0), jnp.zeros((1, tq), I32), jnp.zeros((1, tq), F32)))
        return jnp.where(needs, j_slow, j_fast)

    j_fin = lax.cond(any_lane(needs), tie_search, lambda: j_fast)

    m_scr[...] = jnp.full(m_scr.shape, NEG_BIG, F32)
    acc_scr[...] = jnp.zeros(acc_scr.shape, F32)

    s_bufs = (sa_scr, sb_scr)

    def stage_logits(tile, par):
        t_c = jnp.minimum(tile, nkt - 1)
        off = pl.multiple_of(t_c * tk, tk)
        key = key_scr[t_c]
        sel = jnp.logical_or(key > t_fin, jnp.logical_and(key == t_fin, off + rowi < j_fin))
        bias = jnp.where(sel, 0.0, -jnp.inf)
        bias = jnp.concatenate([bias] * group, axis=1)
        for g in range(A_KV_HEADS):
            s = _dot(k_ref[0, pl.ds(off, tk), g * LANES:(g + 1) * LANES], qs_scr[g]) + bias
            s_bufs[par][g] = s
            cm_scr[par, g] = jnp.max(s, axis=0, keepdims=True)

    def attend_tile(t_c, par):
        for g in range(A_KV_HEADS):
            m_old = m_scr[g]
            m_new = jnp.maximum(m_old, cm_scr[par, g])
            alpha = jnp.exp2(m_old - m_new)
            p = jnp.exp2(s_bufs[par][g] - m_new).astype(BF16)
            acc_scr[g] = alpha * acc_scr[g] + _dot(vt_ref[0, t_c, g], p)
            m_scr[g] = m_new

    stage_logits(0, 0)

    def attend_pair(i, carry):
        stage_logits(2 * i + 1, 1)
        attend_tile(2 * i, 0)
        stage_logits(2 * i + 2, 0)
        attend_tile(2 * i + 1, 1)
        return carry

    lax.fori_loop(0, nkt // 2, attend_pair, 0)

    @pl.when(nkt % 2 == 1)
    def _():
        attend_tile(nkt - 1, 0)
    for g in range(A_KV_HEADS):
        acc = acc_scr[g]
        out_t = acc[:A_HEAD_DIM] / acc[A_HEAD_DIM:A_HEAD_DIM + 1]
        for r in range(group):
            h = g * group + r
            o = jnp.transpose(out_t[:, r * tq:(r + 1) * tq])
            o_ref[0, :, h * LANES:(h + 1) * LANES] = o[:tqb].astype(BF16)


def _attn(q, qi, wi, kb, vb, kiab, *, tq, tk, causal, s_valid, n_sel):
    b, t, aw = q.shape
    tqb = min(tq, t)
    s_pad = kb.shape[1]
    n_tiles = s_pad // tk
    n_heads = aw // A_HEAD_DIM
    group = n_heads // A_KV_HEADS
    vt = jnp.transpose(vb.reshape(b, s_pad, A_KV_HEADS, A_HEAD_DIM), (0, 2, 3, 1))
    vt = jnp.concatenate([vt, jnp.ones((b, A_KV_HEADS, V_EXT_ROWS - A_HEAD_DIM, s_pad), BF16)], axis=2)
    vt = jnp.transpose(vt.reshape(b, A_KV_HEADS, V_EXT_ROWS, n_tiles, tk), (0, 3, 1, 2, 4))
    qspec = lambda w: pl.BlockSpec((1, tqb, w), lambda bi, qi_: (bi, qi_, 0))
    kspec = lambda w: pl.BlockSpec((1, s_pad, w), lambda bi, qi_: (bi, 0, 0))
    kern = functools.partial(_attn_kernel, tq=tq, tqb=tqb, tk=tk, causal=causal, s_valid=s_valid,
                             n_sel=n_sel, n_tiles=n_tiles, group=group)
    return pl.pallas_call(
        kern,
        grid=(b, t // tqb),
        in_specs=[qspec(aw), qspec(qi.shape[2]), qspec(LANES),
                  kspec(kb.shape[2]),
                  pl.BlockSpec((1,) + vt.shape[1:], lambda bi, qi_: (bi, 0, 0, 0, 0)),
                  kspec(kiab.shape[2])],
        out_specs=qspec(aw),
        out_shape=jax.ShapeDtypeStruct((b, t, aw), BF16),
        scratch_shapes=[
            pltpu.VMEM((n_tiles, tk, tq), I32),
            pltpu.VMEM((n_tiles, tk, tq), I32),
            pltpu.VMEM((A_KV_HEADS, LANES, group * tq), BF16),
            pltpu.VMEM((LANES, IDX_HEADS // 2 * tq), BF16),
            pltpu.VMEM((A_KV_HEADS, 1, group * tq), F32),
            pltpu.VMEM((A_KV_HEADS, V_EXT_ROWS, group * tq), F32),
            pltpu.VMEM((A_KV_HEADS, tk, group * tq), F32),
            pltpu.VMEM((A_KV_HEADS, tk, group * tq), F32),
            pltpu.VMEM((2, A_KV_HEADS, 1, group * tq), F32),
            pltpu.VMEM((IDX_HEADS // 2, tk, 2 * tq), F32),
            pltpu.VMEM((IDX_HEADS // 2, tk, 2 * tq), F32),
        ],
        compiler_params=pltpu.CompilerParams(
            dimension_semantics=("arbitrary", "arbitrary"), vmem_limit_bytes=VMEM_LIMIT),
        name="attn",
    )(q, qi, wi, kb, vt, kiab)


def _mlstm_kernel(u_ref, vm_ref, om_ref, ig_ref, fg_ref, c0_ref, n0_ref, m0_ref, cv0_ref,
                  cw_ref, cb_ref, wq_ref, wk_ref, wkt_ref, bg_ref, hg_ref, tri_ref,
                  mo_ref, c_out, n_out, m_out, cv_out,
                  c_scr, n_scr, m_scr, prev_scr, ubuf,
                  *, L, lb, n_heads):
    t = pl.program_id(1)
    nt = pl.num_programs(1)
    hd = M_HEAD_DIM
    pad = SUBLANES

    @pl.when(t == 0)
    def _():
        c_scr[...] = c0_ref[0]
        n_scr[...] = n0_ref[0]
        m_scr[...] = m0_ref[0]
        prev_scr[...] = cv0_ref[0]

    ubuf[0:pad, :] = prev_scr[...]
    ubuf[pad:pad + L, :] = _pad_rows(u_ref[...], L)
    uc = cb_ref[...]
    for i in range(CONV_W):
        uc = uc + ubuf[pad - (CONV_W - 1) + i:pad - (CONV_W - 1) + i + L, :] * cw_ref[i:i + 1, :]
    prev_scr[...] = ubuf[lb:lb + pad, :]
    uh = (uc * _sigmoid(uc)).astype(BF16)

    vm = _pad_rows(vm_ref[...], L)
    om = _pad_rows(om_ref[...], L)
    rowv = lax.broadcasted_iota(I32, (L, LANES), 0) < lb
    li = jnp.where(rowv, _pad_rows(ig_ref[...], L) + bg_ref[0:1, :], NEG_BIG)
    xf = _pad_rows(fg_ref[...], L) + bg_ref[1:2, :]
    lf = jnp.where(rowv, jnp.minimum(xf, 0.0) - jnp.log(1.0 + jnp.exp(-jnp.abs(xf))), 0.0)
    b = jnp.dot(tri_ref[...], lf, precision=lax.Precision.HIGHEST, preferred_element_type=F32)
    c = li - b
    c_t = c.T
    m_prev = m_scr[...]
    inter = b + m_prev
    b_last = b[L - 1:L, :]
    dec = b_last - b + li
    m_new = jnp.maximum(b_last + m_prev, jnp.max(dec, axis=0, keepdims=True))
    wts = jnp.exp(dec - m_new)
    wts_t = wts.T
    sc = jnp.exp(b_last + m_prev - m_new)
    causal = lax.broadcasted_iota(I32, (L, L), 0) >= lax.broadcasted_iota(I32, (L, L), 1)
    k_scale = hd ** -0.5

    heads = range(n_heads)
    sls = [slice(h * hd, (h + 1) * hd) for h in heads]
    q_l = [_dot(uh[:, sls[h]], wq_ref[h]) for h in heads]
    k_l = [_dot(uh[:, sls[h]], wk_ref[h]) * k_scale for h in heads]
    kt_l = [_dot_t(wkt_ref[h], uh[:, sls[h]]) * k_scale for h in heads]
    qb_l = [q.astype(BF16) for q in q_l]
    s_l = [_dot_t(qb_l[h], k_l[h].astype(BF16)) for h in heads]
    qc_l = [_dot(qb_l[h], c_scr[h].astype(BF16)) for h in heads]
    d_l = [jnp.where(causal, b[:, h:h + 1] + c_t[h:h + 1, :], -jnp.inf) for h in heads]
    mt_l = [jnp.maximum(inter[:, h:h + 1], jnp.max(d_l[h], axis=-1, keepdims=True)) for h in heads]
    sw_l = [jnp.exp(d_l[h] - mt_l[h]) * s_l[h] for h in heads]
    sp_l = [jnp.exp(inter[:, h:h + 1] - mt_l[h]) for h in heads]
    num_l = [sp_l[h] * qc_l[h] + _dot(sw_l[h].astype(BF16), vm[:, sls[h]]) for h in heads]
    den_l = [sp_l[h] * jnp.sum(q_l[h] * n_scr[h:h + 1, :], axis=-1, keepdims=True)
             + jnp.sum(sw_l[h], axis=-1, keepdims=True) for h in heads]
    kv_l = [_dot((kt_l[h] * wts_t[h:h + 1, :]).astype(BF16), vm[:, sls[h]]) for h in heads]
    hh_l = [num_l[h] / jnp.maximum(jnp.abs(den_l[h]), jnp.exp(-mt_l[h])) for h in heads]
    r_l = [lax.rsqrt(jnp.mean(hh * hh, axis=-1, keepdims=True) + EPS) for hh in hh_l]
    og_l = [_sigmoid(om[:, sls[h]]) for h in heads]
    for h in heads:
        mo_ref[:, sls[h]] = (og_l[h] * (hh_l[h] * r_l[h] * hg_ref[:, sls[h]]))[:lb].astype(BF16)
    for h in heads:
        sc_h = sc[:, h:h + 1]
        c_scr[h] = sc_h * c_scr[h] + kv_l[h]
        n_scr[h:h + 1, :] = (sc_h * n_scr[h:h + 1, :]
                             + jnp.sum(k_l[h] * wts[:, h:h + 1], axis=0, keepdims=True))
    m_scr[...] = m_new

    @pl.when(t == nt - 1)
    def _():
        c_out[0] = c_scr[...]
        n_out[0] = n_scr[...]
        m_out[0] = m_scr[...]
        cv_out[0] = prev_scr[...]


def _mlstm(u, vm, om, ig, fg, c0, n0, m0, cv0, cw, cb, wq, wk, wkt, bg, hg, tri, *, b, L, lb):
    m, mw = u.shape
    nt = m // (b * lb)
    n_heads = mw // M_HEAD_DIM
    row = lambda w: pl.BlockSpec((lb, w), lambda bi, ti: (bi * nt + ti, 0))
    const = lambda a: pl.BlockSpec(a.shape, lambda bi, ti: (0,) * a.ndim)
    perb = lambda a: pl.BlockSpec((1,) + a.shape[1:], lambda bi, ti: (bi,) + (0,) * (a.ndim - 1))
    out_shapes = [
        jax.ShapeDtypeStruct((m, mw), BF16),
        jax.ShapeDtypeStruct(c0.shape, F32),
        jax.ShapeDtypeStruct(n0.shape, F32),
        jax.ShapeDtypeStruct(m0.shape, F32),
        jax.ShapeDtypeStruct(cv0.shape, F32),
    ]
    return pl.pallas_call(
        functools.partial(_mlstm_kernel, L=L, lb=lb, n_heads=n_heads),
        grid=(b, nt),
        in_specs=[row(mw), row(mw), row(mw), row(LANES), row(LANES),
                  perb(c0), perb(n0), perb(m0), perb(cv0),
                  const(cw), const(cb), const(wq), const(wk), const(wkt), const(bg), const(hg), const(tri)],
        out_specs=[row(mw), perb(c0), perb(n0), perb(m0), perb(cv0)],
        out_shape=out_shapes,
        scratch_shapes=[
            pltpu.VMEM(c0.shape[1:], F32),
            pltpu.VMEM(n0.shape[1:], F32),
            pltpu.VMEM(m0.shape[1:], F32),
            pltpu.VMEM((SUBLANES, mw), F32),
            pltpu.VMEM((L + SUBLANES, mw), F32),
        ],
        compiler_params=pltpu.CompilerParams(
            dimension_semantics=("arbitrary", "arbitrary"), vmem_limit_bytes=VMEM_LIMIT),
        name="mlstm",
    )(u, vm, om, ig, fg, c0, n0, m0, cv0, cw, cb, wq, wk, wkt, bg, hg, tri)


def _ffn_kernel(x_ref, a_ref, mo_ref, wo_ref, g2_ref, wu_ref, wd_ref, y_ref, hn_scr):
    j = pl.program_id(1)

    @pl.when(j == 0)
    def _():
        aw = a_ref.shape[1]
        h = x_ref[...] + _dot(a_ref[...], wo_ref[:aw, :]) + _dot(mo_ref[...], wo_ref[aw:, :])
        y_ref[...] = h
        hn_scr[...] = _rms(h, g2_ref[...]).astype(BF16)

    f = jnp.maximum(_dot(hn_scr[...], wu_ref[...]), 0.0)
    y_ref[...] += _dot((f * f).astype(BF16), wd_ref[...])


def _ffn(x, a, mo, wo, g2, wu, wd, *, tm, tf):
    m, d = x.shape
    dff = wu.shape[1]
    row = lambda w: pl.BlockSpec((tm, w), lambda i, j: (i, 0))
    const = lambda arr: pl.BlockSpec(arr.shape, lambda i, j: (0, 0), pipeline_mode=pl.Buffered(1))
    return pl.pallas_call(
        _ffn_kernel,
        grid=(m // tm, dff // tf),
        in_specs=[row(d), row(a.shape[1]), row(mo.shape[1]), const(wo), const(g2),
                  pl.BlockSpec((d, tf), lambda i, j: (0, j)),
                  pl.BlockSpec((tf, d), lambda i, j: (j, 0))],
        out_specs=row(d),
        out_shape=jax.ShapeDtypeStruct((m, d), F32),
        scratch_shapes=[pltpu.VMEM((tm, d), BF16)],
        compiler_params=pltpu.CompilerParams(
            dimension_semantics=("arbitrary", "arbitrary"), vmem_limit_bytes=VMEM_LIMIT),
        name="ffn",
    )(x, a, mo, wo, g2, wu, wd)


def _rope_tables(pos, rot, width):
    half = rot // 2
    inv_freq = ROPE_THETA ** (-jnp.arange(half, dtype=F32) / half)
    ang = pos.astype(F32)[:, None] * inv_freq[None, :]
    cos, sin = jnp.cos(ang), jnp.sin(ang)
    n = pos.shape[0]
    rest1 = jnp.ones((n, width - rot), F32)
    rest0 = jnp.zeros((n, width - rot), F32)
    z = jnp.zeros_like(sin)
    c = jnp.concatenate([cos, cos, rest1], axis=1)
    sa = jnp.concatenate([-sin, z, rest0], axis=1)
    sb = jnp.concatenate([z, sin, rest0], axis=1)
    rep = LANES // width
    return jnp.concatenate([jnp.tile(c, (1, rep)), jnp.tile(sa, (1, rep)), jnp.tile(sb, (1, rep))], axis=1)


def _pad_cols(w, n):
    return jnp.pad(w, ((0, 0), (0, n - w.shape[1])))


def _tile_plan(n_rows, t, n_keys, prompt):
    if prompt:
        return dict(tm_proj=256, tm_ffn=512, tf=1024, tq=LANES, tk=512, L=128, lb=128)
    return dict(tm_proj=n_rows, tm_ffn=n_rows, tf=512, tq=LANES, tk=-(-n_keys // LANES) * LANES, L=128, lb=t)


def _layer(x, pos_rows, mode, state, params, splits):
    (norm1_g, w_in, q_norm_g, k_norm_g, conv_w, conv_b, wq_m, wk_m, b_igate, b_fgate,
     hnorm_g, w_out, norm2_g, w_up, w_down) = params
    bsz, t, d = x.shape
    m = bsz * t
    n_keys = t if mode == "prompt" else state[0].shape[1] + t
    plan = _tile_plan(m, t, n_keys, mode == "prompt")
    tm_proj, tm_ffn, tf, tq, tk, L, lb = (plan[k] for k in ("tm_proj", "tm_ffn", "tf", "tq", "tk", "L", "lb"))
    x2 = x.reshape(m, d)
    offs = np.cumsum((0,) + tuple(splits))
    col = lambda i: w_in[:, int(offs[i]):int(offs[i + 1])]
    aw, mw = splits[0], splits[6]
    n_q_heads = aw // A_HEAD_DIM
    n_m_heads = mw // M_HEAD_DIM
    wa = jnp.concatenate(
        [col(0), col(1), col(2), col(3), col(4), col(4),
         _pad_cols(col(5), LANES), _pad_cols(col(9), LANES), _pad_cols(col(10), LANES)], axis=1).astype(BF16)
    wb = jnp.concatenate([col(6), col(7), col(8)], axis=1).astype(BF16)
    ra = _rope_tables(pos_rows, A_HEAD_DIM // 4, A_HEAD_DIM)
    ri = _rope_tables(pos_rows, IDX_DIM // 4, IDX_DIM)

    (q, k, v, ki, kb, vb, kiab, qi, wi, ig, fg, u, vm, om) = _proj(
        x2, norm1_g[None], wa, wb, q_norm_g[None], k_norm_g[None], ra, ri,
        tm=tm_proj, n_q_heads=n_q_heads, m_width=mw)

    kvw = A_KV_HEADS * A_HEAD_DIM
    r3 = lambda a_: a_.reshape(bsz, t, a_.shape[1])
    if mode == "prompt":
        c0 = jnp.zeros((bsz, n_m_heads, M_HEAD_DIM, M_HEAD_DIM), F32)
        n0 = jnp.zeros((bsz, n_m_heads, M_HEAD_DIM), F32)
        m0 = jnp.zeros((bsz, 1, LANES), F32)
        cv0 = jnp.zeros((bsz, SUBLANES, mw), F32)
        n_sel = min(TOPK_MAX, t // 4)
        a = _attn(r3(q), r3(qi), r3(wi), r3(kb), r3(vb), r3(kiab),
                  tq=tq, tk=tk, causal=True, s_valid=t, n_sel=n_sel)
    else:
        ck, cv, ckidx, s_c, s_n, s_m, s_conv = state
        p = ck.shape[1]
        s_valid = p + t
        s_pad = -(-s_valid // tk) * tk
        padk = lambda a_: jnp.pad(a_, ((0, 0), (0, s_pad - s_valid), (0, 0)))
        k_all = padk(jnp.concatenate([ck.reshape(bsz, p, kvw).astype(BF16), r3(kb)], axis=1))
        v_all = padk(jnp.concatenate([cv.reshape(bsz, p, kvw).astype(BF16), r3(vb)], axis=1))
        cki = ckidx.astype(BF16)
        zki = jnp.zeros_like(cki)
        ki_c = jnp.concatenate([cki, zki, zki, cki], axis=2)
        ki_all = padk(jnp.concatenate([ki_c, r3(kiab)], axis=1))
        c0, n0 = s_c.astype(F32), s_n.astype(F32)
        m0 = jnp.pad(s_m.astype(F32), ((0, 0), (0, LANES - n_m_heads)))[:, None, :]
        cv0 = jnp.pad(s_conv.astype(F32), ((0, 0), (SUBLANES - (CONV_W - 1), 0), (0, 0)))
        n_sel = min(TOPK_MAX, s_valid // 4)
        a = _attn(r3(q), r3(qi), r3(wi), k_all, v_all, ki_all,
                  tq=tq, tk=tk, causal=False, s_valid=s_valid, n_sel=n_sel)

    bg = jnp.stack([_pad_cols(b_igate[None], LANES)[0], _pad_cols(b_fgate[None], LANES)[0]])
    tri = (np.arange(L)[:, None] >= np.arange(L)[None, :]).astype(np.float32)
    mo, c_new, n_new, m_new, cv_new = _mlstm(
        u, vm, om, ig, fg, c0, n0, m0, cv0,
        conv_w, conv_b[None], wq_m.astype(BF16), wk_m.astype(BF16),
        jnp.swapaxes(wk_m, 1, 2).astype(BF16), bg, hnorm_g[None], jnp.asarray(tri),
        b=bsz, L=L, lb=lb)

    y = _ffn(x2, a.reshape(m, aw), mo, w_out.astype(BF16), norm2_g[None],
             w_up.astype(BF16), w_down.astype(BF16), tm=tm_ffn, tf=tf)
    new_state = (
        k.reshape(bsz, t, A_KV_HEADS, A_HEAD_DIM), v.reshape(bsz, t, A_KV_HEADS, A_HEAD_DIM),
        ki.reshape(bsz, t, IDX_DIM), c_new, n_new, m_new[:, 0, :n_m_heads],
        cv_new[:, SUBLANES - (CONV_W - 1):, :])
    return y.reshape(bsz, t, d), new_state


def kernel(x_prompt, x_sample, cache_k, cache_v, cache_kidx, state_mlstm_C, state_mlstm_n, state_mlstm_m,
           state_conv, norm1_g, w_in, q_norm_g, k_norm_g, conv_w, conv_b, wq_m, wk_m, b_igate, b_fgate,
           hnorm_g, w_out, norm2_g, w_up, w_down):
    depth = w_in.shape[0]
    d = x_prompt.shape[-1]
    mix = w_out.shape[1]
    aw = mix // 2
    mw = mix - aw
    kvw = A_KV_HEADS * A_HEAD_DIM
    n_m_heads = mw // M_HEAD_DIM
    splits = (aw, kvw, kvw, IDX_HEADS * IDX_DIM, IDX_DIM, IDX_HEADS, mw, mw, mw, n_m_heads, n_m_heads)
    t_p = x_prompt.shape[1]
    b_s, t_s = x_sample.shape[0], x_sample.shape[1]
    past = cache_k.shape[2]

    y_p, y_s = x_prompt, x_sample
    new_p, new_s = [], []
    for l in range(depth):
        params = (norm1_g[l], w_in[l], q_norm_g[l], k_norm_g[l], conv_w[l], conv_b[l], wq_m[l], wk_m[l],
                  b_igate[l], b_fgate[l], hnorm_g[l], w_out[l], norm2_g[l], w_up[l], w_down[l])
        y_p, st_p = _layer(
            y_p, jnp.arange(t_p), "prompt", None, params, splits)
        y_s, st_s = _layer(
            y_s, past + (jnp.arange(b_s * t_s) % t_s), "sample",
            (cache_k[l], cache_v[l], cache_kidx[l], state_mlstm_C[l], state_mlstm_n[l],
             state_mlstm_m[l], state_conv[l]), params, splits)
        new_p.append(st_p)
        new_s.append(st_s)

    stack = lambda states, i: jnp.stack([s[i] for s in states])
    return (y_p, y_s,
            *[stack(new_p, i) for i in range(7)],
            *[stack(new_s, i) for i in range(7)])
```

```python
import functools

import jax
import jax.numpy as jnp
import numpy as np
from jax import lax
from jax.experimental import pallas as pl
from jax.experimental.pallas import tpu as pltpu

F32 = jnp.float32
BF16 = jnp.bfloat16
I32 = jnp.int32

EPS = 1e-6
ROPE_THETA = 500000.0
CHUNK = 64
CHUNK_LOG2 = 6
TOPK_MAX = 256
A_HEAD_DIM = 128
A_KV_HEADS = 2
IDX_HEADS = 8
IDX_DIM = 64
M_HEAD_DIM = 128
CONV_W = 4

LANES = 128
SUBLANES = 8
VMEM_LIMIT = 56 * 1024 * 1024
NEG_BIG = -1e30
LOG2_E = 1.4426950408889634
INT_MIN = -(2 ** 31)
BIG_J = 2 ** 30
UNSETTLED = 1e9
SEARCH_EARLY_BITS = 26


def _rms(x, g):
    r = lax.rsqrt(jnp.mean(x * x, axis=-1, keepdims=True) + EPS)
    return x * r * g


def _rope(t, c, sa, sb, half):
    return t * c + pltpu.roll(t, LANES - half, 1) * sa + pltpu.roll(t, half, 1) * sb


def _sigmoid(x):
    return 1.0 / (1.0 + jnp.exp(-x))


def _dot(a, b):
    return jnp.dot(a, b, preferred_element_type=F32)


def _dot_t(a, b):
    return lax.dot_general(a, b, (((1,), (1,)), ((), ())), preferred_element_type=F32)


def _proj_kernel(x_ref, g1_ref, wa_ref, wb_ref, qg_ref, kg_ref, ra_ref, ri_ref,
                 q_out, k_out, v_out, ki_out, kb_out, vb_out, kiab_out, qi_out,
                 wi_out, ig_out, fg_out, u_out, vm_out, om_out, *, n_q_heads):
    xn = _rms(x_ref[...], g1_ref[...]).astype(BF16)
    ra = ra_ref[...]
    ca, saa, sba = ra[:, :LANES], ra[:, LANES:2 * LANES], ra[:, 2 * LANES:]
    ri = ri_ref[...]
    ci, sai, sbi = ri[:, :LANES], ri[:, LANES:2 * LANES], ri[:, 2 * LANES:]
    qg = qg_ref[...]
    kg = kg_ref[...]
    aw = n_q_heads * A_HEAD_DIM
    kvw = A_KV_HEADS * A_HEAD_DIM
    q_scale = A_HEAD_DIM ** -0.5 * LOG2_E
    i_scale = IDX_DIM ** -0.5

    for c in range(0, aw, 512):
        z = _dot(xn, wa_ref[:, c:c + 512])
        for j in range(0, 512, LANES):
            qh = _rope(_rms(z[:, j:j + LANES], qg), ca, saa, sba, A_HEAD_DIM // 8) * q_scale
            q_out[:, c + j:c + j + LANES] = qh.astype(BF16)
    off = aw
    z = _dot(xn, wa_ref[:, off:off + 2 * kvw])
    for j in range(0, kvw, LANES):
        kh = _rope(_rms(z[:, j:j + LANES], kg), ca, saa, sba, A_HEAD_DIM // 8)
        k_out[:, j:j + LANES] = kh
        kb_out[:, j:j + LANES] = kh.astype(BF16)
    vv = z[:, kvw:]
    v_out[...] = vv
    vb_out[...] = vv.astype(BF16)
    off += 2 * kvw
    iw = IDX_HEADS * IDX_DIM
    z = _dot(xn, wa_ref[:, off:off + iw])
    for j in range(0, iw, LANES):
        qi = _rope(z[:, j:j + LANES], ci, sai, sbi, IDX_DIM // 8) * i_scale
        qi_out[:, j:j + LANES] = qi.astype(BF16)
    off += iw
    z = _dot(xn, wa_ref[:, off:off + 4 * LANES])
    ki2 = _rope(z[:, :LANES], ci, sai, sbi, IDX_DIM // 8)
    ki_out[...] = ki2[:, :IDX_DIM]
    lane = lax.broadcasted_iota(I32, (1, LANES), 1)
    lo = lane < IDX_DIM
    kiab_out[:, :LANES] = jnp.where(lo, ki2, 0.0).astype(BF16)
    kiab_out[:, LANES:] = jnp.where(lo, 0.0, ki2).astype(BF16)
    wi_out[...] = z[:, LANES:2 * LANES]
    ig_out[...] = z[:, 2 * LANES:3 * LANES]
    fg_out[...] = z[:, 3 * LANES:]
    mw = u_out.shape[1]
    for c in range(0, mw, 512):
        u_out[:, c:c + 512] = _dot(xn, wb_ref[:, c:c + 512])
        vm_out[:, c:c + 512] = _dot(xn, wb_ref[:, mw + c:mw + c + 512]).astype(BF16)
        om_out[:, c:c + 512] = _dot(xn, wb_ref[:, 2 * mw + c:2 * mw + c + 512])


def _proj(x, g1, wa, wb, qg, kg, ra, ri, *, tm, n_q_heads, m_width):
    m, d = x.shape
    n_tab = ra.shape[0] // tm
    aw = n_q_heads * A_HEAD_DIM
    kvw = A_KV_HEADS * A_HEAD_DIM
    iw = IDX_HEADS * IDX_DIM
    row = lambda w: pl.BlockSpec((tm, w), lambda i: (i, 0))
    const = lambda a: pl.BlockSpec(a.shape, lambda i: (0, 0), pipeline_mode=pl.Buffered(1))
    tab = pl.BlockSpec((tm, 3 * LANES), lambda i: (i % n_tab, 0))
    outs = [
        (aw, BF16), (kvw, F32), (kvw, F32), (IDX_DIM, F32), (kvw, BF16), (kvw, BF16),
        (2 * LANES, BF16), (iw, BF16), (LANES, F32), (LANES, F32), (LANES, F32),
        (m_width, F32), (m_width, BF16), (m_width, F32),
    ]
    return pl.pallas_call(
        functools.partial(_proj_kernel, n_q_heads=n_q_heads),
        grid=(m // tm,),
        in_specs=[row(d), const(g1), const(wa), const(wb), const(qg), const(kg), tab, tab],
        out_specs=[row(w) for w, _ in outs],
        out_shape=[jax.ShapeDtypeStruct((m, w), dt) for w, dt in outs],
        compiler_params=pltpu.CompilerParams(
            dimension_semantics=("arbitrary",), vmem_limit_bytes=VMEM_LIMIT),
        name="proj",
    )(x, g1, wa, wb, qg, kg, ra, ri)


def _pad_rows(x, n):
    if x.shape[0] == n:
        return x
    return jnp.concatenate([x, jnp.zeros((n - x.shape[0],) + x.shape[1:], x.dtype)], axis=0)


def _transpose_bf16(x):
    return jnp.transpose(x.astype(F32)).astype(BF16)


V_EXT_ROWS = A_HEAD_DIM + 16


def _attn_kernel(q_ref, qi_ref, wi_ref, k_ref, vt_ref, ki_ref, o_ref,
                 key_scr, tie_scr, qs_scr, qis_scr, m_scr, acc_scr, sa_scr, sb_scr, cm_scr, da_scr, db_scr,
                 *, tq, tqb, tk, causal, s_valid, n_sel, n_tiles, group):
    start = pl.program_id(1) * tqb
    nr = tk // SUBLANES
    n_acc = 8
    lane = lax.broadcasted_iota(I32, (1, tq), 1)
    rowi = lax.broadcasted_iota(I32, (tk, 1), 0)
    if causal:
        limit = (lax.shift_right_logical(start + lane, CHUNK_LOG2) + 1) * CHUNK
        nkt = jnp.minimum((start + tqb + tk - 1) // tk, n_tiles)
    else:
        limit = jnp.full((1, tq), s_valid, I32)
        nkt = n_tiles

    q = _pad_rows(q_ref[0], tq)
    for g in range(A_KV_HEADS):
        for r in range(group):
            h = g * group + r
            qs_scr[g, :, r * tq:(r + 1) * tq] = _transpose_bf16(q[:, h * LANES:(h + 1) * LANES])
    qi = _pad_rows(qi_ref[0], tq)
    for j in range(IDX_HEADS // 2):
        qis_scr[:, j * tq:(j + 1) * tq] = _transpose_bf16(qi[:, j * LANES:(j + 1) * LANES])
    w_t = jnp.transpose(_pad_rows(wi_ref[0], tq))[:IDX_HEADS] * (IDX_HEADS ** -0.5)

    d_bufs = (da_scr, db_scr)

    def stage_scores(tile, par):
        t_c = jnp.minimum(tile, nkt - 1)
        off = pl.multiple_of(t_c * tk, tk)
        kiab = ki_ref[0, pl.ds(off, tk), :]
        for half in range(IDX_HEADS // 4):
            qh = qis_scr[:, half * 2 * tq:(half + 1) * 2 * tq]
            for odd in range(2):
                d_bufs[par][2 * half + odd] = _dot(kiab[:, odd * LANES:(odd + 1) * LANES], qh)

    def finish_scores(t_c, par):
        acc = jnp.zeros((tk, tq), F32)
        for half in range(IDX_HEADS // 4):
            for odd in range(2):
                d = d_bufs[par][2 * half + odd]
                for jj in range(2):
                    h = 2 * (2 * half + jj) + odd
                    acc = acc + w_t[h:h + 1, :] * jnp.maximum(d[:, jj * tq:(jj + 1) * tq], 0.0)
        bits = lax.bitcast_convert_type(acc, I32)
        key = bits ^ (lax.shift_right_arithmetic(bits, 31) & 0x7FFFFFFF)
        key = jnp.where(key == -1, 0, key)
        key_scr[t_c] = jnp.where(t_c * tk + rowi < limit, key, INT_MIN)

    stage_scores(0, 0)

    def score_pair(i, carry):
        stage_scores(2 * i + 1, 1)
        finish_scores(2 * i, 0)
        stage_scores(2 * i + 2, 0)
        finish_scores(2 * i + 1, 1)
        return carry

    lax.fori_loop(0, nkt // 2, score_pair, 0)

    @pl.when(nkt % 2 == 1)
    def _():
        finish_scores(nkt - 1, 0)

    def count_ref(ref, pred):
        def body(kt, acc):
            hit = pred(ref[kt].reshape(nr, SUBLANES, tq))
            ones = jnp.where(hit, 1.0, 0.0).reshape(nr // n_acc, n_acc, SUBLANES, tq)
            return acc + jnp.sum(ones, axis=0)
        acc = lax.fori_loop(0, nkt, body, jnp.zeros((n_acc, SUBLANES, tq), F32))
        return jnp.sum(jnp.sum(acc, axis=0), axis=0, keepdims=True)

    def bit_step(i, carry):
        t_cur, c_cur = carry
        cand = t_cur + lax.shift_left(jnp.int32(1), 31 - i)
        cnt = count_ref(key_scr, lambda key: key >= cand[None])
        ok = cnt >= n_sel
        return jnp.where(ok, cand, t_cur), jnp.where(ok, cnt, c_cur)

    def any_lane(mask):
        return jnp.max(jnp.where(mask, 1.0, 0.0)) > 0.0

    valid = lane < tqb
    n_self = float(n_sel)

    def settle(t_cur, c_cur):
        pend = jnp.logical_and(valid, c_cur != n_self)
        return lax.cond(any_lane(pend), lambda: count_ref(key_scr, lambda key: key > t_cur[None]),
                        lambda: jnp.zeros((1, tq), F32))

    t_e, c_e = lax.fori_loop(
        0, SEARCH_EARLY_BITS, bit_step,
        (jnp.full((1, tq), INT_MIN, I32), jnp.where(limit < n_sel, n_self, UNSETTLED)))
    g_e = settle(t_e, c_e)
    unresolved = jnp.logical_and(jnp.logical_and(valid, c_e != n_self), g_e >= n_self)

    def finish_search():
        t_l, c_l = lax.fori_loop(SEARCH_EARLY_BITS, 32, bit_step, (t_e, c_e))
        return t_l, c_l, settle(t_l, c_l)

    t_fin, c_fin, c_gt = lax.cond(any_lane(unresolved), finish_search, lambda: (t_e, c_e, g_e))
    is_min = t_fin == INT_MIN
    needs = jnp.logical_and(jnp.logical_and(jnp.logical_not(is_min), c_fin > n_sel), valid)
    j_fast = jnp.where(is_min, 0, BIG_J)

    def tie_search():
        rem = n_self - c_gt

        def mark(kt, carry):
            tie_scr[kt] = jnp.where(key_scr[kt] == t_fin, kt * tk + rowi, BIG_J)
            return carry

        lax.fori_loop(0, nkt, mark, 0)
        j_bits = (n_tiles * tk).bit_length()

        def j_pending(carry):
            i, _, f_cur = carry
            return jnp.logical_and(i < j_bits, any_lane(jnp.logical_and(needs, f_cur != rem)))

        def j_step(carry):
            i, j_cur, f_cur = carry
            cand = j_cur + lax.shift_left(jnp.int32(1), j_bits - 1 - i)
            f = count_ref(tie_scr, lambda col: col < cand[None])
            ok = f <= rem
            return i + 1, jnp.where(ok, cand, j_cur), jnp.where(ok, f, f_cur)

        _, j_slow, _ = lax.while_loop(
            j_pending, j_step, (jnp.int32(0), jnp.zeros((1, tq), I32), jnp.zeros((1, tq), F32)))
        return jnp.where(needs, j_slow, j_fast)

    j_fin = lax.cond(any_lane(needs), tie_search, lambda: j_fast)

    m_scr[...] = jnp.full(m_scr.shape, NEG_BIG, F32)
    acc_scr[...] = jnp.zeros(acc_scr.shape, F32)

    s_bufs = (sa_scr, sb_scr)

    def stage_logits(tile, par):
        t_c = jnp.minimum(tile, nkt - 1)
        off = pl.multiple_of(t_c * tk, tk)
        key = key_scr[t_c]
        sel = jnp.logical_or(key > t_fin, jnp.logical_and(key == t_fin, off + rowi < j_fin))
        bias = jnp.where(sel, 0.0, -jnp.inf)
        bias = jnp.concatenate([bias] * group, axis=1)
        for g in range(A_KV_HEADS):
            s = _dot(k_ref[0, pl.ds(off, tk), g * LANES:(g + 1) * LANES], qs_scr[g]) + bias
            s_bufs[par][g] = s
            cm_scr[par, g] = jnp.max(s, axis=0, keepdims=True)

    def attend_tile(t_c, par):
        for g in range(A_KV_HEADS):
            m_old = m_scr[g]
            m_new = jnp.maximum(m_old, cm_scr[par, g])
            alpha = jnp.exp2(m_old - m_new)
            p = jnp.exp2(s_bufs[par][g] - m_new).astype(BF16)
            acc_scr[g] = alpha * acc_scr[g] + _dot(vt_ref[0, t_c, g], p)
            m_scr[g] = m_new

    stage_logits(0, 0)

    def attend_pair(i, carry):
        stage_logits(2 * i + 1, 1)
        attend_tile(2 * i, 0)
        stage_logits(2 * i + 2, 0)
        attend_tile(2 * i + 1, 1)
        return carry

    lax.fori_loop(0, (nkt - 1) // 2, attend_pair, 0)

    @pl.when(nkt % 2 == 1)
    def _():
        attend_tile(nkt - 1, 0)

    @pl.when(nkt % 2 == 0)
    def _():
        stage_logits(nkt - 1, 1)
        attend_tile(nkt - 2, 0)
        attend_tile(nkt - 1, 1)
    for g in range(A_KV_HEADS):
        acc = acc_scr[g]
        out_t = acc[:A_HEAD_DIM] / acc[A_HEAD_DIM:A_HEAD_DIM + 1]
        for r in range(group):
            h = g * group + r
            o = jnp.transpose(out_t[:, r * tq:(r + 1) * tq])
            o_ref[0, :, h * LANES:(h + 1) * LANES] = o[:tqb].astype(BF16)


def _attn(q, qi, wi, kb, vb, kiab, *, tq, tk, causal, s_valid, n_sel):
    b, t, aw = q.shape
    tqb = min(tq, t)
    s_pad = kb.shape[1]
    n_tiles = s_pad // tk
    n_heads = aw // A_HEAD_DIM
    group = n_heads // A_KV_HEADS
    vt = jnp.transpose(vb.reshape(b, s_pad, A_KV_HEADS, A_HEAD_DIM), (0, 2, 3, 1))
    vt = jnp.concatenate([vt, jnp.ones((b, A_KV_HEADS, V_EXT_ROWS - A_HEAD_DIM, s_pad), BF16)], axis=2)
    vt = jnp.transpose(vt.reshape(b, A_KV_HEADS, V_EXT_ROWS, n_tiles, tk), (0, 3, 1, 2, 4))
    qspec = lambda w: pl.BlockSpec((1, tqb, w), lambda bi, qi_: (bi, qi_, 0))
    kspec = lambda w: pl.BlockSpec((1, s_pad, w), lambda bi, qi_: (bi, 0, 0))
    kern = functools.partial(_attn_kernel, tq=tq, tqb=tqb, tk=tk, causal=causal, s_valid=s_valid,
                             n_sel=n_sel, n_tiles=n_tiles, group=group)
    return pl.pallas_call(
        kern,
        grid=(b, t // tqb),
        in_specs=[qspec(aw), qspec(qi.shape[2]), qspec(LANES),
                  kspec(kb.shape[2]),
                  pl.BlockSpec((1,) + vt.shape[1:], lambda bi, qi_: (bi, 0, 0, 0, 0)),
                  kspec(kiab.shape[2])],
        out_specs=qspec(aw),
        out_shape=jax.ShapeDtypeStruct((b, t, aw), BF16),
        scratch_shapes=[
            pltpu.VMEM((n_tiles, tk, tq), I32),
            pltpu.VMEM((n_tiles, tk, tq), I32),
            pltpu.VMEM((A_KV_HEADS, LANES, group * tq), BF16),
            pltpu.VMEM((LANES, IDX_HEADS // 2 * tq), BF16),
            pltpu.VMEM((A_KV_HEADS, 1, group * tq), F32),
            pltpu.VMEM((A_KV_HEADS, V_EXT_ROWS, group * tq), F32),
            pltpu.VMEM((A_KV_HEADS, tk, group * tq), F32),
            pltpu.VMEM((A_KV_HEADS, tk, group * tq), F32),
            pltpu.VMEM((2, A_KV_HEADS, 1, group * tq), F32),
            pltpu.VMEM((IDX_HEADS // 2, tk, 2 * tq), F32),
            pltpu.VMEM((IDX_HEADS // 2, tk, 2 * tq), F32),
        ],
        compiler_params=pltpu.CompilerParams(
            dimension_semantics=("arbitrary", "arbitrary"), vmem_limit_bytes=VMEM_LIMIT),
        name="attn",
    )(q, qi, wi, kb, vt, kiab)


def _mlstm_kernel(u_ref, vm_ref, om_ref, ig_ref, fg_ref, c0_ref, n0_ref, m0_ref, cv0_ref,
                  cw_ref, cb_ref, wq_ref, wk_ref, wkt_ref, bg_ref, hg_ref, tri_ref,
                  mo_ref, c_out, n_out, m_out, cv_out,
                  c_scr, n_scr, m_scr, prev_scr, ubuf,
                  *, L, lb, n_heads):
    t = pl.program_id(1)
    nt = pl.num_programs(1)
    hd = M_HEAD_DIM
    pad = SUBLANES

    @pl.when(t == 0)
    def _():
        c_scr[...] = c0_ref[0]
        n_scr[...] = n0_ref[0]
        m_scr[...] = m0_ref[0]
        prev_scr[...] = cv0_ref[0]

    ubuf[0:pad, :] = prev_scr[...]
    ubuf[pad:pad + L, :] = _pad_rows(u_ref[...], L)
    uc = cb_ref[...]
    for i in range(CONV_W):
        uc = uc + ubuf[pad - (CONV_W - 1) + i:pad - (CONV_W - 1) + i + L, :] * cw_ref[i:i + 1, :]
    prev_scr[...] = ubuf[lb:lb + pad, :]
    uh = (uc * _sigmoid(uc)).astype(BF16)

    vm = _pad_rows(vm_ref[...], L)
    om = _pad_rows(om_ref[...], L)
    rowv = lax.broadcasted_iota(I32, (L, LANES), 0) < lb
    li = jnp.where(rowv, _pad_rows(ig_ref[...], L) + bg_ref[0:1, :], NEG_BIG)
    xf = _pad_rows(fg_ref[...], L) + bg_ref[1:2, :]
    lf = jnp.where(rowv, jnp.minimum(xf, 0.0) - jnp.log(1.0 + jnp.exp(-jnp.abs(xf))), 0.0)
    b = jnp.dot(tri_ref[...], lf, precision=lax.Precision.HIGHEST, preferred_element_type=F32)
    c = li - b
    c_t = c.T
    m_prev = m_scr[...]
    inter = b + m_prev
    b_last = b[L - 1:L, :]
    dec = b_last - b + li
    m_new = jnp.maximum(b_last + m_prev, jnp.max(dec, axis=0, keepdims=True))
    wts = jnp.exp(dec - m_new)
    wts_t = wts.T
    sc = jnp.exp(b_last + m_prev - m_new)
    causal = lax.broadcasted_iota(I32, (L, L), 0) >= lax.broadcasted_iota(I32, (L, L), 1)
    k_scale = hd ** -0.5

    heads = range(n_heads)
    sls = [slice(h * hd, (h + 1) * hd) for h in heads]
    q_l = [_dot(uh[:, sls[h]], wq_ref[h]) for h in heads]
    k_l = [_dot(uh[:, sls[h]], wk_ref[h]) * k_scale for h in heads]
    kt_l = [_dot_t(wkt_ref[h], uh[:, sls[h]]) * k_scale for h in heads]
    qb_l = [q.astype(BF16) for q in q_l]
    s_l = [_dot_t(qb_l[h], k_l[h].astype(BF16)) for h in heads]
    qc_l = [_dot(qb_l[h], c_scr[h].astype(BF16)) for h in heads]
    d_l = [jnp.where(causal, b[:, h:h + 1] + c_t[h:h + 1, :], -jnp.inf) for h in heads]
    mt_l = [jnp.maximum(inter[:, h:h + 1], jnp.max(d_l[h], axis=-1, keepdims=True)) for h in heads]
    sw_l = [jnp.exp(d_l[h] - mt_l[h]) * s_l[h] for h in heads]
    sp_l = [jnp.exp(inter[:, h:h + 1] - mt_l[h]) for h in heads]
    num_l = [sp_l[h] * qc_l[h] + _dot(sw_l[h].astype(BF16), vm[:, sls[h]]) for h in heads]
    den_l = [sp_l[h] * jnp.sum(q_l[h] * n_scr[h:h + 1, :], axis=-1, keepdims=True)
             + jnp.sum(sw_l[h], axis=-1, keepdims=True) for h in heads]
    kv_l = [_dot((kt_l[h] * wts_t[h:h + 1, :]).astype(BF16), vm[:, sls[h]]) for h in heads]
    hh_l = [num_l[h] / jnp.maximum(jnp.abs(den_l[h]), jnp.exp(-mt_l[h])) for h in heads]
    r_l = [lax.rsqrt(jnp.mean(hh * hh, axis=-1, keepdims=True) + EPS) for hh in hh_l]
    og_l = [_sigmoid(om[:, sls[h]]) for h in heads]
    for h in heads:
        mo_ref[:, sls[h]] = (og_l[h] * (hh_l[h] * r_l[h] * hg_ref[:, sls[h]]))[:lb].astype(BF16)
    for h in heads:
        sc_h = sc[:, h:h + 1]
        c_scr[h] = sc_h * c_scr[h] + kv_l[h]
        n_scr[h:h + 1, :] = (sc_h * n_scr[h:h + 1, :]
                             + jnp.sum(k_l[h] * wts[:, h:h + 1], axis=0, keepdims=True))
    m_scr[...] = m_new

    @pl.when(t == nt - 1)
    def _():
        c_out[0] = c_scr[...]
        n_out[0] = n_scr[...]
        m_out[0] = m_scr[...]
        cv_out[0] = prev_scr[...]


def _mlstm(u, vm, om, ig, fg, c0, n0, m0, cv0, cw, cb, wq, wk, wkt, bg, hg, tri, *, b, L, lb):
    m, mw = u.shape
    nt = m // (b * lb)
    n_heads = mw // M_HEAD_DIM
    row = lambda w: pl.BlockSpec((lb, w), lambda bi, ti: (bi * nt + ti, 0))
    const = lambda a: pl.BlockSpec(a.shape, lambda bi, ti: (0,) * a.ndim)
    perb = lambda a: pl.BlockSpec((1,) + a.shape[1:], lambda bi, ti: (bi,) + (0,) * (a.ndim - 1))
    out_shapes = [
        jax.ShapeDtypeStruct((m, mw), BF16),
        jax.ShapeDtypeStruct(c0.shape, F32),
        jax.ShapeDtypeStruct(n0.shape, F32),
        jax.ShapeDtypeStruct(m0.shape, F32),
        jax.ShapeDtypeStruct(cv0.shape, F32),
    ]
    return pl.pallas_call(
        functools.partial(_mlstm_kernel, L=L, lb=lb, n_heads=n_heads),
        grid=(b, nt),
        in_specs=[row(mw), row(mw), row(mw), row(LANES), row(LANES),
                  perb(c0), perb(n0), perb(m0), perb(cv0),
                  const(cw), const(cb), const(wq), const(wk), const(wkt), const(bg), const(hg), const(tri)],
        out_specs=[row(mw), perb(c0), perb(n0), perb(m0), perb(cv0)],
        out_shape=out_shapes,
        scratch_shapes=[
            pltpu.VMEM(c0.shape[1:], F32),
            pltpu.VMEM(n0.shape[1:], F32),
            pltpu.VMEM(m0.shape[1:], F32),
            pltpu.VMEM((SUBLANES, mw), F32),
            pltpu.VMEM((L + SUBLANES, mw), F32),
        ],
        compiler_params=pltpu.CompilerParams(
            dimension_semantics=("arbitrary", "arbitrary"), vmem_limit_bytes=VMEM_LIMIT),
        name="mlstm",
    )(u, vm, om, ig, fg, c0, n0, m0, cv0, cw, cb, wq, wk, wkt, bg, hg, tri)


def _ffn_kernel(x_ref, a_ref, mo_ref, wo_ref, g2_ref, wu_ref, wd_ref, y_ref, hn_scr):
    j = pl.program_id(1)

    @pl.when(j == 0)
    def _():
        aw = a_ref.shape[1]
        h = x_ref[...] + _dot(a_ref[...], wo_ref[:aw, :]) + _dot(mo_ref[...], wo_ref[aw:, :])
        y_ref[...] = h
        hn_scr[...] = _rms(h, g2_ref[...]).astype(BF16)

    f = jnp.maximum(_dot(hn_scr[...], wu_ref[...]), 0.0)
    y_ref[...] += _dot((f * f).astype(BF16), wd_ref[...])


def _ffn(x, a, mo, wo, g2, wu, wd, *, tm, tf):
    m, d = x.shape
    dff = wu.shape[1]
    row = lambda w: pl.BlockSpec((tm, w), lambda i, j: (i, 0))
    const = lambda arr: pl.BlockSpec(arr.shape, lambda i, j: (0, 0), pipeline_mode=pl.Buffered(1))
    return pl.pallas_call(
        _ffn_kernel,
        grid=(m // tm, dff // tf),
        in_specs=[row(d), row(a.shape[1]), row(mo.shape[1]), const(wo), const(g2),
                  pl.BlockSpec((d, tf), lambda i, j: (0, j)),
                  pl.BlockSpec((tf, d), lambda i, j: (j, 0))],
        out_specs=row(d),
        out_shape=jax.ShapeDtypeStruct((m, d), F32),
        scratch_shapes=[pltpu.VMEM((tm, d), BF16)],
        compiler_params=pltpu.CompilerParams(
            dimension_semantics=("arbitrary", "arbitrary"), vmem_limit_bytes=VMEM_LIMIT),
        name="ffn",
    )(x, a, mo, wo, g2, wu, wd)


def _rope_tables(pos, rot, width):
    half = rot // 2
    inv_freq = ROPE_THETA ** (-jnp.arange(half, dtype=F32) / half)
    ang = pos.astype(F32)[:, None] * inv_freq[None, :]
    cos, sin = jnp.cos(ang), jnp.sin(ang)
    n = pos.shape[0]
    rest1 = jnp.ones((n, width - rot), F32)
    rest0 = jnp.zeros((n, width - rot), F32)
    z = jnp.zeros_like(sin)
    c = jnp.concatenate([cos, cos, rest1], axis=1)
    sa = jnp.concatenate([-sin, z, rest0], axis=1)
    sb = jnp.concatenate([z, sin, rest0], axis=1)
    rep = LANES // width
    return jnp.concatenate([jnp.tile(c, (1, rep)), jnp.tile(sa, (1, rep)), jnp.tile(sb, (1, rep))], axis=1)


def _pad_cols(w, n):
    return jnp.pad(w, ((0, 0), (0, n - w.shape[1])))


def _tile_plan(n_rows, t, n_keys, prompt):
    if prompt:
        return dict(tm_proj=256, tm_ffn=512, tf=1024, tq=LANES, tk=512, L=128, lb=128)
    return dict(tm_proj=n_rows, tm_ffn=n_rows, tf=512, tq=LANES, tk=-(-n_keys // LANES) * LANES, L=128, lb=t)


def _layer(x, pos_rows, mode, state, params, splits):
    (norm1_g, w_in, q_norm_g, k_norm_g, conv_w, conv_b, wq_m, wk_m, b_igate, b_fgate,
     hnorm_g, w_out, norm2_g, w_up, w_down) = params
    bsz, t, d = x.shape
    m = bsz * t
    n_keys = t if mode == "prompt" else state[0].shape[1] + t
    plan = _tile_plan(m, t, n_keys, mode == "prompt")
    tm_proj, tm_ffn, tf, tq, tk, L, lb = (plan[k] for k in ("tm_proj", "tm_ffn", "tf", "tq", "tk", "L", "lb"))
    x2 = x.reshape(m, d)
    offs = np.cumsum((0,) + tuple(splits))
    col = lambda i: w_in[:, int(offs[i]):int(offs[i + 1])]
    aw, mw = splits[0], splits[6]
    n_q_heads = aw // A_HEAD_DIM
    n_m_heads = mw // M_HEAD_DIM
    wa = jnp.concatenate(
        [col(0), col(1), col(2), col(3), col(4), col(4),
         _pad_cols(col(5), LANES), _pad_cols(col(9), LANES), _pad_cols(col(10), LANES)], axis=1).astype(BF16)
    wb = jnp.concatenate([col(6), col(7), col(8)], axis=1).astype(BF16)
    ra = _rope_tables(pos_rows, A_HEAD_DIM // 4, A_HEAD_DIM)
    ri = _rope_tables(pos_rows, IDX_DIM // 4, IDX_DIM)

    (q, k, v, ki, kb, vb, kiab, qi, wi, ig, fg, u, vm, om) = _proj(
        x2, norm1_g[None], wa, wb, q_norm_g[None], k_norm_g[None], ra, ri,
        tm=tm_proj, n_q_heads=n_q_heads, m_width=mw)

    kvw = A_KV_HEADS * A_HEAD_DIM
    r3 = lambda a_: a_.reshape(bsz, t, a_.shape[1])
    if mode == "prompt":
        c0 = jnp.zeros((bsz, n_m_heads, M_HEAD_DIM, M_HEAD_DIM), F32)
        n0 = jnp.zeros((bsz, n_m_heads, M_HEAD_DIM), F32)
        m0 = jnp.zeros((bsz, 1, LANES), F32)
        cv0 = jnp.zeros((bsz, SUBLANES, mw), F32)
        n_sel = min(TOPK_MAX, t // 4)
        a = _attn(r3(q), r3(qi), r3(wi), r3(kb), r3(vb), r3(kiab),
                  tq=tq, tk=tk, causal=True, s_valid=t, n_sel=n_sel)
    else:
        ck, cv, ckidx, s_c, s_n, s_m, s_conv = state
        p = ck.shape[1]
        s_valid = p + t
        s_pad = -(-s_valid // tk) * tk
        padk = lambda a_: jnp.pad(a_, ((0, 0), (0, s_pad - s_valid), (0, 0)))
        k_all = padk(jnp.concatenate([ck.reshape(bsz, p, kvw).astype(BF16), r3(kb)], axis=1))
        v_all = padk(jnp.concatenate([cv.reshape(bsz, p, kvw).astype(BF16), r3(vb)], axis=1))
        cki = ckidx.astype(BF16)
        zki = jnp.zeros_like(cki)
        ki_c = jnp.concatenate([cki, zki, zki, cki], axis=2)
        ki_all = padk(jnp.concatenate([ki_c, r3(kiab)], axis=1))
        c0, n0 = s_c.astype(F32), s_n.astype(F32)
        m0 = jnp.pad(s_m.astype(F32), ((0, 0), (0, LANES - n_m_heads)))[:, None, :]
        cv0 = jnp.pad(s_conv.astype(F32), ((0, 0), (SUBLANES - (CONV_W - 1), 0), (0, 0)))
        n_sel = min(TOPK_MAX, s_valid // 4)
        a = _attn(r3(q), r3(qi), r3(wi), k_all, v_all, ki_all,
                  tq=tq, tk=tk, causal=False, s_valid=s_valid, n_sel=n_sel)

    bg = jnp.stack([_pad_cols(b_igate[None], LANES)[0], _pad_cols(b_fgate[None], LANES)[0]])
    tri = (np.arange(L)[:, None] >= np.arange(L)[None, :]).astype(np.float32)
    mo, c_new, n_new, m_new, cv_new = _mlstm(
        u, vm, om, ig, fg, c0, n0, m0, cv0,
        conv_w, conv_b[None], wq_m.astype(BF16), wk_m.astype(BF16),
        jnp.swapaxes(wk_m, 1, 2).astype(BF16), bg, hnorm_g[None], jnp.asarray(tri),
        b=bsz, L=L, lb=lb)

    y = _ffn(x2, a.reshape(m, aw), mo, w_out.astype(BF16), norm2_g[None],
             w_up.astype(BF16), w_down.astype(BF16), tm=tm_ffn, tf=tf)
    new_state = (
        k.reshape(bsz, t, A_KV_HEADS, A_HEAD_DIM), v.reshape(bsz, t, A_KV_HEADS, A_HEAD_DIM),
        ki.reshape(bsz, t, IDX_DIM), c_new, n_new, m_new[:, 0, :n_m_heads],
        cv_new[:, SUBLANES - (CONV_W - 1):, :])
    return y.reshape(bsz, t, d), new_state


def kernel(x_prompt, x_sample, cache_k, cache_v, cache_kidx, state_mlstm_C, state_mlstm_n, state_mlstm_m,
           state_conv, norm1_g, w_in, q_norm_g, k_norm_g, conv_w, conv_b, wq_m, wk_m, b_igate, b_fgate,
           hnorm_g, w_out, norm2_g, w_up, w_down):
    depth = w_in.shape[0]
    d = x_prompt.shape[-1]
    mix = w_out.shape[1]
    aw = mix // 2
    mw = mix - aw
    kvw = A_KV_HEADS * A_HEAD_DIM
    n_m_heads = mw // M_HEAD_DIM
    splits = (aw, kvw, kvw, IDX_HEADS * IDX_DIM, IDX_DIM, IDX_HEADS, mw, mw, mw, n_m_heads, n_m_heads)
    t_p = x_prompt.shape[1]
    b_s, t_s = x_sample.shape[0], x_sample.shape[1]
    past = cache_k.shape[2]

    y_p, y_s = x_prompt, x_sample
    new_p, new_s = [], []
    for l in range(depth):
        params = (norm1_g[l], w_in[l], q_norm_g[l], k_norm_g[l], conv_w[l], conv_b[l], wq_m[l], wk_m[l],
                  b_igate[l], b_fgate[l], hnorm_g[l], w_out[l], norm2_g[l], w_up[l], w_down[l])
        y_p, st_p = _layer(
            y_p, jnp.arange(t_p), "prompt", None, params, splits)
        y_s, st_s = _layer(
            y_s, past + (jnp.arange(b_s * t_s) % t_s), "sample",
            (cache_k[l], cache_v[l], cache_kidx[l], state_mlstm_C[l], state_mlstm_n[l],
             state_mlstm_m[l], state_conv[l]), params, splits)
        new_p.append(st_p)
        new_s.append(st_s)

    stack = lambda states, i: jnp.stack([s[i] for s in states])
    return (y_p, y_s,
            *[stack(new_p, i) for i in range(7)],
            *[stack(new_s, i) for i in range(7)])
```

```python
import functools

import jax
import jax.numpy as jnp
import numpy as np
from jax import lax
from jax.experimental import pallas as pl
from jax.experimental.pallas import tpu as pltpu

F32 = jnp.float32
BF16 = jnp.bfloat16
I32 = jnp.int32

EPS = 1e-6
ROPE_THETA = 500000.0
CHUNK = 64
CHUNK_LOG2 = 6
TOPK_MAX = 256
A_HEAD_DIM = 128
A_KV_HEADS = 2
IDX_HEADS = 8
IDX_DIM = 64
M_HEAD_DIM = 128
CONV_W = 4

LANES = 128
SUBLANES = 8
VMEM_LIMIT = 56 * 1024 * 1024
NEG_BIG = -1e30
LOG2_E = 1.4426950408889634
INT_MIN = -(2 ** 31)
BIG_J = 2 ** 30
UNSETTLED = 1e9
SEARCH_EARLY_BITS = 26


def _rms(x, g):
    r = lax.rsqrt(jnp.mean(x * x, axis=-1, keepdims=True) + EPS)
    return x * r * g


def _rope(t, c, sa, sb, half):
    return t * c + pltpu.roll(t, LANES - half, 1) * sa + pltpu.roll(t, half, 1) * sb


def _sigmoid(x):
    return 1.0 / (1.0 + jnp.exp(-x))


def _dot(a, b):
    return jnp.dot(a, b, preferred_element_type=F32)


def _dot_t(a, b):
    return lax.dot_general(a, b, (((1,), (1,)), ((), ())), preferred_element_type=F32)


def _proj_kernel(x_ref, g1_ref, wa_ref, wb_ref, qg_ref, kg_ref, ra_ref, ri_ref,
                 q_out, k_out, v_out, ki_out, kb_out, vb_out, kiab_out, qi_out,
                 wi_out, ig_out, fg_out, u_out, vm_out, om_out, *, n_q_heads):
    xn = _rms(x_ref[...], g1_ref[...]).astype(BF16)
    ra = ra_ref[...]
    ca, saa, sba = ra[:, :LANES], ra[:, LANES:2 * LANES], ra[:, 2 * LANES:]
    ri = ri_ref[...]
    ci, sai, sbi = ri[:, :LANES], ri[:, LANES:2 * LANES], ri[:, 2 * LANES:]
    qg = qg_ref[...]
    kg = kg_ref[...]
    aw = n_q_heads * A_HEAD_DIM
    kvw = A_KV_HEADS * A_HEAD_DIM
    q_scale = A_HEAD_DIM ** -0.5 * LOG2_E
    i_scale = IDX_DIM ** -0.5

    for c in range(0, aw, 512):
        z = _dot(xn, wa_ref[:, c:c + 512])
        for j in range(0, 512, LANES):
            qh = _rope(_rms(z[:, j:j + LANES], qg), ca, saa, sba, A_HEAD_DIM // 8) * q_scale
            q_out[:, c + j:c + j + LANES] = qh.astype(BF16)
    off = aw
    z = _dot(xn, wa_ref[:, off:off + 2 * kvw])
    for j in range(0, kvw, LANES):
        kh = _rope(_rms(z[:, j:j + LANES], kg), ca, saa, sba, A_HEAD_DIM // 8)
        k_out[:, j:j + LANES] = kh
        kb_out[:, j:j + LANES] = kh.astype(BF16)
    vv = z[:, kvw:]
    v_out[...] = vv
    vb_out[...] = vv.astype(BF16)
    off += 2 * kvw
    iw = IDX_HEADS * IDX_DIM
    z = _dot(xn, wa_ref[:, off:off + iw])
    for j in range(0, iw, LANES):
        qi = _rope(z[:, j:j + LANES], ci, sai, sbi, IDX_DIM // 8) * i_scale
        qi_out[:, j:j + LANES] = qi.astype(BF16)
    off += iw
    z = _dot(xn, wa_ref[:, off:off + 4 * LANES])
    ki2 = _rope(z[:, :LANES], ci, sai, sbi, IDX_DIM // 8)
    ki_out[...] = ki2[:, :IDX_DIM]
    lane = lax.broadcasted_iota(I32, (1, LANES), 1)
    lo = lane < IDX_DIM
    kiab_out[:, :LANES] = jnp.where(lo, ki2, 0.0).astype(BF16)
    kiab_out[:, LANES:] = jnp.where(lo, 0.0, ki2).astype(BF16)
    wi_out[...] = z[:, LANES:2 * LANES]
    ig_out[...] = z[:, 2 * LANES:3 * LANES]
    fg_out[...] = z[:, 3 * LANES:]
    mw = u_out.shape[1]
    for c in range(0, mw, 512):
        u_out[:, c:c + 512] = _dot(xn, wb_ref[:, c:c + 512])
        vm_out[:, c:c + 512] = _dot(xn, wb_ref[:, mw + c:mw + c + 512]).astype(BF16)
        om_out[:, c:c + 512] = _dot(xn, wb_ref[:, 2 * mw + c:2 * mw + c + 512])


def _proj(x, g1, wa, wb, qg, kg, ra, ri, *, tm, n_q_heads, m_width):
    m, d = x.shape
    n_tab = ra.shape[0] // tm
    aw = n_q_heads * A_HEAD_DIM
    kvw = A_KV_HEADS * A_HEAD_DIM
    iw = IDX_HEADS * IDX_DIM
    row = lambda w: pl.BlockSpec((tm, w), lambda i: (i, 0))
    const = lambda a: pl.BlockSpec(a.shape, lambda i: (0, 0), pipeline_mode=pl.Buffered(1))
    tab = pl.BlockSpec((tm, 3 * LANES), lambda i: (i % n_tab, 0))
    outs = [
        (aw, BF16), (kvw, F32), (kvw, F32), (IDX_DIM, F32), (kvw, BF16), (kvw, BF16),
        (2 * LANES, BF16), (iw, BF16), (LANES, F32), (LANES, F32), (LANES, F32),
        (m_width, F32), (m_width, BF16), (m_width, F32),
    ]
    return pl.pallas_call(
        functools.partial(_proj_kernel, n_q_heads=n_q_heads),
        grid=(m // tm,),
        in_specs=[row(d), const(g1), const(wa), const(wb), const(qg), const(kg), tab, tab],
        out_specs=[row(w) for w, _ in outs],
        out_shape=[jax.ShapeDtypeStruct((m, w), dt) for w, dt in outs],
        compiler_params=pltpu.CompilerParams(
            dimension_semantics=("arbitrary",), vmem_limit_bytes=VMEM_LIMIT),
        name="proj",
    )(x, g1, wa, wb, qg, kg, ra, ri)


def _pad_rows(x, n):
    if x.shape[0] == n:
        return x
    return jnp.concatenate([x, jnp.zeros((n - x.shape[0],) + x.shape[1:], x.dtype)], axis=0)


def _transpose_bf16(x):
    return jnp.transpose(x.astype(F32)).astype(BF16)


V_EXT_ROWS = A_HEAD_DIM + 16


def _attn_kernel(q_ref, qi_ref, wi_ref, k_ref, vt_ref, ki_ref, o_ref,
                 key_scr, tie_scr, qs_scr, qis_scr, m_scr, acc_scr, sa_scr, sb_scr, cm_scr, da_scr, db_scr,
                 *, tq, tqb, tk, causal, s_valid, n_sel, n_tiles, group):
    start = pl.program_id(1) * tqb
    nr = tk // SUBLANES
    n_acc = 8
    lane = lax.broadcasted_iota(I32, (1, tq), 1)
    rowi = lax.broadcasted_iota(I32, (tk, 1), 0)
    if causal:
        limit = (lax.shift_right_logical(start + lane, CHUNK_LOG2) + 1) * CHUNK
        nkt = jnp.minimum((start + tqb + tk - 1) // tk, n_tiles)
    else:
        limit = jnp.full((1, tq), s_valid, I32)
        nkt = n_tiles

    q = _pad_rows(q_ref[0], tq)
    for g in range(A_KV_HEADS):
        for r in range(group):
            h = g * group + r
            qs_scr[g, :, r * tq:(r + 1) * tq] = _transpose_bf16(q[:, h * LANES:(h + 1) * LANES])
    qi = _pad_rows(qi_ref[0], tq)
    for j in range(IDX_HEADS // 2):
        qis_scr[:, j * tq:(j + 1) * tq] = _transpose_bf16(qi[:, j * LANES:(j + 1) * LANES])
    w_t = jnp.transpose(_pad_rows(wi_ref[0], tq))[:IDX_HEADS] * (IDX_HEADS ** -0.5)

    d_bufs = (da_scr, db_scr)

    def stage_scores(tile, par):
        t_c = jnp.minimum(tile, nkt - 1)
        off = pl.multiple_of(t_c * tk, tk)
        kiab = ki_ref[0, pl.ds(off, tk), :]
        for half in range(IDX_HEADS // 4):
            qh = qis_scr[:, half * 2 * tq:(half + 1) * 2 * tq]
            for odd in range(2):
                d_bufs[par][2 * half + odd] = _dot(kiab[:, odd * LANES:(odd + 1) * LANES], qh)

    def finish_scores(t_c, par):
        acc = jnp.zeros((tk, tq), F32)
        for half in range(IDX_HEADS // 4):
            for odd in range(2):
                d = d_bufs[par][2 * half + odd]
                for jj in range(2):
                    h = 2 * (2 * half + jj) + odd
                    acc = acc + w_t[h:h + 1, :] * jnp.maximum(d[:, jj * tq:(jj + 1) * tq], 0.0)
        bits = lax.bitcast_convert_type(acc, I32)
        key = bits ^ (lax.shift_right_arithmetic(bits, 31) & 0x7FFFFFFF)
        key = jnp.where(key == -1, 0, key)
        key_scr[t_c] = jnp.where(t_c * tk + rowi < limit, key, INT_MIN)

    stage_scores(0, 0)

    def score_pair(i, carry):
        stage_scores(2 * i + 1, 1)
        finish_scores(2 * i, 0)
        stage_scores(2 * i + 2, 0)
        finish_scores(2 * i + 1, 1)
        return carry

    lax.fori_loop(0, nkt // 2, score_pair, 0)

    @pl.when(nkt % 2 == 1)
    def _():
        finish_scores(nkt - 1, 0)

    def count_ref(ref, pred):
        def body(kt, acc):
            hit = pred(ref[kt].reshape(nr, SUBLANES, tq))
            ones = jnp.where(hit, 1.0, 0.0).reshape(nr // n_acc, n_acc, SUBLANES, tq)
            return acc + jnp.sum(ones, axis=0)
        acc = lax.fori_loop(0, nkt, body, jnp.zeros((n_acc, SUBLANES, tq), F32))
        return jnp.sum(jnp.sum(acc, axis=0), axis=0, keepdims=True)

    def bit_step(i, carry):
        t_cur, c_cur = carry
        cand = t_cur + lax.shift_left(jnp.int32(1), 31 - i)
        cnt = count_ref(key_scr, lambda key: key >= cand[None])
        ok = cnt >= n_sel
        return jnp.where(ok, cand, t_cur), jnp.where(ok, cnt, c_cur)

    def any_lane(mask):
        return jnp.max(jnp.where(mask, 1.0, 0.0)) > 0.0

    valid = lane < tqb
    n_self = float(n_sel)

    def settle(t_cur, c_cur):
        pend = jnp.logical_and(valid, c_cur != n_self)
        return lax.cond(any_lane(pend), lambda: count_ref(key_scr, lambda key: key > t_cur[None]),
                        lambda: jnp.zeros((1, tq), F32))

    t_e, c_e = lax.fori_loop(
        0, SEARCH_EARLY_BITS, bit_step,
        (jnp.full((1, tq), INT_MIN, I32), jnp.where(limit < n_sel, n_self, UNSETTLED)))
    g_e = settle(t_e, c_e)
    unresolved = jnp.logical_and(jnp.logical_and(valid, c_e != n_self), g_e >= n_self)

    def finish_search():
        t_l, c_l = lax.fori_loop(SEARCH_EARLY_BITS, 32, bit_step, (t_e, c_e))
        return t_l, c_l, settle(t_l, c_l)

    t_fin, c_fin, c_gt = lax.cond(any_lane(unresolved), finish_search, lambda: (t_e, c_e, g_e))
    is_min = t_fin == INT_MIN
    needs = jnp.logical_and(jnp.logical_and(jnp.logical_not(is_min), c_fin > n_sel), valid)
    j_fast = jnp.where(is_min, 0, BIG_J)

    def tie_search():
        rem = n_self - c_gt

        def mark(kt, carry):
            tie_scr[kt] = jnp.where(key_scr[kt] == t_fin, kt * tk + rowi, BIG_J)
            return carry

        lax.fori_loop(0, nkt, mark, 0)
        j_bits = (n_tiles * tk).bit_length()

        def j_pending(carry):
            i, _, f_cur = carry
            return jnp.logical_and(i < j_bits, any_lane(jnp.logical_and(needs, f_cur != rem)))

        def j_step(carry):
            i, j_cur, f_cur = carry
            cand = j_cur + lax.shift_left(jnp.int32(1), j_bits - 1 - i)
            f = count_ref(tie_scr, lambda col: col < cand[None])
            ok = f <= rem
            return i + 1, jnp.where(ok, cand, j_cur), jnp.where(ok, f, f_cur)

        _, j_slow, _ = lax.while_loop(
            j_pending, j_step, (jnp.int32(0), jnp.zeros((1, tq), I32), jnp.zeros((1, tq), F32)))
        return jnp.where(needs, j_slow, j_fast)

    j_fin = lax.cond(any_lane(needs), tie_search, lambda: j_fast)

    m_scr[...] = jnp.full(m_scr.shape, NEG_BIG, F32)
    acc_scr[...] = jnp.zeros(acc_scr.shape, F32)

    s_bufs = (sa_scr, sb_scr)

    def stage_logits(tile, par):
        t_c = jnp.minimum(tile, nkt - 1)
        off = pl.multiple_of(t_c * tk, tk)
        key = key_scr[t_c]
        sel = jnp.logical_or(key > t_fin, jnp.logical_and(key == t_fin, off + rowi < j_fin))
        bias = jnp.where(sel, 0.0, -jnp.inf)
        bias = jnp.concatenate([bias] * group, axis=1)
        for g in range(A_KV_HEADS):
            s = _dot(k_ref[0, pl.ds(off, tk), g * LANES:(g + 1) * LANES], qs_scr[g]) + bias
            s_bufs[par][g] = s
            cm_scr[par, g] = jnp.max(s, axis=0, keepdims=True)

    def attend_tile(t_c, par):
        for g in range(A_KV_HEADS):
            m_old = m_scr[g]
            m_new = jnp.maximum(m_old, cm_scr[par, g])
            alpha = jnp.exp2(m_old - m_new)
            p = jnp.exp2(s_bufs[par][g] - m_new).astype(BF16)
            acc_scr[g] = alpha * acc_scr[g] + _dot(vt_ref[0, t_c, g], p)
            m_scr[g] = m_new

    stage_logits(0, 0)

    def attend_pair(i, carry):
        stage_logits(2 * i + 1, 1)
        attend_tile(2 * i, 0)
        stage_logits(2 * i + 2, 0)
        attend_tile(2 * i + 1, 1)
        return carry

    lax.fori_loop(0, (nkt - 1) // 2, attend_pair, 0)

    @pl.when(nkt % 2 == 1)
    def _():
        attend_tile(nkt - 1, 0)

    @pl.when(nkt % 2 == 0)
    def _():
        stage_logits(nkt - 1, 1)
        attend_tile(nkt - 2, 0)
        attend_tile(nkt - 1, 1)
    for g in range(A_KV_HEADS):
        acc = acc_scr[g]
        out_t = acc[:A_HEAD_DIM] / acc[A_HEAD_DIM:A_HEAD_DIM + 1]
        for r in range(group):
            h = g * group + r
            o = jnp.transpose(out_t[:, r * tq:(r + 1) * tq])
            o_ref[0, :, h * LANES:(h + 1) * LANES] = o[:tqb].astype(BF16)


def _attn(q, qi, wi, kb, vb, kiab, *, tq, tk, causal, s_valid, n_sel):
    b, t, aw = q.shape
    tqb = min(tq, t)
    s_pad = kb.shape[1]
    n_tiles = s_pad // tk
    n_heads = aw // A_HEAD_DIM
    group = n_heads // A_KV_HEADS
    vt = jnp.transpose(vb.reshape(b, s_pad, A_KV_HEADS, A_HEAD_DIM), (0, 2, 3, 1))
    vt = jnp.concatenate([vt, jnp.ones((b, A_KV_HEADS, V_EXT_ROWS - A_HEAD_DIM, s_pad), BF16)], axis=2)
    vt = jnp.transpose(vt.reshape(b, A_KV_HEADS, V_EXT_ROWS, n_tiles, tk), (0, 3, 1, 2, 4))
    qspec = lambda w: pl.BlockSpec((1, tqb, w), lambda bi, qi_: (bi, qi_, 0))
    kspec = lambda w: pl.BlockSpec((1, s_pad, w), lambda bi, qi_: (bi, 0, 0), pipeline_mode=pl.Buffered(1))
    kern = functools.partial(_attn_kernel, tq=tq, tqb=tqb, tk=tk, causal=causal, s_valid=s_valid,
                             n_sel=n_sel, n_tiles=n_tiles, group=group)
    return pl.pallas_call(
        kern,
        grid=(b, t // tqb),
        in_specs=[qspec(aw), qspec(qi.shape[2]), qspec(LANES),
                  kspec(kb.shape[2]),
                  pl.BlockSpec((1,) + vt.shape[1:], lambda bi, qi_: (bi, 0, 0, 0, 0),
                               pipeline_mode=pl.Buffered(1)),
                  kspec(kiab.shape[2])],
        out_specs=qspec(aw),
        out_shape=jax.ShapeDtypeStruct((b, t, aw), BF16),
        scratch_shapes=[
            pltpu.VMEM((n_tiles, tk, tq), I32),
            pltpu.VMEM((n_tiles, tk, tq), I32),
            pltpu.VMEM((A_KV_HEADS, LANES, group * tq), BF16),
            pltpu.VMEM((LANES, IDX_HEADS // 2 * tq), BF16),
            pltpu.VMEM((A_KV_HEADS, 1, group * tq), F32),
            pltpu.VMEM((A_KV_HEADS, V_EXT_ROWS, group * tq), F32),
            pltpu.VMEM((A_KV_HEADS, tk, group * tq), F32),
            pltpu.VMEM((A_KV_HEADS, tk, group * tq), F32),
            pltpu.VMEM((2, A_KV_HEADS, 1, group * tq), F32),
            pltpu.VMEM((IDX_HEADS // 2, tk, 2 * tq), F32),
            pltpu.VMEM((IDX_HEADS // 2, tk, 2 * tq), F32),
        ],
        compiler_params=pltpu.CompilerParams(
            dimension_semantics=("arbitrary", "arbitrary"), vmem_limit_bytes=VMEM_LIMIT),
        name="attn",
    )(q, qi, wi, kb, vt, kiab)


def _mlstm_kernel(u_ref, vm_ref, om_ref, ig_ref, fg_ref, c0_ref, n0_ref, m0_ref, cv0_ref,
                  cw_ref, cb_ref, wq_ref, wk_ref, wkt_ref, bg_ref, hg_ref, tri_ref,
                  mo_ref, c_out, n_out, m_out, cv_out,
                  c_scr, n_scr, m_scr, prev_scr, ubuf,
                  *, L, lb, n_heads):
    t = pl.program_id(1)
    nt = pl.num_programs(1)
    hd = M_HEAD_DIM
    pad = SUBLANES

    @pl.when(t == 0)
    def _():
        c_scr[...] = c0_ref[0]
        n_scr[...] = n0_ref[0]
        m_scr[...] = m0_ref[0]
        prev_scr[...] = cv0_ref[0]

    ubuf[0:pad, :] = prev_scr[...]
    ubuf[pad:pad + L, :] = _pad_rows(u_ref[...], L)
    uc = cb_ref[...]
    for i in range(CONV_W):
        uc = uc + ubuf[pad - (CONV_W - 1) + i:pad - (CONV_W - 1) + i + L, :] * cw_ref[i:i + 1, :]
    prev_scr[...] = ubuf[lb:lb + pad, :]
    uh = (uc * _sigmoid(uc)).astype(BF16)

    vm = _pad_rows(vm_ref[...], L)
    om = _pad_rows(om_ref[...], L)
    rowv = lax.broadcasted_iota(I32, (L, LANES), 0) < lb
    li = jnp.where(rowv, _pad_rows(ig_ref[...], L) + bg_ref[0:1, :], NEG_BIG)
    xf = _pad_rows(fg_ref[...], L) + bg_ref[1:2, :]
    lf = jnp.where(rowv, jnp.minimum(xf, 0.0) - jnp.log(1.0 + jnp.exp(-jnp.abs(xf))), 0.0)
    b = jnp.dot(tri_ref[...], lf, precision=lax.Precision.HIGHEST, preferred_element_type=F32)
    c = li - b
    c_t = c.T
    m_prev = m_scr[...]
    inter = b + m_prev
    b_last = b[L - 1:L, :]
    dec = b_last - b + li
    m_new = jnp.maximum(b_last + m_prev, jnp.max(dec, axis=0, keepdims=True))
    wts = jnp.exp(dec - m_new)
    wts_t = wts.T
    sc = jnp.exp(b_last + m_prev - m_new)
    causal = lax.broadcasted_iota(I32, (L, L), 0) >= lax.broadcasted_iota(I32, (L, L), 1)
    k_scale = hd ** -0.5

    heads = range(n_heads)
    sls = [slice(h * hd, (h + 1) * hd) for h in heads]
    q_l = [_dot(uh[:, sls[h]], wq_ref[h]) for h in heads]
    k_l = [_dot(uh[:, sls[h]], wk_ref[h]) * k_scale for h in heads]
    kt_l = [_dot_t(wkt_ref[h], uh[:, sls[h]]) * k_scale for h in heads]
    qb_l = [q.astype(BF16) for q in q_l]
    s_l = [_dot_t(qb_l[h], k_l[h].astype(BF16)) for h in heads]
    qc_l = [_dot(qb_l[h], c_scr[h].astype(BF16)) for h in heads]
    d_l = [jnp.where(causal, b[:, h:h + 1] + c_t[h:h + 1, :], -jnp.inf) for h in heads]
    mt_l = [jnp.maximum(inter[:, h:h + 1], jnp.max(d_l[h], axis=-1, keepdims=True)) for h in heads]
    sw_l = [jnp.exp(d_l[h] - mt_l[h]) * s_l[h] for h in heads]
    sp_l = [jnp.exp(inter[:, h:h + 1] - mt_l[h]) for h in heads]
    num_l = [sp_l[h] * qc_l[h] + _dot(sw_l[h].astype(BF16), vm[:, sls[h]]) for h in heads]
    den_l = [sp_l[h] * jnp.sum(q_l[h] * n_scr[h:h + 1, :], axis=-1, keepdims=True)
             + jnp.sum(sw_l[h], axis=-1, keepdims=True) for h in heads]
    kv_l = [_dot((kt_l[h] * wts_t[h:h + 1, :]).astype(BF16), vm[:, sls[h]]) for h in heads]
    hh_l = [num_l[h] / jnp.maximum(jnp.abs(den_l[h]), jnp.exp(-mt_l[h])) for h in heads]
    r_l = [lax.rsqrt(jnp.mean(hh * hh, axis=-1, keepdims=True) + EPS) for hh in hh_l]
    og_l = [_sigmoid(om[:, sls[h]]) for h in heads]
    for h in heads:
        mo_ref[:, sls[h]] = (og_l[h] * (hh_l[h] * r_l[h] * hg_ref[:, sls[h]]))[:lb].astype(BF16)
    for h in heads:
        sc_h = sc[:, h:h + 1]
        c_scr[h] = sc_h * c_scr[h] + kv_l[h]
        n_scr[h:h + 1, :] = (sc_h * n_scr[h:h + 1, :]
                             + jnp.sum(k_l[h] * wts[:, h:h + 1], axis=0, keepdims=True))
    m_scr[...] = m_new

    @pl.when(t == nt - 1)
    def _():
        c_out[0] = c_scr[...]
        n_out[0] = n_scr[...]
        m_out[0] = m_scr[...]
        cv_out[0] = prev_scr[...]


def _mlstm(u, vm, om, ig, fg, c0, n0, m0, cv0, cw, cb, wq, wk, wkt, bg, hg, tri, *, b, L, lb):
    m, mw = u.shape
    nt = m // (b * lb)
    n_heads = mw // M_HEAD_DIM
    row = lambda w: pl.BlockSpec((lb, w), lambda bi, ti: (bi * nt + ti, 0))
    const = lambda a: pl.BlockSpec(a.shape, lambda bi, ti: (0,) * a.ndim)
    perb = lambda a: pl.BlockSpec((1,) + a.shape[1:], lambda bi, ti: (bi,) + (0,) * (a.ndim - 1))
    out_shapes = [
        jax.ShapeDtypeStruct((m, mw), BF16),
        jax.ShapeDtypeStruct(c0.shape, F32),
        jax.ShapeDtypeStruct(n0.shape, F32),
        jax.ShapeDtypeStruct(m0.shape, F32),
        jax.ShapeDtypeStruct(cv0.shape, F32),
    ]
    return pl.pallas_call(
        functools.partial(_mlstm_kernel, L=L, lb=lb, n_heads=n_heads),
        grid=(b, nt),
        in_specs=[row(mw), row(mw), row(mw), row(LANES), row(LANES),
                  perb(c0), perb(n0), perb(m0), perb(cv0),
                  const(cw), const(cb), const(wq), const(wk), const(wkt), const(bg), const(hg), const(tri)],
        out_specs=[row(mw), perb(c0), perb(n0), perb(m0), perb(cv0)],
        out_shape=out_shapes,
        scratch_shapes=[
            pltpu.VMEM(c0.shape[1:], F32),
            pltpu.VMEM(n0.shape[1:], F32),
            pltpu.VMEM(m0.shape[1:], F32),
            pltpu.VMEM((SUBLANES, mw), F32),
            pltpu.VMEM((L + SUBLANES, mw), F32),
        ],
        compiler_params=pltpu.CompilerParams(
            dimension_semantics=("arbitrary", "arbitrary"), vmem_limit_bytes=VMEM_LIMIT),
        name="mlstm",
    )(u, vm, om, ig, fg, c0, n0, m0, cv0, cw, cb, wq, wk, wkt, bg, hg, tri)


def _ffn_kernel(x_ref, a_ref, mo_ref, wo_ref, g2_ref, wu_ref, wd_ref, y_ref, hn_scr):
    j = pl.program_id(1)

    @pl.when(j == 0)
    def _():
        aw = a_ref.shape[1]
        h = x_ref[...] + _dot(a_ref[...], wo_ref[:aw, :]) + _dot(mo_ref[...], wo_ref[aw:, :])
        y_ref[...] = h
        hn_scr[...] = _rms(h, g2_ref[...]).astype(BF16)

    f = jnp.maximum(_dot(hn_scr[...], wu_ref[...]), 0.0)
    y_ref[...] += _dot((f * f).astype(BF16), wd_ref[...])


def _ffn(x, a, mo, wo, g2, wu, wd, *, tm, tf):
    m, d = x.shape
    dff = wu.shape[1]
    row = lambda w: pl.BlockSpec((tm, w), lambda i, j: (i, 0))
    const = lambda arr: pl.BlockSpec(arr.shape, lambda i, j: (0, 0), pipeline_mode=pl.Buffered(1))
    return pl.pallas_call(
        _ffn_kernel,
        grid=(m // tm, dff // tf),
        in_specs=[row(d), row(a.shape[1]), row(mo.shape[1]), const(wo), const(g2),
                  pl.BlockSpec((d, tf), lambda i, j: (0, j)),
                  pl.BlockSpec((tf, d), lambda i, j: (j, 0))],
        out_specs=row(d),
        out_shape=jax.ShapeDtypeStruct((m, d), F32),
        scratch_shapes=[pltpu.VMEM((tm, d), BF16)],
        compiler_params=pltpu.CompilerParams(
            dimension_semantics=("arbitrary", "arbitrary"), vmem_limit_bytes=VMEM_LIMIT),
        name="ffn",
    )(x, a, mo, wo, g2, wu, wd)


def _rope_tables(pos, rot, width):
    half = rot // 2
    inv_freq = ROPE_THETA ** (-jnp.arange(half, dtype=F32) / half)
    ang = pos.astype(F32)[:, None] * inv_freq[None, :]
    cos, sin = jnp.cos(ang), jnp.sin(ang)
    n = pos.shape[0]
    rest1 = jnp.ones((n, width - rot), F32)
    rest0 = jnp.zeros((n, width - rot), F32)
    z = jnp.zeros_like(sin)
    c = jnp.concatenate([cos, cos, rest1], axis=1)
    sa = jnp.concatenate([-sin, z, rest0], axis=1)
    sb = jnp.concatenate([z, sin, rest0], axis=1)
    rep = LANES // width
    return jnp.concatenate([jnp.tile(c, (1, rep)), jnp.tile(sa, (1, rep)), jnp.tile(sb, (1, rep))], axis=1)


def _pad_cols(w, n):
    return jnp.pad(w, ((0, 0), (0, n - w.shape[1])))


def _tile_plan(n_rows, t, n_keys, prompt):
    if prompt:
        return dict(tm_proj=256, tm_ffn=512, tf=1024, tq=LANES, tk=512, L=128, lb=128)
    return dict(tm_proj=n_rows, tm_ffn=n_rows, tf=512, tq=LANES, tk=-(-n_keys // LANES) * LANES, L=128, lb=t)


def _layer(x, pos_rows, mode, state, params, splits):
    (norm1_g, w_in, q_norm_g, k_norm_g, conv_w, conv_b, wq_m, wk_m, b_igate, b_fgate,
     hnorm_g, w_out, norm2_g, w_up, w_down) = params
    bsz, t, d = x.shape
    m = bsz * t
    n_keys = t if mode == "prompt" else state[0].shape[1] + t
    plan = _tile_plan(m, t, n_keys, mode == "prompt")
    tm_proj, tm_ffn, tf, tq, tk, L, lb = (plan[k] for k in ("tm_proj", "tm_ffn", "tf", "tq", "tk", "L", "lb"))
    x2 = x.reshape(m, d)
    offs = np.cumsum((0,) + tuple(splits))
    col = lambda i: w_in[:, int(offs[i]):int(offs[i + 1])]
    aw, mw = splits[0], splits[6]
    n_q_heads = aw // A_HEAD_DIM
    n_m_heads = mw // M_HEAD_DIM
    wa = jnp.concatenate(
        [col(0), col(1), col(2), col(3), col(4), col(4),
         _pad_cols(col(5), LANES), _pad_cols(col(9), LANES), _pad_cols(col(10), LANES)], axis=1).astype(BF16)
    wb = jnp.concatenate([col(6), col(7), col(8)], axis=1).astype(BF16)
    ra = _rope_tables(pos_rows, A_HEAD_DIM // 4, A_HEAD_DIM)
    ri = _rope_tables(pos_rows, IDX_DIM // 4, IDX_DIM)

    (q, k, v, ki, kb, vb, kiab, qi, wi, ig, fg, u, vm, om) = _proj(
        x2, norm1_g[None], wa, wb, q_norm_g[None], k_norm_g[None], ra, ri,
        tm=tm_proj, n_q_heads=n_q_heads, m_width=mw)

    kvw = A_KV_HEADS * A_HEAD_DIM
    r3 = lambda a_: a_.reshape(bsz, t, a_.shape[1])
    if mode == "prompt":
        c0 = jnp.zeros((bsz, n_m_heads, M_HEAD_DIM, M_HEAD_DIM), F32)
        n0 = jnp.zeros((bsz, n_m_heads, M_HEAD_DIM), F32)
        m0 = jnp.zeros((bsz, 1, LANES), F32)
        cv0 = jnp.zeros((bsz, SUBLANES, mw), F32)
        n_sel = min(TOPK_MAX, t // 4)
        a = _attn(r3(q), r3(qi), r3(wi), r3(kb), r3(vb), r3(kiab),
                  tq=tq, tk=tk, causal=True, s_valid=t, n_sel=n_sel)
    else:
        ck, cv, ckidx, s_c, s_n, s_m, s_conv = state
        p = ck.shape[1]
        s_valid = p + t
        s_pad = -(-s_valid // tk) * tk
        padk = lambda a_: jnp.pad(a_, ((0, 0), (0, s_pad - s_valid), (0, 0)))
        k_all = padk(jnp.concatenate([ck.reshape(bsz, p, kvw).astype(BF16), r3(kb)], axis=1))
        v_all = padk(jnp.concatenate([cv.reshape(bsz, p, kvw).astype(BF16), r3(vb)], axis=1))
        cki = ckidx.astype(BF16)
        zki = jnp.zeros_like(cki)
        ki_c = jnp.concatenate([cki, zki, zki, cki], axis=2)
        ki_all = padk(jnp.concatenate([ki_c, r3(kiab)], axis=1))
        c0, n0 = s_c.astype(F32), s_n.astype(F32)
        m0 = jnp.pad(s_m.astype(F32), ((0, 0), (0, LANES - n_m_heads)))[:, None, :]
        cv0 = jnp.pad(s_conv.astype(F32), ((0, 0), (SUBLANES - (CONV_W - 1), 0), (0, 0)))
        n_sel = min(TOPK_MAX, s_valid // 4)
        a = _attn(r3(q), r3(qi), r3(wi), k_all, v_all, ki_all,
                  tq=tq, tk=tk, causal=False, s_valid=s_valid, n_sel=n_sel)

    bg = jnp.stack([_pad_cols(b_igate[None], LANES)[0], _pad_cols(b_fgate[None], LANES)[0]])
    tri = (np.arange(L)[:, None] >= np.arange(L)[None, :]).astype(np.float32)
    mo, c_new, n_new, m_new, cv_new = _mlstm(
        u, vm, om, ig, fg, c0, n0, m0, cv0,
        conv_w, conv_b[None], wq_m.astype(BF16), wk_m.astype(BF16),
        jnp.swapaxes(wk_m, 1, 2).astype(BF16), bg, hnorm_g[None], jnp.asarray(tri),
        b=bsz, L=L, lb=lb)

    y = _ffn(x2, a.reshape(m, aw), mo, w_out.astype(BF16), norm2_g[None],
             w_up.astype(BF16), w_down.astype(BF16), tm=tm_ffn, tf=tf)
    new_state = (
        k.reshape(bsz, t, A_KV_HEADS, A_HEAD_DIM), v.reshape(bsz, t, A_KV_HEADS, A_HEAD_DIM),
        ki.reshape(bsz, t, IDX_DIM), c_new, n_new, m_new[:, 0, :n_m_heads],
        cv_new[:, SUBLANES - (CONV_W - 1):, :])
    return y.reshape(bsz, t, d), new_state


def kernel(x_prompt, x_sample, cache_k, cache_v, cache_kidx, state_mlstm_C, state_mlstm_n, state_mlstm_m,
           state_conv, norm1_g, w_in, q_norm_g, k_norm_g, conv_w, conv_b, wq_m, wk_m, b_igate, b_fgate,
           hnorm_g, w_out, norm2_g, w_up, w_down):
    depth = w_in.shape[0]
    d = x_prompt.shape[-1]
    mix = w_out.shape[1]
    aw = mix // 2
    mw = mix - aw
    kvw = A_KV_HEADS * A_HEAD_DIM
    n_m_heads = mw // M_HEAD_DIM
    splits = (aw, kvw, kvw, IDX_HEADS * IDX_DIM, IDX_DIM, IDX_HEADS, mw, mw, mw, n_m_heads, n_m_heads)
    t_p = x_prompt.shape[1]
    b_s, t_s = x_sample.shape[0], x_sample.shape[1]
    past = cache_k.shape[2]

    y_p, y_s = x_prompt, x_sample
    new_p, new_s = [], []
    for l in range(depth):
        params = (norm1_g[l], w_in[l], q_norm_g[l], k_norm_g[l], conv_w[l], conv_b[l], wq_m[l], wk_m[l],
                  b_igate[l], b_fgate[l], hnorm_g[l], w_out[l], norm2_g[l], w_up[l], w_down[l])
        y_p, st_p = _layer(
            y_p, jnp.arange(t_p), "prompt", None, params, splits)
        y_s, st_s = _layer(
            y_s, past + (jnp.arange(b_s * t_s) % t_s), "sample",
            (cache_k[l], cache_v[l], cache_kidx[l], state_mlstm_C[l], state_mlstm_n[l],
             state_mlstm_m[l], state_conv[l]), params, splits)
        new_p.append(st_p)
        new_s.append(st_s)

    stack = lambda states, i: jnp.stack([s[i] for s in states])
    return (y_p, y_s,
            *[stack(new_p, i) for i in range(7)],
            *[stack(new_s, i) for i in range(7)])
```
